```python
import math
import jax, jax.numpy as jnp
from jax import lax
import numpy as np

D_MODEL = 1024
BATCH = 16
SEQ = 2048
DEPTH = 1

PLE_DIM = 256
RWKV_HEADS = 8
RWKV_HEAD_DIM = 64
RWKV_WIDTH = RWKV_HEADS * RWKV_HEAD_DIM
DECAY_LORA = 64
AAA_LORA = 64
GATE_LORA = 128
RWKV_COLS = 3 * RWKV_WIDTH + DECAY_LORA + AAA_LORA + GATE_LORA
S5_GROUPS = 16
S5_GROUP_CH = 16
S5_WIDTH = S5_GROUPS * S5_GROUP_CH
S5_STATE = 64
IN_COLS = RWKV_COLS + S5_WIDTH + 2 * D_MODEL
N_GROUPS = 4
EXPERTS_PER_GROUP = 8
N_EXPERTS = N_GROUPS * EXPERTS_PER_GROUP
TOP_K = 2
D_EXPERT = 256

NORM_EPS = 1e-6
GN_EPS = 64e-5

kernel_name = "hybrid_rwkv7_s5_hmoe_block"


def rms_norm(x, g):
    xf = x.astype(jnp.float32)
    y = xf * lax.rsqrt(jnp.mean(xf * xf, axis=-1, keepdims=True) + NORM_EPS)
    return (y * g.astype(jnp.float32)).astype(x.dtype)


def wkv7_scan(r, decay, k, v, a, b):
    bsz, _, h, n = r.shape

    def step(state, xs):
        r_t, w_t, k_t, v_t, a_t, b_t = xs
        sa = jnp.einsum('bhvk,bhk->bhv', state, a_t)
        state = (state * w_t[:, :, None, :]
                 + sa[..., :, None] * b_t[:, :, None, :]
                 + v_t[..., :, None] * k_t[:, :, None, :])
        return state, jnp.einsum('bhvk,bhk->bhv', state, r_t)

    xs = tuple(jnp.swapaxes(t, 0, 1) for t in (r, decay, k, v, a, b))
    state0 = jnp.zeros((bsz, h, n, n), jnp.float32)
    _, y = lax.scan(step, state0, xs)
    return jnp.swapaxes(y, 0, 1)


def rwkv7_branch(cols, w0, w_up, a0, a_up, g_up, k_k, k_a, r_k, ln_g, ln_b):
    bsz, s, _ = cols.shape
    W = RWKV_WIDTH
    r, k, v, wd, ad, gd = jnp.split(
        cols, [W, 2 * W, 3 * W, 3 * W + DECAY_LORA, 3 * W + DECAY_LORA + AAA_LORA], axis=-1)
    w_log = -jax.nn.softplus(-(w0 + jnp.tanh(wd) @ w_up)) - 0.5
    decay = jnp.exp(-jnp.exp(w_log.astype(jnp.float32)))
    a = jax.nn.sigmoid(a0 + ad @ a_up)
    g = jax.nn.sigmoid(gd) @ g_up
    heads = lambda t: t.astype(jnp.float32).reshape(bsz, s, RWKV_HEADS, RWKV_HEAD_DIM)
    kk = heads(k * k_k)
    kk = kk / jnp.maximum(jnp.sqrt(jnp.sum(kk * kk, axis=-1, keepdims=True)), 1e-12)
    k = k * (1.0 + (a - 1.0) * k_a)
    r_h, k_h, v_h, a_h, w_h = heads(r), heads(k), heads(v), heads(a), heads(decay)
    y = wkv7_scan(r_h, w_h, k_h, v_h, -kk, kk * a_h)
    mu = jnp.mean(y, axis=-1, keepdims=True)
    var = jnp.mean(jnp.square(y - mu), axis=-1, keepdims=True)
    y = ((y - mu) * lax.rsqrt(var + GN_EPS)).reshape(bsz, s, W)
    y = y * ln_g.astype(jnp.float32) + ln_b.astype(jnp.float32)
    bonus = jnp.sum(r_h * k_h * r_k.astype(jnp.float32), axis=-1, keepdims=True) * v_h
    y = y + bonus.reshape(bsz, s, W)
    return (y * g.astype(jnp.float32)).astype(cols.dtype)


def complex_linear_combine(e1, e2):
    a1r, a1i, b1r, b1i = e1
    a2r, a2i, b2r, b2i = e2
    ar = a2r * a1r - a2i * a1i
    ai = a2r * a1i + a2i * a1r
    br = a2r * b1r - a2i * b1i + b2r
    bi = a2r * b1i + a2i * b1r + b2i
    return ar, ai, br, bi


def s5_branch(u, lam_re, lam_im, log_dt, b_re, b_im, c_re, c_im, d_skip, glu_w, glu_b):
    bsz, s, _ = u.shape
    uf = u.astype(jnp.float32).reshape(bsz, s, S5_GROUPS, S5_GROUP_CH)
    dt = jnp.exp(log_dt.astype(jnp.float32))[:, None]
    lr, li = lam_re.astype(jnp.float32), lam_im.astype(jnp.float32)
    mag = jnp.exp(lr * dt)
    lb_re, lb_im = mag * jnp.cos(li * dt), mag * jnp.sin(li * dt)
    den = lr * lr + li * li
    nr, ni = lb_re - 1.0, lb_im
    coef_re = (nr * lr + ni * li) / den
    coef_im = (ni * lr - nr * li) / den
    br, bi = b_re.astype(jnp.float32), b_im.astype(jnp.float32)
    bb_re = coef_re[..., None] * br - coef_im[..., None] * bi
    bb_im = coef_re[..., None] * bi + coef_im[..., None] * br
    bu_re = jnp.einsum('bsgc,gnc->bsgn', uf, bb_re)
    bu_im = jnp.einsum('bsgc,gnc->bsgn', uf, bb_im)
    a_re = jnp.broadcast_to(lb_re, bu_re.shape)
    a_im = jnp.broadcast_to(lb_im, bu_im.shape)
    _, _, x_re, x_im = lax.associative_scan(
        complex_linear_combine, (a_re, a_im, bu_re, bu_im), axis=1)
    y = (jnp.einsum('bsgn,gcn->bsgc', x_re, c_re.astype(jnp.float32))
         - jnp.einsum('bsgn,gcn->bsgc', x_im, c_im.astype(jnp.float32))
         + d_skip.astype(jnp.float32) * uf)
    z = jax.nn.gelu(y.reshape(bsz, s, S5_WIDTH)).astype(u.dtype)
    return z * jax.nn.sigmoid(z @ glu_w + glu_b)


def hier_moe(h, rg_w, rg_b, re_w, re_b, w_gate, w_up, w_down):
    bsz, s, dm = h.shape
    t = h.reshape(-1, dm)
    tn = t.shape[0]
    g_prob = jax.nn.softmax((t @ rg_w + rg_b).astype(jnp.float32), axis=-1)
    g_p, g_idx = lax.top_k(g_prob, 1)
    e_logits = (t @ re_w + re_b).astype(jnp.float32).reshape(tn, N_GROUPS, EXPERTS_PER_GROUP)
    e_in_group = jnp.take_along_axis(e_logits, g_idx[:, :, None], axis=1)[:, 0]
    e_top, e_idx = lax.top_k(e_in_group, TOP_K)
    e_w = jax.nn.softmax(e_top, axis=-1) * g_p
    expert_id = g_idx * EXPERTS_PER_GROUP + e_idx
    comb = jnp.sum(jax.nn.one_hot(expert_id, N_EXPERTS, dtype=jnp.float32) * e_w[..., None],
                   axis=1)
    out = jnp.zeros((tn, dm), jnp.float32)
    for e in range(N_EXPERTS):
        hid = jax.nn.silu(t @ w_gate[e]) * (t @ w_up[e])
        out = out + comb[:, e:e + 1] * (hid @ w_down[e]).astype(jnp.float32)
    return out.reshape(bsz, s, dm).astype(h.dtype)


def setup_inputs(seed: int = 0) -> dict:
    key = jax.random.key(seed)
    ks = iter(jax.random.split(key, 48))
    L, D = DEPTH, D_MODEL
    nrm = lambda shape, std: std * jax.random.normal(next(ks), shape, jnp.float32)
    unif = lambda shape, lo, hi: jax.random.uniform(next(ks), shape, jnp.float32, lo, hi)
    n_idx = jnp.arange(S5_STATE, dtype=jnp.float32)
    return {
        "x": nrm((BATCH, SEQ, D), 1.0),
        "p": nrm((L, BATCH, SEQ, PLE_DIM), 1.0),
        "mix_norm": 1.0 + nrm((L, D), 0.02),
        "w_in": nrm((L, D, IN_COLS), D ** -0.5),
        "mu_shift": unif((L, RWKV_COLS), 0.0, 1.0),
        "rk_w0": unif((L, RWKV_WIDTH), -6.0, 1.0),
        "rk_w_up": nrm((L, DECAY_LORA, RWKV_WIDTH), DECAY_LORA ** -0.5),
        "rk_a0": nrm((L, RWKV_WIDTH), 0.1),
        "rk_a_up": nrm((L, AAA_LORA, RWKV_WIDTH), AAA_LORA ** -0.5),
        "rk_g_up": nrm((L, GATE_LORA, RWKV_WIDTH), GATE_LORA ** -0.5),
        "rk_k_k": 0.85 + nrm((L, RWKV_WIDTH), 0.02),
        "rk_k_a": 1.0 + nrm((L, RWKV_WIDTH), 0.02),
        "rk_r_k": nrm((L, RWKV_HEADS, RWKV_HEAD_DIM), 0.1),
        "rk_ln_g": 1.0 + nrm((L, RWKV_WIDTH), 0.02),
        "rk_ln_b": nrm((L, RWKV_WIDTH), 0.02),
        "s5_lam_re": -0.5 + nrm((L, S5_GROUPS, S5_STATE), 0.01),
        "s5_lam_im": math.pi * n_idx + nrm((L, S5_GROUPS, S5_STATE), 0.01),
        "s5_log_dt": unif((L, S5_GROUPS), math.log(1e-3), math.log(1e-1)),
        "s5_b_re": nrm((L, S5_GROUPS, S5_STATE, S5_GROUP_CH), (0.5 / S5_GROUP_CH) ** 0.5),
        "s5_b_im": nrm((L, S5_GROUPS, S5_STATE, S5_GROUP_CH), (0.5 / S5_GROUP_CH) ** 0.5),
        "s5_c_re": nrm((L, S5_GROUPS, S5_GROUP_CH, S5_STATE), 0.5 ** 0.5),
        "s5_c_im": nrm((L, S5_GROUPS, S5_GROUP_CH, S5_STATE), 0.5 ** 0.5),
        "s5_d": nrm((L, S5_GROUPS, S5_GROUP_CH), 1.0),
        "s5_glu_w": nrm((L, S5_WIDTH, S5_WIDTH), S5_WIDTH ** -0.5),
        "s5_glu_b": nrm((L, S5_WIDTH), 0.02),
        "w_branch_a": nrm((L, RWKV_WIDTH, D), RWKV_WIDTH ** -0.5),
        "w_branch_b": nrm((L, S5_WIDTH, D), S5_WIDTH ** -0.5),
        "w_out": nrm((L, D, D), D ** -0.5),
        "ffn_norm": 1.0 + nrm((L, D), 0.02),
        "router_group_w": nrm((L, D, N_GROUPS), D ** -0.5),
        "router_group_b": nrm((L, N_GROUPS), 0.01),
        "router_expert_w": nrm((L, D, N_EXPERTS), D ** -0.5),
        "router_expert_b": nrm((L, N_EXPERTS), 0.01),
        "exp_w_gate": nrm((L, N_EXPERTS, D, D_EXPERT), D ** -0.5),
        "exp_w_up": nrm((L, N_EXPERTS, D, D_EXPERT), D ** -0.5),
        "exp_w_down": nrm((L, N_EXPERTS, D_EXPERT, D), D_EXPERT ** -0.5),
        "ple_norm": 1.0 + nrm((L, D), 0.02),
        "ple_gate_w": nrm((L, D, D), D ** -0.5),
        "ple_proj": nrm((L, PLE_DIM, D), PLE_DIM ** -0.5),
        "final_norm": 1.0 + nrm((D,), 0.02),
    }


def reference(x, p, mix_norm, w_in, mu_shift, rk_w0, rk_w_up, rk_a0, rk_a_up, rk_g_up,
              rk_k_k, rk_k_a, rk_r_k, rk_ln_g, rk_ln_b, s5_lam_re, s5_lam_im, s5_log_dt,
              s5_b_re, s5_b_im, s5_c_re, s5_c_im, s5_d, s5_glu_w, s5_glu_b,
              w_branch_a, w_branch_b, w_out, ffn_norm, router_group_w, router_group_b,
              router_expert_w, router_expert_b, exp_w_gate, exp_w_up, exp_w_down,
              ple_norm, ple_gate_w, ple_proj, final_norm):
    for i in range(DEPTH):
        h = rms_norm(x, mix_norm[i])
        cols = h @ w_in[i]
        c_rwkv, u_s5, gate_a, gate_b = jnp.split(
            cols, [RWKV_COLS, RWKV_COLS + S5_WIDTH, RWKV_COLS + S5_WIDTH + D_MODEL], axis=-1)
        prev = jnp.pad(c_rwkv, ((0, 0), (1, 0), (0, 0)))[:, :-1]
        c_rwkv = c_rwkv + (prev - c_rwkv) * mu_shift[i]
        y_a = rwkv7_branch(c_rwkv, rk_w0[i], rk_w_up[i], rk_a0[i], rk_a_up[i], rk_g_up[i],
                           rk_k_k[i], rk_k_a[i], rk_r_k[i], rk_ln_g[i], rk_ln_b[i]) @ w_branch_a[i]
        y_b = s5_branch(u_s5, s5_lam_re[i], s5_lam_im[i], s5_log_dt[i], s5_b_re[i], s5_b_im[i],
                        s5_c_re[i], s5_c_im[i], s5_d[i], s5_glu_w[i], s5_glu_b[i]) @ w_branch_b[i]
        merged = jax.nn.sigmoid(gate_a) * y_a + jax.nn.sigmoid(gate_b) * y_b
        x = x + merged @ w_out[i]
        x = x + hier_moe(rms_norm(x, ffn_norm[i]), router_group_w[i], router_group_b[i],
                         router_expert_w[i], router_expert_b[i],
                         exp_w_gate[i], exp_w_up[i], exp_w_down[i])
        hp = rms_norm(x, ple_norm[i])
        x = x + jax.nn.sigmoid(hp @ ple_gate_w[i]) * (p[i] @ ple_proj[i])
    return rms_norm(x, final_norm)
```

```python
import functools
import math

import jax
import jax.numpy as jnp
from jax import lax
from jax.experimental import pallas as pl
from jax.experimental.pallas import tpu as pltpu

F32 = jnp.float32
BF16 = jnp.bfloat16

NORM_EPS = 1e-6
GN_EPS = 64e-5

RWKV_HEADS = 8
RWKV_HEAD_DIM = 64
RWKV_WIDTH = RWKV_HEADS * RWKV_HEAD_DIM
DECAY_LORA = 64
AAA_LORA = 64
GATE_LORA = 128
LORA_COLS = DECAY_LORA + AAA_LORA + GATE_LORA
RWKV_COLS = 3 * RWKV_WIDTH + LORA_COLS
S5_GROUPS = 16
S5_GROUP_CH = 16
S5_WIDTH = S5_GROUPS * S5_GROUP_CH
S5_STATE = 64
S5_ZW = 2 * S5_GROUPS * S5_STATE
N_GROUPS = 4
EXPERTS_PER_GROUP = 8
N_EXPERTS = N_GROUPS * EXPERTS_PER_GROUP

LANES = 128
RWKV_CHUNK = 64
RWKV_INV_BLOCK = 16
S5_CHUNK = 8
ROUTER_LANES = 128
VMEM_LIMIT = 56 * 1024 * 1024


def _mm(a, b):
    return jnp.dot(a.astype(BF16), b.astype(BF16), preferred_element_type=F32)


def _mm_nt(a, b):
    return lax.dot_general(a.astype(BF16), b.astype(BF16), (((1,), (1,)), ((), ())),
                           preferred_element_type=F32)


def _mm_tn(a, b):
    return lax.dot_general(a.astype(BF16), b.astype(BF16), (((0,), (0,)), ((), ())),
                           preferred_element_type=F32)


def _split3(x):
    hi = x.astype(BF16)
    r1 = x - hi.astype(F32)
    mid = r1.astype(BF16)
    lo = (r1 - mid.astype(F32)).astype(BF16)
    return hi, mid, lo


def _mm_exact_lhs(a_bf16, x):
    hi, mid, lo = _split3(x)
    d = lambda t: jnp.dot(a_bf16, t, preferred_element_type=F32)
    return d(hi) + d(mid) + d(lo)


def _mm_exact_rhs(x, b_bf16):
    hi, mid, lo = _split3(x)
    d = lambda t: jnp.dot(t, b_bf16, preferred_element_type=F32)
    return d(hi) + d(mid) + d(lo)


def _sigmoid(x):
    return 1.0 / (1.0 + jnp.exp(-x))


def _rms_norm(x, g):
    ms = jnp.mean(x * x, axis=-1, keepdims=True)
    return x * lax.rsqrt(ms + NORM_EPS) * g


def _in_proj_kernel(x_ref, g_ref, w_ref, crw_ref, us5_ref, gates_ref):
    h = _rms_norm(x_ref[...], g_ref[...]).astype(BF16)
    c0, c1 = RWKV_COLS, RWKV_COLS + S5_WIDTH
    crw_ref[...] = jnp.dot(h, w_ref[:, :c0], preferred_element_type=F32)
    us5_ref[...] = jnp.dot(h, w_ref[:, c0:c1], preferred_element_type=F32)
    gates_ref[...] = jnp.dot(h, w_ref[:, c1:], preferred_element_type=F32).astype(BF16)


def _in_proj(x2, g, w, tm):
    t, d = x2.shape
    n = w.shape[1]
    ng = n - RWKV_COLS - S5_WIDTH
    return pl.pallas_call(
        _in_proj_kernel,
        grid=(t // tm,),
        in_specs=[pl.BlockSpec((tm, d), lambda i: (i, 0)),
                  pl.BlockSpec((1, d), lambda i: (0, 0)),
                  pl.BlockSpec((d, n), lambda i: (0, 0))],
        out_specs=[pl.BlockSpec((tm, RWKV_COLS), lambda i: (i, 0)),
                   pl.BlockSpec((tm, S5_WIDTH), lambda i: (i, 0)),
                   pl.BlockSpec((tm, ng), lambda i: (i, 0))],
        out_shape=[jax.ShapeDtypeStruct((t, RWKV_COLS), F32),
                   jax.ShapeDtypeStruct((t, S5_WIDTH), F32),
                   jax.ShapeDtypeStruct((t, ng), BF16)],
        compiler_params=pltpu.CompilerParams(dimension_semantics=("arbitrary",),
                                             vmem_limit_bytes=VMEM_LIMIT),
        name="in_proj",
    )(x2, g, w)


def _rwkv_kernel(c_ref, mu_ref, wl_ref, w0_ref, a0_ref, kk_ref, ka_ref, rk_ref, lng_ref, lnb_ref,
                 ones_ref, tril_ref, o_ref, carry_ref, s_ref):
    C = RWKV_CHUNK
    W = RWKV_WIDTH
    npairs = W // LANES

    @pl.when(pl.program_id(1) == 0)
    def _():
        carry_ref[...] = jnp.zeros_like(carry_ref)
        s_ref[...] = jnp.zeros_like(s_ref)

    c = c_ref[...]
    row = lax.broadcasted_iota(jnp.int32, (C, 1), 0)
    prev = jnp.where(row == 0, carry_ref[...], pltpu.roll(c, 1, 0))
    carry_ref[...] = c[C - 1:C, :]
    cs = c + (prev - c) * mu_ref[...]

    r = cs[:, 0:W]
    k = cs[:, W:2 * W]
    v = cs[:, 2 * W:3 * W]
    lin = cs[:, 3 * W:]
    llane = lax.broadcasted_iota(jnp.int32, lin.shape, 1)
    lact = jnp.where(llane < DECAY_LORA, jnp.tanh(lin),
                     jnp.where(llane < DECAY_LORA + AAA_LORA, lin, _sigmoid(lin)))
    lo = _mm(lact, wl_ref[...])
    zw = -(w0_ref[...] + lo[:, 0:W])
    softplus = jnp.maximum(zw, 0.0) + jnp.log(1.0 + jnp.exp(-jnp.abs(zw)))
    ld = -jnp.exp(-softplus - 0.5)
    a = _sigmoid(a0_ref[...] + lo[:, W:2 * W])
    g = lo[:, 2 * W:3 * W]

    ones_bd = ones_ref[...]
    segsum = lambda t: _mm_exact_rhs(t, ones_bd)
    kk = k * kk_ref[...]
    kkn = kk / jnp.maximum(jnp.sqrt(segsum(kk * kk)), 1e-12)
    kmod = k * (1.0 + (a - 1.0) * ka_ref[...])

    cum = _mm_exact_lhs(tril_ref[...], ld)
    cum_last = cum[C - 1:C, :]
    inv = jnp.exp(-cum)
    tail = jnp.exp(cum_last - cum)
    At = -kkn * jnp.exp(cum - ld)
    Rt = r * jnp.exp(cum)
    kka = kkn * a
    Bt = kka * inv
    Kt = kmod * inv
    Bend = kka * tail
    Kend = kmod * tail
    pc = jnp.exp(cum_last)

    lane = lax.broadcasted_iota(jnp.int32, (C, LANES), 1)
    h0 = lane < RWKV_HEAD_DIM
    split = lambda t: jnp.concatenate([jnp.where(h0, t, 0.0), jnp.where(h0, 0.0, t)], axis=0)
    grow = lax.broadcasted_iota(jnp.int32, (C, 4 * C), 0)
    gcol = lax.broadcasted_iota(jnp.int32, (C, 4 * C), 1) & (C - 1)
    r2 = lax.broadcasted_iota(jnp.int32, (2 * C, 2 * C), 0)
    c2 = lax.broadcasted_iota(jnp.int32, (2 * C, 2 * C), 1)
    eye = (r2 == c2).astype(F32)
    blk_shift = int(math.log2(RWKV_INV_BLOCK))
    same_blk = (r2 >> blk_shift) == (c2 >> blk_shift)
    same_head = (r2 < C) == (c2 < C)

    ys = []
    for p in range(npairs):
        sl = slice(p * LANES, (p + 1) * LANES)
        lhs = jnp.concatenate([At[:, sl], Rt[:, sl]], axis=0)
        rhs = jnp.concatenate([split(Bt[:, sl]), split(Kt[:, sl])], axis=0)
        G = _mm_nt(lhs, rhs)
        a_row = jnp.where(gcol < grow, G[:C], 0.0)
        m_row = jnp.where(gcol <= grow, G[C:], 0.0)
        a_bd = split(a_row[:, :2 * C])

        a_d = jnp.where(same_blk, a_bd, 0.0)
        a_off = a_bd - a_d
        dinv = eye + a_d
        pw = a_d
        for _ in range(int(math.log2(RWKV_INV_BLOCK)) - 1):
            pw = _mm(pw, pw)
            dinv = dinv + _mm(dinv, pw)
        n1 = _mm(dinv, a_off)
        n2 = _mm(n1, n1)
        tinv = _mm(eye + n1 + n2 + _mm(n1, n2), dinv)

        vp = v[:, sl]
        v_st = split(vp)
        s_old = s_ref[p]
        a_s = _mm_nt(lhs, s_old)
        u_st = _mm(tinv, split(a_s[:C] + _mm(a_row[:, 2 * C:], v_st)))
        u_lp = u_st[:C] + u_st[C:]
        ys.append(a_s[C:] + _mm(m_row, jnp.concatenate([u_st, v_st], axis=0)))
        upd = _mm_tn(jnp.concatenate([u_lp, vp], axis=0),
                     jnp.concatenate([Bend[:, sl], Kend[:, sl]], axis=0))
        s_ref[p] = s_old * pc[:, sl] + jnp.where(same_head, upd, 0.0)

    y = jnp.concatenate(ys, axis=1)
    inv_n = 1.0 / RWKV_HEAD_DIM
    mean = segsum(y) * inv_n
    d = y - mean
    var = segsum(d * d) * inv_n
    yn = d * lax.rsqrt(var + GN_EPS) * lng_ref[...] + lnb_ref[...]
    bonus = segsum(r * kmod * rk_ref[...]) * v
    o_ref[...] = ((yn + bonus) * g).astype(o_ref.dtype)


def _rwkv(crw3, mu, wl, w0, a0, k_k, k_a, r_k, ln_g, ln_b, ones_bd, tril):
    b, s, _ = crw3.shape
    C = RWKV_CHUNK
    W = RWKV_WIDTH
    vec = lambda n: pl.BlockSpec((1, n), lambda i, j: (0, 0))
    return pl.pallas_call(
        _rwkv_kernel,
        grid=(b, s // C),
        in_specs=[pl.BlockSpec((None, C, RWKV_COLS), lambda i, j: (i, j, 0)),
                  vec(RWKV_COLS),
                  pl.BlockSpec((LORA_COLS, 3 * W), lambda i, j: (0, 0)),
                  vec(W), vec(W), vec(W), vec(W), vec(W), vec(W), vec(W),
                  pl.BlockSpec((W, W), lambda i, j: (0, 0)),
                  pl.BlockSpec((C, C), lambda i, j: (0, 0))],
        out_specs=pl.BlockSpec((None, C, W), lambda i, j: (i, j, 0)),
        out_shape=jax.ShapeDtypeStruct((b, s, W), BF16),
        scratch_shapes=[pltpu.VMEM((1, RWKV_COLS), F32),
                        pltpu.VMEM((W // LANES, LANES, LANES), F32)],
        compiler_params=pltpu.CompilerParams(dimension_semantics=("arbitrary", "arbitrary"),
                                             vmem_limit_bytes=VMEM_LIMIT),
        name="rwkv",
    )(crw3, mu, wl, w0, a0, k_k, k_a, r_k, ln_g, ln_b, ones_bd, tril)


def _s5_mats(lam_re, lam_im, log_dt, b_re, b_im, c_re, c_im, d_skip):
    L = S5_CHUNK
    G, N = lam_re.shape
    ch = b_re.shape[-1]
    dt = jnp.exp(log_dt)[:, None]
    lr, li = lam_re, lam_im
    mag = jnp.exp(lr * dt)
    lb_re, lb_im = mag * jnp.cos(li * dt), mag * jnp.sin(li * dt)
    den = lr * lr + li * li
    nr, ni = lb_re - 1.0, lb_im
    coef_re = (nr * lr + ni * li) / den
    coef_im = (ni * lr - nr * li) / den
    bb_re = coef_re[..., None] * b_re - coef_im[..., None] * b_im
    bb_im = coef_re[..., None] * b_im + coef_im[..., None] * b_re
    prs, pis = [jnp.ones_like(lb_re)], [jnp.zeros_like(lb_im)]
    for _ in range(L):
        pr_, pi_ = prs[-1], pis[-1]
        prs.append(pr_ * lb_re - pi_ * lb_im)
        pis.append(pr_ * lb_im + pi_ * lb_re)
    pr = jnp.stack(prs)
    pi = jnp.stack(pis)
    eye_g = jnp.eye(G, dtype=F32)

    def lam_bb(qr, qi):
        return (qr[..., None] * bb_re[None] - qi[..., None] * bb_im[None],
                qr[..., None] * bb_im[None] + qi[..., None] * bb_re[None])

    lre, lim = lam_bb(pr[:L], pi[:L])
    hp = lax.Precision.HIGHEST
    kern = (jnp.einsum('gon,lgni->lgio', c_re, lre, precision=hp)
            - jnp.einsum('gon,lgni->lgio', c_im, lim, precision=hp))
    bdk = (kern[:, :, :, None, :] * eye_g[None, :, None, :, None]).reshape(L, G * ch, G * ch)
    bdk = bdk.at[0].add(jnp.diag(d_skip.reshape(-1)))
    wre, wim = lam_bb(pr[:L][::-1], pi[:L][::-1])
    expand_in = lambda t: jnp.swapaxes(t, 2, 3)[:, :, :, None, :] * eye_g[None, :, None, :, None]
    w_in = jnp.stack([expand_in(wre), expand_in(wim)], axis=3).reshape(L, G * ch, 2 * G * N)
    qr, qi = pr[1:L + 1][:, :, None, :], pi[1:L + 1][:, :, None, :]
    o_re = c_re[None] * qr - c_im[None] * qi
    o_im = -c_re[None] * qi - c_im[None] * qr
    expand_out = lambda t: jnp.swapaxes(t, 2, 3)[:, :, :, None, :] * eye_g[None, :, None, :, None]
    w_out = jnp.stack([expand_out(o_re), expand_out(o_im)], axis=1).reshape(L, 2 * G * N, G * ch)
    plr = pr[L].reshape(1, G * N)
    pli = pi[L].reshape(1, G * N)
    lag = jnp.arange(L)[None, :] - jnp.arange(L)[:, None]
    toep = jnp.where((lag >= 0)[:, :, None, None], bdk[jnp.maximum(lag, 0)], 0.0)
    toep = jnp.swapaxes(toep, 1, 2).reshape(L * G * ch, L * G * ch)
    w_in_flat = w_in.reshape(L * G * ch, 2 * G * N)
    w_out_flat = jnp.swapaxes(w_out, 0, 1).reshape(2 * G * N, L * G * ch)
    w_a = jnp.concatenate([w_in_flat, toep], axis=1)
    return w_a.astype(BF16), w_out_flat.astype(BF16), plr, pli


def _s5_kernel(u_ref, wa_ref, wo_ref, plr_ref, pli_ref, o_ref, wloc_ref, zprev_ref):
    nch = u_ref.shape[0]
    half = S5_ZW // 2

    r = _mm(u_ref[...], wa_ref[...])
    wloc_ref[...] = r[:, :S5_ZW]
    y_lag = r[:, S5_ZW:]

    plr = plr_ref[...]
    pli = pli_ref[...]

    def step(ci, z):
        zprev_ref[pl.ds(ci, 1), :] = z
        zr, zi = z[:, :half], z[:, half:]
        nz = jnp.concatenate([plr * zr - pli * zi, plr * zi + pli * zr], axis=1)
        return nz + wloc_ref[pl.ds(ci, 1), :]

    lax.fori_loop(0, nch, step, jnp.zeros((1, S5_ZW), F32))
    o_ref[...] = y_lag + _mm(zprev_ref[...], wo_ref[...])


def _s5(u_flat, w_a, w_o, plr, pli):
    b, nch, fw = u_flat.shape
    const = lambda a: pl.BlockSpec(a.shape, lambda i: (0,) * a.ndim, pipeline_mode=pl.Buffered(1))
    return pl.pallas_call(
        _s5_kernel,
        grid=(b,),
        in_specs=[pl.BlockSpec((None, nch, fw), lambda i: (i, 0, 0)),
                  const(w_a), const(w_o), const(plr), const(pli)],
        out_specs=pl.BlockSpec((None, nch, fw), lambda i: (i, 0, 0)),
        out_shape=jax.ShapeDtypeStruct((b, nch, fw), F32),
        scratch_shapes=[pltpu.VMEM((nch, S5_ZW), F32),
                        pltpu.VMEM((nch, S5_ZW), F32)],
        compiler_params=pltpu.CompilerParams(dimension_semantics=("arbitrary",),
                                             vmem_limit_bytes=VMEM_LIMIT),
        name="s5",
    )(u_flat, w_a, w_o, plr, pli)


def _merge_kernel(x_ref, ya_ref, yb_ref, gates_ref, gluw_ref, glub_ref, wba_ref, wbb_ref, wout_ref,
                  fng_ref, rw_ref, rb_ref, x1_ref, t_ref, comb_ref):
    d = x_ref.shape[1]
    y_a = jnp.dot(ya_ref[...], wba_ref[...], preferred_element_type=F32)
    ys = yb_ref[...]
    z = 0.5 * ys * (1.0 + jnp.tanh(math.sqrt(2.0 / math.pi) * (ys + 0.044715 * (ys * ys * ys))))
    z = z * _sigmoid(_mm(z, gluw_ref[...]) + glub_ref[...])
    y_b = _mm(z, wbb_ref[...])
    gates = gates_ref[...].astype(F32)
    merged = _sigmoid(gates[:, :d]) * y_a + _sigmoid(gates[:, d:]) * y_b
    x1 = x_ref[...] + _mm(merged, wout_ref[...])
    x1_ref[...] = x1
    t = _rms_norm(x1, fng_ref[...])
    t_ref[...] = t.astype(BF16)

    logits = jnp.dot(t, rw_ref[...], precision=lax.Precision.HIGHEST,
                     preferred_element_type=F32) + rb_ref[...]
    lane = lax.broadcasted_iota(jnp.int32, logits.shape, 1)
    neg = -jnp.inf
    big = jnp.int32(1 << 20)
    is_g = (lane >= N_EXPERTS) & (lane < N_EXPERTS + N_GROUPS)
    gl = jnp.where(is_g, logits, neg)
    gmax = jnp.max(gl, axis=-1, keepdims=True)
    g_p = 1.0 / jnp.sum(jnp.exp(gl - gmax), axis=-1, keepdims=True)
    g_idx = jnp.min(jnp.where(gl == gmax, lane - N_EXPERTS, big), axis=-1, keepdims=True)
    el = jnp.where((lane < N_EXPERTS) & ((lane >> int(math.log2(EXPERTS_PER_GROUP))) == g_idx), logits, neg)
    t1 = jnp.max(el, axis=-1, keepdims=True)
    i1 = jnp.min(jnp.where(el == t1, lane, big), axis=-1, keepdims=True)
    el2 = jnp.where(lane == i1, neg, el)
    t2 = jnp.max(el2, axis=-1, keepdims=True)
    i2 = jnp.min(jnp.where(el2 == t2, lane, big), axis=-1, keepdims=True)
    e21 = jnp.exp(t2 - t1)
    w1 = g_p / (1.0 + e21)
    w2 = g_p * e21 / (1.0 + e21)
    comb_ref[...] = jnp.where(lane == i1, w1, 0.0) + jnp.where(lane == i2, w2, 0.0)


def _merge(x2, ya, yb, gates, gluw, glub, wba, wbb, wout, fng, rw, rb, tm):
    t, d = x2.shape
    full = lambda a: pl.BlockSpec(a.shape, lambda i: (0,) * a.ndim)
    rowblk = lambda n: pl.BlockSpec((tm, n), lambda i: (i, 0))
    return pl.pallas_call(
        _merge_kernel,
        grid=(t // tm,),
        in_specs=[rowblk(d), rowblk(ya.shape[1]), rowblk(yb.shape[1]), rowblk(gates.shape[1]),
                  full(gluw), full(glub), full(wba), full(wbb), full(wout), full(fng), full(rw),
                  full(rb)],
        out_specs=[rowblk(d), rowblk(d), rowblk(ROUTER_LANES)],
        out_shape=[jax.ShapeDtypeStruct((t, d), F32),
                   jax.ShapeDtypeStruct((t, d), BF16),
                   jax.ShapeDtypeStruct((t, ROUTER_LANES), F32)],
        compiler_params=pltpu.CompilerParams(dimension_semantics=("arbitrary",),
                                             vmem_limit_bytes=VMEM_LIMIT),
        name="merge",
    )(x2, ya, yb, gates, gluw, glub, wba, wbb, wout, fng, rw, rb)


def _moe_kernel(t_ref, comb_ref, x1_ref, wg_ref, wu_ref, wd_ref, o_ref, acc_ref):
    e = pl.program_id(1)

    @pl.when(e == 0)
    def _():
        acc_ref[...] = jnp.zeros_like(acc_ref)

    t = t_ref[...]
    hg = jnp.dot(t, wg_ref[...], preferred_element_type=F32)
    hu = jnp.dot(t, wu_ref[...], preferred_element_type=F32)
    comb = comb_ref[...]
    lane = lax.broadcasted_iota(jnp.int32, comb.shape, 1)
    ce = jnp.sum(jnp.where(lane == e, comb, 0.0), axis=-1, keepdims=True)
    hid = hg * _sigmoid(hg) * hu * ce
    acc_ref[...] += _mm(hid, wd_ref[...])

    @pl.when(e == pl.num_programs(1) - 1)
    def _():
        o_ref[...] = x1_ref[...] + acc_ref[...]


def _moe(t_bf, comb, x1, wg, wu, wd, tm):
    t, d = x1.shape
    ne, _, de = wg.shape
    return pl.pallas_call(
        _moe_kernel,
        grid=(t // tm, ne),
        in_specs=[pl.BlockSpec((tm, d), lambda i, e: (i, 0)),
                  pl.BlockSpec((tm, ROUTER_LANES), lambda i, e: (i, 0)),
                  pl.BlockSpec((tm, d), lambda i, e: (i, 0)),
                  pl.BlockSpec((None, d, de), lambda i, e: (e, 0, 0)),
                  pl.BlockSpec((None, d, de), lambda i, e: (e, 0, 0)),
                  pl.BlockSpec((None, de, d), lambda i, e: (e, 0, 0))],
        out_specs=pl.BlockSpec((tm, d), lambda i, e: (i, 0)),
        out_shape=jax.ShapeDtypeStruct((t, d), F32),
        scratch_shapes=[pltpu.VMEM((tm, d), F32)],
        compiler_params=pltpu.CompilerParams(dimension_semantics=("arbitrary", "arbitrary"),
                                             vmem_limit_bytes=VMEM_LIMIT),
        name="moe",
    )(t_bf, comb, x1, wg, wu, wd)


def _ple_kernel(x_ref, p_ref, png_ref, wg_ref, wp_ref, fng_ref, o_ref):
    x2 = x_ref[...]
    hp = _rms_norm(x2, png_ref[...])
    gate = _sigmoid(_mm(hp, wg_ref[...]))
    x3 = x2 + gate * _mm(p_ref[...], wp_ref[...])
    o_ref[...] = _rms_norm(x3, fng_ref[...])


def _ple(x2, p2, png, wg, wp, fng, tm):
    t, d = x2.shape
    full = lambda a: pl.BlockSpec(a.shape, lambda i: (0,) * a.ndim)
    return pl.pallas_call(
        _ple_kernel,
        grid=(t // tm,),
        in_specs=[pl.BlockSpec((tm, d), lambda i: (i, 0)),
                  pl.BlockSpec((tm, p2.shape[1]), lambda i: (i, 0)),
                  full(png), full(wg), full(wp), full(fng)],
        out_specs=pl.BlockSpec((tm, d), lambda i: (i, 0)),
        out_shape=jax.ShapeDtypeStruct((t, d), F32),
        compiler_params=pltpu.CompilerParams(dimension_semantics=("arbitrary",),
                                             vmem_limit_bytes=VMEM_LIMIT),
        name="ple",
    )(x2, p2, png, wg, wp, fng)


def _row_tile(t, want):
    tm = min(want, t)
    while t % tm:
        tm //= 2
    return tm


def _layer(x, p, mix_norm, w_in, mu_shift, rk_w0, rk_w_up, rk_a0, rk_a_up, rk_g_up,
           rk_k_k, rk_k_a, rk_r_k, rk_ln_g, rk_ln_b, s5_lam_re, s5_lam_im, s5_log_dt,
           s5_b_re, s5_b_im, s5_c_re, s5_c_im, s5_d, s5_glu_w, s5_glu_b,
           w_branch_a, w_branch_b, w_out, ffn_norm, router_group_w, router_group_b,
           router_expert_w, router_expert_b, exp_w_gate, exp_w_up, exp_w_down,
           ple_norm, ple_gate_w, ple_proj):
    b, s, d = x.shape
    t = b * s
    W = RWKV_WIDTH
    row = lambda a: a.reshape(1, -1).astype(F32)
    x2 = x.reshape(t, d)

    crw, us5, gates = _in_proj(x2, row(mix_norm), w_in.astype(BF16), _row_tile(t, 256))

    wl = jnp.zeros((LORA_COLS, 3 * W), F32)
    wl = wl.at[:DECAY_LORA, :W].set(rk_w_up)
    wl = wl.at[DECAY_LORA:DECAY_LORA + AAA_LORA, W:2 * W].set(rk_a_up)
    wl = wl.at[DECAY_LORA + AAA_LORA:, 2 * W:].set(rk_g_up)
    hid = jnp.arange(W) // RWKV_HEAD_DIM
    ones_bd = (hid[:, None] == hid[None, :]).astype(BF16)
    tt = jnp.arange(RWKV_CHUNK)
    tril = (tt[None, :] <= tt[:, None]).astype(BF16)
    ya = _rwkv(crw.reshape(b, s, RWKV_COLS), row(mu_shift), wl.astype(BF16), row(rk_w0), row(rk_a0),
               row(rk_k_k), row(rk_k_a), row(rk_r_k), row(rk_ln_g), row(rk_ln_b), ones_bd, tril)

    s5_wa, s5_wo, plr, pli = _s5_mats(s5_lam_re, s5_lam_im, s5_log_dt, s5_b_re, s5_b_im,
                                      s5_c_re, s5_c_im, s5_d)
    yb = _s5(us5.reshape(b, s // S5_CHUNK, S5_CHUNK * S5_WIDTH), s5_wa, s5_wo, plr, pli)

    rw = jnp.zeros((d, ROUTER_LANES), F32)
    rw = rw.at[:, :N_EXPERTS].set(router_expert_w).at[:, N_EXPERTS:N_EXPERTS + N_GROUPS].set(router_group_w)
    rb = jnp.zeros((1, ROUTER_LANES), F32)
    rb = rb.at[0, :N_EXPERTS].set(router_expert_b).at[0, N_EXPERTS:N_EXPERTS + N_GROUPS].set(router_group_b)
    x1, t_bf, comb = _merge(x2, ya.reshape(t, W), yb.reshape(t, S5_WIDTH), gates,
                            s5_glu_w.astype(BF16), row(s5_glu_b), w_branch_a.astype(BF16), w_branch_b.astype(BF16), w_out.astype(BF16),
                            row(ffn_norm), rw, rb, _row_tile(t, 256))

    xm = _moe(t_bf, comb, x1, exp_w_gate.astype(BF16), exp_w_up.astype(BF16),
              exp_w_down.astype(BF16), _row_tile(t, 1024))
    return xm, p.reshape(t, -1), row(ple_norm), ple_gate_w.astype(BF16), ple_proj.astype(BF16)


def kernel(x, p, mix_norm, w_in, mu_shift, rk_w0, rk_w_up, rk_a0, rk_a_up, rk_g_up, rk_k_k, rk_k_a,
           rk_r_k, rk_ln_g, rk_ln_b, s5_lam_re, s5_lam_im, s5_log_dt, s5_b_re, s5_b_im, s5_c_re,
           s5_c_im, s5_d, s5_glu_w, s5_glu_b, w_branch_a, w_branch_b, w_out, ffn_norm,
           router_group_w, router_group_b, router_expert_w, router_expert_b, exp_w_gate, exp_w_up,
           exp_w_down, ple_norm, ple_gate_w, ple_proj, final_norm):
    b, s, d = x.shape
    depth = w_in.shape[0]
    assert depth == 1, "the final norm is fused into the last layer's PLE kernel"
    i = 0
    xm, p2, png, wpg, wpp = _layer(
        x, p[i], mix_norm[i], w_in[i], mu_shift[i], rk_w0[i], rk_w_up[i], rk_a0[i], rk_a_up[i],
        rk_g_up[i], rk_k_k[i], rk_k_a[i], rk_r_k[i], rk_ln_g[i], rk_ln_b[i], s5_lam_re[i],
        s5_lam_im[i], s5_log_dt[i], s5_b_re[i], s5_b_im[i], s5_c_re[i], s5_c_im[i], s5_d[i],
        s5_glu_w[i], s5_glu_b[i], w_branch_a[i], w_branch_b[i], w_out[i], ffn_norm[i],
        router_group_w[i], router_group_b[i], router_expert_w[i], router_expert_b[i],
        exp_w_gate[i], exp_w_up[i], exp_w_down[i], ple_norm[i], ple_gate_w[i], ple_proj[i])
    out = _ple(xm, p2, png, wpg, wpp, final_norm.reshape(1, -1).astype(F32), _row_tile(b * s, 512))
    return out.reshape(b, s, d)
```

```python
import functools
import math

import jax
import jax.numpy as jnp
from jax import lax
from jax.experimental import pallas as pl
from jax.experimental.pallas import tpu as pltpu

F32 = jnp.float32
BF16 = jnp.bfloat16

NORM_EPS = 1e-6
GN_EPS = 64e-5

RWKV_HEADS = 8
RWKV_HEAD_DIM = 64
RWKV_WIDTH = RWKV_HEADS * RWKV_HEAD_DIM
DECAY_LORA = 64
AAA_LORA = 64
GATE_LORA = 128
LORA_COLS = DECAY_LORA + AAA_LORA + GATE_LORA
RWKV_COLS = 3 * RWKV_WIDTH + LORA_COLS
S5_GROUPS = 16
S5_GROUP_CH = 16
S5_WIDTH = S5_GROUPS * S5_GROUP_CH
S5_STATE = 64
S5_ZW = 2 * S5_GROUPS * S5_STATE
N_GROUPS = 4
EXPERTS_PER_GROUP = 8
N_EXPERTS = N_GROUPS * EXPERTS_PER_GROUP

LANES = 128
RWKV_CHUNK = 64
RWKV_INV_BLOCK = 16
RWKV_BLOCK = 256
S5_CHUNK = 8
ROUTER_LANES = 128
VMEM_LIMIT = 56 * 1024 * 1024


def _mm(a, b):
    return jnp.dot(a.astype(BF16), b.astype(BF16), preferred_element_type=F32)


def _mm_nt(a, b):
    return lax.dot_general(a.astype(BF16), b.astype(BF16), (((1,), (1,)), ((), ())),
                           preferred_element_type=F32)


def _mm_tn(a, b):
    return lax.dot_general(a.astype(BF16), b.astype(BF16), (((0,), (0,)), ((), ())),
                           preferred_element_type=F32)


def _split3(x):
    hi = x.astype(BF16)
    r1 = x - hi.astype(F32)
    mid = r1.astype(BF16)
    lo = (r1 - mid.astype(F32)).astype(BF16)
    return hi, mid, lo


def _mm_exact_lhs(a_bf16, x):
    hi, mid, lo = _split3(x)
    d = lambda t: jnp.dot(a_bf16, t, preferred_element_type=F32)
    return d(hi) + d(mid) + d(lo)


def _mm_exact_rhs(x, b_bf16):
    hi, mid, lo = _split3(x)
    d = lambda t: jnp.dot(t, b_bf16, preferred_element_type=F32)
    return d(hi) + d(mid) + d(lo)


def _sigmoid(x):
    return 1.0 / (1.0 + jnp.exp(-x))


def _rms_norm(x, g):
    ms = jnp.mean(x * x, axis=-1, keepdims=True)
    return x * lax.rsqrt(ms + NORM_EPS) * g


def _in_proj_kernel(x_ref, g_ref, w_ref, crw_ref, us5_ref, gates_ref):
    h = _rms_norm(x_ref[...], g_ref[...]).astype(BF16)
    c0, c1 = RWKV_COLS, RWKV_COLS + S5_WIDTH
    crw_ref[...] = jnp.dot(h, w_ref[:, :c0], preferred_element_type=F32)
    us5_ref[...] = jnp.dot(h, w_ref[:, c0:c1], preferred_element_type=F32)
    gates_ref[...] = jnp.dot(h, w_ref[:, c1:], preferred_element_type=F32).astype(BF16)


def _in_proj(x2, g, w, tm):
    t, d = x2.shape
    n = w.shape[1]
    ng = n - RWKV_COLS - S5_WIDTH
    return pl.pallas_call(
        _in_proj_kernel,
        grid=(t // tm,),
        in_specs=[pl.BlockSpec((tm, d), lambda i: (i, 0)),
                  pl.BlockSpec((1, d), lambda i: (0, 0)),
                  pl.BlockSpec((d, n), lambda i: (0, 0))],
        out_specs=[pl.BlockSpec((tm, RWKV_COLS), lambda i: (i, 0)),
                   pl.BlockSpec((tm, S5_WIDTH), lambda i: (i, 0)),
                   pl.BlockSpec((tm, ng), lambda i: (i, 0))],
        out_shape=[jax.ShapeDtypeStruct((t, RWKV_COLS), F32),
                   jax.ShapeDtypeStruct((t, S5_WIDTH), F32),
                   jax.ShapeDtypeStruct((t, ng), BF16)],
        compiler_params=pltpu.CompilerParams(dimension_semantics=("arbitrary",),
                                             vmem_limit_bytes=VMEM_LIMIT),
        name="in_proj",
    )(x2, g, w)


def _rwkv_kernel(c_ref, mu_ref, wl_ref, w0_ref, a0_ref, kk_ref, ka_ref, rk_ref, lng_ref, lnb_ref,
                 ones_ref, cum_ref, o_ref, carry_ref, s_ref):
    C = RWKV_CHUNK
    W = RWKV_WIDTH
    npairs = W // LANES

    @pl.when(pl.program_id(1) == 0)
    def _():
        carry_ref[...] = jnp.zeros_like(carry_ref)
        s_ref[...] = jnp.zeros_like(s_ref)

    c = c_ref[...]
    R = c.shape[0]
    nchunks = R // C
    row = lax.broadcasted_iota(jnp.int32, (R, 1), 0)
    prev = jnp.where(row == 0, carry_ref[...], pltpu.roll(c, 1, 0))
    carry_ref[...] = c[R - 1:R, :]
    cs = c + (prev - c) * mu_ref[...]

    r = cs[:, 0:W]
    k = cs[:, W:2 * W]
    v = cs[:, 2 * W:3 * W]
    lin = cs[:, 3 * W:]
    llane = lax.broadcasted_iota(jnp.int32, lin.shape, 1)
    lact = jnp.where(llane < DECAY_LORA, jnp.tanh(lin),
                     jnp.where(llane < DECAY_LORA + AAA_LORA, lin, _sigmoid(lin)))
    lo = _mm(lact, wl_ref[...])
    zw = -(w0_ref[...] + lo[:, 0:W])
    softplus = jnp.maximum(zw, 0.0) + jnp.log(1.0 + jnp.exp(-jnp.abs(zw)))
    ld = -jnp.exp(-softplus - 0.5)
    a = _sigmoid(a0_ref[...] + lo[:, W:2 * W])
    g = lo[:, 2 * W:3 * W]

    ones_bd = ones_ref[...]
    segsum = lambda t: _mm_exact_rhs(t, ones_bd)
    kk = k * kk_ref[...]
    kkn = kk / jnp.maximum(jnp.sqrt(segsum(kk * kk)), 1e-12)
    kmod = k * (1.0 + (a - 1.0) * ka_ref[...])

    cums = _mm_exact_lhs(cum_ref[...], ld)
    cum = cums[:R]
    tot = cums[R:]
    inv = jnp.exp(-cum)
    tail = jnp.exp(tot - cum)
    At = -kkn * jnp.exp(cum - ld)
    Rt = r * jnp.exp(cum)
    kka = kkn * a
    Bt = kka * inv
    Kt = kmod * inv
    Bend = kka * tail
    Kend = kmod * tail
    pc = jnp.exp(tot)

    lane = lax.broadcasted_iota(jnp.int32, (C, LANES), 1)
    h0 = lane < RWKV_HEAD_DIM
    split = lambda t: jnp.concatenate([jnp.where(h0, t, 0.0), jnp.where(h0, 0.0, t)], axis=0)
    grow = lax.broadcasted_iota(jnp.int32, (C, 4 * C), 0)
    gcol = lax.broadcasted_iota(jnp.int32, (C, 4 * C), 1) & (C - 1)
    r2 = lax.broadcasted_iota(jnp.int32, (2 * C, 2 * C), 0)
    c2 = lax.broadcasted_iota(jnp.int32, (2 * C, 2 * C), 1)
    eye = (r2 == c2).astype(F32)
    blk_shift = int(math.log2(RWKV_INV_BLOCK))
    same_blk = (r2 >> blk_shift) == (c2 >> blk_shift)
    same_head = (r2 < C) == (c2 < C)
    same_head2 = jnp.concatenate([same_head, same_head], axis=0)
    zeros_c = jnp.zeros((C, LANES), F32)
    zeros_2c = jnp.zeros((2 * C, LANES), F32)

    units = [(ci, p) for ci in range(nchunks) for p in range(npairs)]
    blk = lambda t, u: t[u[0] * C:(u[0] + 1) * C, u[1] * LANES:(u[1] + 1) * LANES]
    each = lambda f, *ls: [f(*xs) for xs in zip(*ls)]

    lhs = [jnp.concatenate([blk(At, u), blk(Rt, u)], axis=0) for u in units]
    rhs = [jnp.concatenate([split(blk(Bt, u)), split(blk(Kt, u))], axis=0) for u in units]
    G = each(_mm_nt, lhs, rhs)
    a_row = [jnp.where(gcol < grow, t[:C], 0.0) for t in G]
    m_row = [jnp.where(gcol <= grow, t[C:], 0.0) for t in G]
    a_bd = [split(t[:, :2 * C]) for t in a_row]

    a_d = [jnp.where(same_blk, t, 0.0) for t in a_bd]
    a_off = each(lambda x, y: x - y, a_bd, a_d)
    dinv = [eye + t for t in a_d]
    pw = a_d
    for _ in range(blk_shift - 1):
        pw = each(_mm, pw, pw)
        dinv = each(lambda x, y: x + _mm(x, y), dinv, pw)
    n1 = each(_mm, dinv, a_off)
    n2 = each(_mm, n1, n1)
    n3 = each(_mm, n1, n2)
    tinv = each(lambda x1, x2, x3, dv: _mm(eye + x1 + x2 + x3, dv), n1, n2, n3, dinv)

    vp = [blk(v, u) for u in units]
    v_st = [split(t) for t in vp]
    rhs0 = each(lambda ar, vs: _mm(ar[:, 2 * C:], vs), a_row, v_st)
    wu = each(lambda t, l, r0: _mm(t, jnp.concatenate([split(l[:C]), split(r0)], axis=1)),
              tinv, lhs, rhs0)
    wu_lp = [t[:C] + t[C:] for t in wu]
    mn = each(lambda w_, v_, u: _mm_tn(
        jnp.concatenate([w_, jnp.concatenate([zeros_c, v_], axis=1)], axis=0),
        jnp.concatenate([blk(Bend, u), blk(Kend, u)], axis=0)), wu_lp, vp, units)
    mn = [jnp.where(same_head2, t, 0.0) for t in mn]
    qy = each(lambda m_, w_, vs: _mm(m_, jnp.concatenate(
        [w_, jnp.concatenate([zeros_2c, vs], axis=1)], axis=0)), m_row, wu, v_st)

    ys = [[None] * npairs for _ in range(nchunks)]
    states = [s_ref[p] for p in range(npairs)]
    for i, (ci, p) in enumerate(units):
        s_old = states[p]
        q = lhs[i][C:] + qy[i][:, :LANES]
        ys[ci][p] = _mm_nt(q, s_old) + qy[i][:, LANES:]
        states[p] = s_old * blk(pc, (ci, p))[0:1, :] + _mm(s_old, mn[i][:LANES]) + mn[i][LANES:]
    for p in range(npairs):
        s_ref[p] = states[p]

    y = jnp.concatenate([jnp.concatenate(t, axis=1) for t in ys], axis=0)
    inv_n = 1.0 / RWKV_HEAD_DIM
    mean = segsum(y) * inv_n
    d = y - mean
    var = segsum(d * d) * inv_n
    yn = d * lax.rsqrt(var + GN_EPS) * lng_ref[...] + lnb_ref[...]
    bonus = segsum(r * kmod * rk_ref[...]) * v
    o_ref[...] = ((yn + bonus) * g).astype(o_ref.dtype)


def _rwkv_cum_matrix(rows):
    t = jnp.arange(rows)
    same = (t[:, None] // RWKV_CHUNK) == (t[None, :] // RWKV_CHUNK)
    return jnp.concatenate([same & (t[None, :] <= t[:, None]), same], axis=0).astype(BF16)


def _rwkv(crw3, mu, wl, w0, a0, k_k, k_a, r_k, ln_g, ln_b, ones_bd, cum_mat):
    b, s, _ = crw3.shape
    R = cum_mat.shape[1]
    W = RWKV_WIDTH
    vec = lambda n: pl.BlockSpec((1, n), lambda i, j: (0, 0))
    return pl.pallas_call(
        _rwkv_kernel,
        grid=(b, s // R),
        in_specs=[pl.BlockSpec((None, R, RWKV_COLS), lambda i, j: (i, j, 0)),
                  vec(RWKV_COLS),
                  pl.BlockSpec((LORA_COLS, 3 * W), lambda i, j: (0, 0)),
                  vec(W), vec(W), vec(W), vec(W), vec(W), vec(W), vec(W),
                  pl.BlockSpec((W, W), lambda i, j: (0, 0)),
                  pl.BlockSpec((2 * R, R), lambda i, j: (0, 0))],
        out_specs=pl.BlockSpec((None, R, W), lambda i, j: (i, j, 0)),
        out_shape=jax.ShapeDtypeStruct((b, s, W), BF16),
        scratch_shapes=[pltpu.VMEM((1, RWKV_COLS), F32),
                        pltpu.VMEM((W // LANES, LANES, LANES), F32)],
        compiler_params=pltpu.CompilerParams(dimension_semantics=("arbitrary", "arbitrary"),
                                             vmem_limit_bytes=VMEM_LIMIT),
        name="rwkv",
    )(crw3, mu, wl, w0, a0, k_k, k_a, r_k, ln_g, ln_b, ones_bd, cum_mat)


def _s5_mats(lam_re, lam_im, log_dt, b_re, b_im, c_re, c_im, d_skip):
    L = S5_CHUNK
    G, N = lam_re.shape
    ch = b_re.shape[-1]
    dt = jnp.exp(log_dt)[:, None]
    lr, li = lam_re, lam_im
    mag = jnp.exp(lr * dt)
    lb_re, lb_im = mag * jnp.cos(li * dt), mag * jnp.sin(li * dt)
    den = lr * lr + li * li
    nr, ni = lb_re - 1.0, lb_im
    coef_re = (nr * lr + ni * li) / den
    coef_im = (ni * lr - nr * li) / den
    bb_re = coef_re[..., None] * b_re - coef_im[..., None] * b_im
    bb_im = coef_re[..., None] * b_im + coef_im[..., None] * b_re
    prs, pis = [jnp.ones_like(lb_re)], [jnp.zeros_like(lb_im)]
    for _ in range(L):
        pr_, pi_ = prs[-1], pis[-1]
        prs.append(pr_ * lb_re - pi_ * lb_im)
        pis.append(pr_ * lb_im + pi_ * lb_re)
    pr = jnp.stack(prs)
    pi = jnp.stack(pis)
    eye_g = jnp.eye(G, dtype=F32)

    def lam_bb(qr, qi):
        return (qr[..., None] * bb_re[None] - qi[..., None] * bb_im[None],
                qr[..., None] * bb_im[None] + qi[..., None] * bb_re[None])

    lre, lim = lam_bb(pr[:L], pi[:L])
    hp = lax.Precision.HIGHEST
    kern = (jnp.einsum('gon,lgni->lgio', c_re, lre, precision=hp)
            - jnp.einsum('gon,lgni->lgio', c_im, lim, precision=hp))
    bdk = (kern[:, :, :, None, :] * eye_g[None, :, None, :, None]).reshape(L, G * ch, G * ch)
    bdk = bdk.at[0].add(jnp.diag(d_skip.reshape(-1)))
    wre, wim = lam_bb(pr[:L][::-1], pi[:L][::-1])
    expand_in = lambda t: jnp.swapaxes(t, 2, 3)[:, :, :, None, :] * eye_g[None, :, None, :, None]
    w_in = jnp.stack([expand_in(wre), expand_in(wim)], axis=3).reshape(L, G * ch, 2 * G * N)
    qr, qi = pr[1:L + 1][:, :, None, :], pi[1:L + 1][:, :, None, :]
    o_re = c_re[None] * qr - c_im[None] * qi
    o_im = -c_re[None] * qi - c_im[None] * qr
    expand_out = lambda t: jnp.swapaxes(t, 2, 3)[:, :, :, None, :] * eye_g[None, :, None, :, None]
    w_out = jnp.stack([expand_out(o_re), expand_out(o_im)], axis=1).reshape(L, 2 * G * N, G * ch)
    plr = pr[L].reshape(1, G * N)
    pli = pi[L].reshape(1, G * N)
    lag = jnp.arange(L)[None, :] - jnp.arange(L)[:, None]
    toep = jnp.where((lag >= 0)[:, :, None, None], bdk[jnp.maximum(lag, 0)], 0.0)
    toep = jnp.swapaxes(toep, 1, 2).reshape(L * G * ch, L * G * ch)
    w_in_flat = w_in.reshape(L * G * ch, 2 * G * N)
    w_out_flat = jnp.swapaxes(w_out, 0, 1).reshape(2 * G * N, L * G * ch)
    w_a = jnp.concatenate([w_in_flat, toep], axis=1)
    return w_a.astype(BF16), w_out_flat.astype(BF16), plr, pli


def _s5_kernel(u_ref, wa_ref, wo_ref, plr_ref, pli_ref, o_ref, wloc_ref, zprev_ref):
    nch = u_ref.shape[0]
    half = S5_ZW // 2

    r = _mm(u_ref[...], wa_ref[...])
    wloc_ref[...] = r[:, :S5_ZW]
    y_lag = r[:, S5_ZW:]

    plr = plr_ref[...]
    pli = pli_ref[...]

    def step(ci, z):
        zprev_ref[pl.ds(ci, 1), :] = z
        zr, zi = z[:, :half], z[:, half:]
        nz = jnp.concatenate([plr * zr - pli * zi, plr * zi + pli * zr], axis=1)
        return nz + wloc_ref[pl.ds(ci, 1), :]

    lax.fori_loop(0, nch, step, jnp.zeros((1, S5_ZW), F32))
    o_ref[...] = y_lag + _mm(zprev_ref[...], wo_ref[...])


def _s5(u_flat, w_a, w_o, plr, pli):
    b, nch, fw = u_flat.shape
    const = lambda a: pl.BlockSpec(a.shape, lambda i: (0,) * a.ndim, pipeline_mode=pl.Buffered(1))
    return pl.pallas_call(
        _s5_kernel,
        grid=(b,),
        in_specs=[pl.BlockSpec((None, nch, fw), lambda i: (i, 0, 0)),
                  const(w_a), const(w_o), const(plr), const(pli)],
        out_specs=pl.BlockSpec((None, nch, fw), lambda i: (i, 0, 0)),
        out_shape=jax.ShapeDtypeStruct((b, nch, fw), F32),
        scratch_shapes=[pltpu.VMEM((nch, S5_ZW), F32),
                        pltpu.VMEM((nch, S5_ZW), F32)],
        compiler_params=pltpu.CompilerParams(dimension_semantics=("arbitrary",),
                                             vmem_limit_bytes=VMEM_LIMIT),
        name="s5",
    )(u_flat, w_a, w_o, plr, pli)


def _merge_kernel(x_ref, ya_ref, yb_ref, gates_ref, gluw_ref, glub_ref, wba_ref, wbb_ref, wout_ref,
                  fng_ref, rw_ref, rb_ref, x1_ref, t_ref, comb_ref):
    d = x_ref.shape[1]
    y_a = jnp.dot(ya_ref[...], wba_ref[...], preferred_element_type=F32)
    ys = yb_ref[...]
    z = 0.5 * ys * (1.0 + jnp.tanh(math.sqrt(2.0 / math.pi) * (ys + 0.044715 * (ys * ys * ys))))
    z = z * _sigmoid(_mm(z, gluw_ref[...]) + glub_ref[...])
    y_b = _mm(z, wbb_ref[...])
    gates = gates_ref[...].astype(F32)
    merged = _sigmoid(gates[:, :d]) * y_a + _sigmoid(gates[:, d:]) * y_b
    x1 = x_ref[...] + _mm(merged, wout_ref[...])
    x1_ref[...] = x1
    t = _rms_norm(x1, fng_ref[...])
    t_hi = t.astype(BF16)
    t_ref[...] = t_hi

    t_lo = (t - t_hi.astype(F32)).astype(BF16)
    hh_hl = jnp.dot(t_hi, rw_ref[...], preferred_element_type=F32)
    lh = jnp.dot(t_lo, rw_ref[:, :ROUTER_LANES], preferred_element_type=F32)
    logits = hh_hl[:, :ROUTER_LANES] + hh_hl[:, ROUTER_LANES:] + lh + rb_ref[...]
    lane = lax.broadcasted_iota(jnp.int32, logits.shape, 1)
    neg = -jnp.inf
    big = jnp.int32(1 << 20)
    is_g = (lane >= N_EXPERTS) & (lane < N_EXPERTS + N_GROUPS)
    gl = jnp.where(is_g, logits, neg)
    gmax = jnp.max(gl, axis=-1, keepdims=True)
    g_p = 1.0 / jnp.sum(jnp.exp(gl - gmax), axis=-1, keepdims=True)
    g_idx = jnp.min(jnp.where(gl == gmax, lane - N_EXPERTS, big), axis=-1, keepdims=True)
    el = jnp.where((lane < N_EXPERTS) & ((lane >> int(math.log2(EXPERTS_PER_GROUP))) == g_idx), logits, neg)
    t1 = jnp.max(el, axis=-1, keepdims=True)
    i1 = jnp.min(jnp.where(el == t1, lane, big), axis=-1, keepdims=True)
    el2 = jnp.where(lane == i1, neg, el)
    t2 = jnp.max(el2, axis=-1, keepdims=True)
    i2 = jnp.min(jnp.where(el2 == t2, lane, big), axis=-1, keepdims=True)
    e21 = jnp.exp(t2 - t1)
    w1 = g_p / (1.0 + e21)
    w2 = g_p * e21 / (1.0 + e21)
    comb_ref[...] = jnp.where(lane == i1, w1, 0.0) + jnp.where(lane == i2, w2, 0.0)


def _merge(x2, ya, yb, gates, gluw, glub, wba, wbb, wout, fng, rw, rb, tm):
    t, d = x2.shape
    full = lambda a: pl.BlockSpec(a.shape, lambda i: (0,) * a.ndim)
    rowblk = lambda n: pl.BlockSpec((tm, n), lambda i: (i, 0))
    return pl.pallas_call(
        _merge_kernel,
        grid=(t // tm,),
        in_specs=[rowblk(d), rowblk(ya.shape[1]), rowblk(yb.shape[1]), rowblk(gates.shape[1]),
                  full(gluw), full(glub), full(wba), full(wbb), full(wout), full(fng), full(rw),
                  full(rb)],
        out_specs=[rowblk(d), rowblk(d), rowblk(ROUTER_LANES)],
        out_shape=[jax.ShapeDtypeStruct((t, d), F32),
                   jax.ShapeDtypeStruct((t, d), BF16),
                   jax.ShapeDtypeStruct((t, ROUTER_LANES), F32)],
        compiler_params=pltpu.CompilerParams(dimension_semantics=("arbitrary",),
                                             vmem_limit_bytes=VMEM_LIMIT),
        name="merge",
    )(x2, ya, yb, gates, gluw, glub, wba, wbb, wout, fng, rw, rb)


def _moe_kernel(t_ref, comb_ref, x1_ref, wg_ref, wu_ref, wd_ref, o_ref, acc_ref):
    e = pl.program_id(1)

    @pl.when(e == 0)
    def _():
        acc_ref[...] = jnp.zeros_like(acc_ref)

    t = t_ref[...]
    hg = jnp.dot(t, wg_ref[...], preferred_element_type=F32)
    hu = jnp.dot(t, wu_ref[...], preferred_element_type=F32)
    comb = comb_ref[...]
    lane = lax.broadcasted_iota(jnp.int32, comb.shape, 1)
    ce = jnp.sum(jnp.where(lane == e, comb, 0.0), axis=-1, keepdims=True)
    hid = hg * _sigmoid(hg) * hu * ce
    acc_ref[...] += _mm(hid, wd_ref[...])

    @pl.when(e == pl.num_programs(1) - 1)
    def _():
        o_ref[...] = x1_ref[...] + acc_ref[...]


def _moe(t_bf, comb, x1, wg, wu, wd, tm):
    t, d = x1.shape
    ne, _, de = wg.shape
    return pl.pallas_call(
        _moe_kernel,
        grid=(t // tm, ne),
        in_specs=[pl.BlockSpec((tm, d), lambda i, e: (i, 0)),
                  pl.BlockSpec((tm, ROUTER_LANES), lambda i, e: (i, 0)),
                  pl.BlockSpec((tm, d), lambda i, e: (i, 0)),
                  pl.BlockSpec((None, d, de), lambda i, e: (e, 0, 0)),
                  pl.BlockSpec((None, d, de), lambda i, e: (e, 0, 0)),
                  pl.BlockSpec((None, de, d), lambda i, e: (e, 0, 0))],
        out_specs=pl.BlockSpec((tm, d), lambda i, e: (i, 0)),
        out_shape=jax.ShapeDtypeStruct((t, d), F32),
        scratch_shapes=[pltpu.VMEM((tm, d), F32)],
        compiler_params=pltpu.CompilerParams(dimension_semantics=("arbitrary", "arbitrary"),
                                             vmem_limit_bytes=VMEM_LIMIT),
        name="moe",
    )(t_bf, comb, x1, wg, wu, wd)


def _ple_kernel(x_ref, p_ref, png_ref, wg_ref, wp_ref, fng_ref, o_ref):
    x2 = x_ref[...]
    hp = _rms_norm(x2, png_ref[...])
    gate = _sigmoid(_mm(hp, wg_ref[...]))
    x3 = x2 + gate * _mm(p_ref[...], wp_ref[...])
    o_ref[...] = _rms_norm(x3, fng_ref[...])


def _ple(x2, p2, png, wg, wp, fng, tm):
    t, d = x2.shape
    full = lambda a: pl.BlockSpec(a.shape, lambda i: (0,) * a.ndim)
    return pl.pallas_call(
        _ple_kernel,
        grid=(t // tm,),
        in_specs=[pl.BlockSpec((tm, d), lambda i: (i, 0)),
                  pl.BlockSpec((tm, p2.shape[1]), lambda i: (i, 0)),
                  full(png), full(wg), full(wp), full(fng)],
        out_specs=pl.BlockSpec((tm, d), lambda i: (i, 0)),
        out_shape=jax.ShapeDtypeStruct((t, d), F32),
        compiler_params=pltpu.CompilerParams(dimension_semantics=("arbitrary",),
                                             vmem_limit_bytes=VMEM_LIMIT),
        name="ple",
    )(x2, p2, png, wg, wp, fng)


def _row_tile(t, want):
    tm = min(want, t)
    while t % tm:
        tm //= 2
    return tm


def _layer(x, p, mix_norm, w_in, mu_shift, rk_w0, rk_w_up, rk_a0, rk_a_up, rk_g_up,
           rk_k_k, rk_k_a, rk_r_k, rk_ln_g, rk_ln_b, s5_lam_re, s5_lam_im, s5_log_dt,
           s5_b_re, s5_b_im, s5_c_re, s5_c_im, s5_d, s5_glu_w, s5_glu_b,
           w_branch_a, w_branch_b, w_out, ffn_norm, router_group_w, router_group_b,
           router_expert_w, router_expert_b, exp_w_gate, exp_w_up, exp_w_down,
           ple_norm, ple_gate_w, ple_proj):
    b, s, d = x.shape
    t = b * s
    W = RWKV_WIDTH
    row = lambda a: a.reshape(1, -1).astype(F32)
    x2 = x.reshape(t, d)

    crw, us5, gates = _in_proj(x2, row(mix_norm), w_in.astype(BF16), _row_tile(t, 256))

    wl = jnp.zeros((LORA_COLS, 3 * W), F32)
    wl = wl.at[:DECAY_LORA, :W].set(rk_w_up)
    wl = wl.at[DECAY_LORA:DECAY_LORA + AAA_LORA, W:2 * W].set(rk_a_up)
    wl = wl.at[DECAY_LORA + AAA_LORA:, 2 * W:].set(rk_g_up)
    hid = jnp.arange(W) // RWKV_HEAD_DIM
    ones_bd = (hid[:, None] == hid[None, :]).astype(BF16)
    ya = _rwkv(crw.reshape(b, s, RWKV_COLS), row(mu_shift), wl.astype(BF16), row(rk_w0), row(rk_a0),
               row(rk_k_k), row(rk_k_a), row(rk_r_k), row(rk_ln_g), row(rk_ln_b), ones_bd,
               _rwkv_cum_matrix(_row_tile(s, RWKV_BLOCK)))

    s5_wa, s5_wo, plr, pli = _s5_mats(s5_lam_re, s5_lam_im, s5_log_dt, s5_b_re, s5_b_im,
                                      s5_c_re, s5_c_im, s5_d)
    yb = _s5(us5.reshape(b, s // S5_CHUNK, S5_CHUNK * S5_WIDTH), s5_wa, s5_wo, plr, pli)

    rw = jnp.zeros((d, ROUTER_LANES), F32)
    rw = rw.at[:, :N_EXPERTS].set(router_expert_w).at[:, N_EXPERTS:N_EXPERTS + N_GROUPS].set(router_group_w)
    rb = jnp.zeros((1, ROUTER_LANES), F32)
    rb = rb.at[0, :N_EXPERTS].set(router_expert_b).at[0, N_EXPERTS:N_EXPERTS + N_GROUPS].set(router_group_b)
    rw_hi = rw.astype(BF16)
    rw = jnp.concatenate([rw_hi, (rw - rw_hi.astype(F32)).astype(BF16)], axis=1)
    x1, t_bf, comb = _merge(x2, ya.reshape(t, W), yb.reshape(t, S5_WIDTH), gates,
                            s5_glu_w.astype(BF16), row(s5_glu_b), w_branch_a.astype(BF16), w_branch_b.astype(BF16), w_out.astype(BF16),
                            row(ffn_norm), rw, rb, _row_tile(t, 256))

    xm = _moe(t_bf, comb, x1, exp_w_gate.astype(BF16), exp_w_up.astype(BF16),
              exp_w_down.astype(BF16), _row_tile(t, 1024))
    return xm, p.reshape(t, -1), row(ple_norm), ple_gate_w.astype(BF16), ple_proj.astype(BF16)


def kernel(x, p, mix_norm, w_in, mu_shift, rk_w0, rk_w_up, rk_a0, rk_a_up, rk_g_up, rk_k_k, rk_k_a,
           rk_r_k, rk_ln_g, rk_ln_b, s5_lam_re, s5_lam_im, s5_log_dt, s5_b_re, s5_b_im, s5_c_re,
           s5_c_im, s5_d, s5_glu_w, s5_glu_b, w_branch_a, w_branch_b, w_out, ffn_norm,
           router_group_w, router_group_b, router_expert_w, router_expert_b, exp_w_gate, exp_w_up,
           exp_w_down, ple_norm, ple_gate_w, ple_proj, final_norm):
    b, s, d = x.shape
    depth = w_in.shape[0]
    assert depth == 1, "the final norm is fused into the last layer's PLE kernel"
    i = 0
    xm, p2, png, wpg, wpp = _layer(
        x, p[i], mix_norm[i], w_in[i], mu_shift[i], rk_w0[i], rk_w_up[i], rk_a0[i], rk_a_up[i],
        rk_g_up[i], rk_k_k[i], rk_k_a[i], rk_r_k[i], rk_ln_g[i], rk_ln_b[i], s5_lam_re[i],
        s5_lam_im[i], s5_log_dt[i], s5_b_re[i], s5_b_im[i], s5_c_re[i], s5_c_im[i], s5_d[i],
        s5_glu_w[i], s5_glu_b[i], w_branch_a[i], w_branch_b[i], w_out[i], ffn_norm[i],
        router_group_w[i], router_group_b[i], router_expert_w[i], router_expert_b[i],
        exp_w_gate[i], exp_w_up[i], exp_w_down[i], ple_norm[i], ple_gate_w[i], ple_proj[i])
    out = _ple(xm, p2, png, wpg, wpp, final_norm.reshape(1, -1).astype(F32), _row_tile(b * s, 512))
    return out.reshape(b, s, d)
```

```python
import functools
import math

import jax
import jax.numpy as jnp
from jax import lax
from jax.experimental import pallas as pl
from jax.experimental.pallas import tpu as pltpu

F32 = jnp.float32
BF16 = jnp.bfloat16

NORM_EPS = 1e-6
GN_EPS = 64e-5

RWKV_HEADS = 8
RWKV_HEAD_DIM = 64
RWKV_WIDTH = RWKV_HEADS * RWKV_HEAD_DIM
DECAY_LORA = 64
AAA_LORA = 64
GATE_LORA = 128
LORA_COLS = DECAY_LORA + AAA_LORA + GATE_LORA
RWKV_COLS = 3 * RWKV_WIDTH + LORA_COLS
S5_GROUPS = 16
S5_GROUP_CH = 16
S5_WIDTH = S5_GROUPS * S5_GROUP_CH
S5_STATE = 64
S5_ZW = 2 * S5_GROUPS * S5_STATE
N_GROUPS = 4
EXPERTS_PER_GROUP = 8
N_EXPERTS = N_GROUPS * EXPERTS_PER_GROUP

LANES = 128
RWKV_CHUNK = 64
RWKV_INV_BLOCK = 16
RWKV_BLOCK = 256
S5_CHUNK = 8
ROUTER_LANES = 128
MOE_TILE = 256
DMA_UNROLL = 8
VMEM_LIMIT = 56 * 1024 * 1024


def _mm(a, b):
    return jnp.dot(a.astype(BF16), b.astype(BF16), preferred_element_type=F32)


def _mm_nt(a, b):
    return lax.dot_general(a.astype(BF16), b.astype(BF16), (((1,), (1,)), ((), ())),
                           preferred_element_type=F32)


def _mm_tn(a, b):
    return lax.dot_general(a.astype(BF16), b.astype(BF16), (((0,), (0,)), ((), ())),
                           preferred_element_type=F32)


def _split3(x):
    hi = x.astype(BF16)
    r1 = x - hi.astype(F32)
    mid = r1.astype(BF16)
    lo = (r1 - mid.astype(F32)).astype(BF16)
    return hi, mid, lo


def _mm_exact_lhs(a_bf16, x):
    hi, mid, lo = _split3(x)
    d = lambda t: jnp.dot(a_bf16, t, preferred_element_type=F32)
    return d(hi) + d(mid) + d(lo)


def _mm_exact_rhs(x, b_bf16):
    hi, mid, lo = _split3(x)
    d = lambda t: jnp.dot(t, b_bf16, preferred_element_type=F32)
    return d(hi) + d(mid) + d(lo)


def _sigmoid(x):
    return 1.0 / (1.0 + jnp.exp(-x))


def _rms_norm(x, g):
    ms = jnp.mean(x * x, axis=-1, keepdims=True)
    return x * lax.rsqrt(ms + NORM_EPS) * g


def _in_proj_kernel(x_ref, g_ref, w_ref, crw_ref, us5_ref, gates_ref):
    h = _rms_norm(x_ref[...], g_ref[...]).astype(BF16)
    c0, c1 = RWKV_COLS, RWKV_COLS + S5_WIDTH
    crw_ref[...] = jnp.dot(h, w_ref[:, :c0], preferred_element_type=F32)
    us5_ref[...] = jnp.dot(h, w_ref[:, c0:c1], preferred_element_type=F32)
    gates_ref[...] = jnp.dot(h, w_ref[:, c1:], preferred_element_type=F32).astype(BF16)


def _in_proj(x2, g, w, tm):
    t, d = x2.shape
    n = w.shape[1]
    ng = n - RWKV_COLS - S5_WIDTH
    return pl.pallas_call(
        _in_proj_kernel,
        grid=(t // tm,),
        in_specs=[pl.BlockSpec((tm, d), lambda i: (i, 0)),
                  pl.BlockSpec((1, d), lambda i: (0, 0)),
                  pl.BlockSpec((d, n), lambda i: (0, 0))],
        out_specs=[pl.BlockSpec((tm, RWKV_COLS), lambda i: (i, 0)),
                   pl.BlockSpec((tm, S5_WIDTH), lambda i: (i, 0)),
                   pl.BlockSpec((tm, ng), lambda i: (i, 0))],
        out_shape=[jax.ShapeDtypeStruct((t, RWKV_COLS), F32),
                   jax.ShapeDtypeStruct((t, S5_WIDTH), F32),
                   jax.ShapeDtypeStruct((t, ng), BF16)],
        compiler_params=pltpu.CompilerParams(dimension_semantics=("arbitrary",),
                                             vmem_limit_bytes=VMEM_LIMIT),
        name="in_proj",
    )(x2, g, w)


def _rwkv_kernel(c_ref, mu_ref, wl_ref, w0_ref, a0_ref, kk_ref, ka_ref, rk_ref, lng_ref, lnb_ref,
                 ones_ref, cum_ref, o_ref, carry_ref, s_ref):
    C = RWKV_CHUNK
    W = RWKV_WIDTH
    npairs = W // LANES

    @pl.when(pl.program_id(1) == 0)
    def _():
        carry_ref[...] = jnp.zeros_like(carry_ref)
        s_ref[...] = jnp.zeros_like(s_ref)

    c = c_ref[...]
    R = c.shape[0]
    nchunks = R // C
    row = lax.broadcasted_iota(jnp.int32, (R, 1), 0)
    prev = jnp.where(row == 0, carry_ref[...], pltpu.roll(c, 1, 0))
    carry_ref[...] = c[R - 1:R, :]
    cs = c + (prev - c) * mu_ref[...]

    r = cs[:, 0:W]
    k = cs[:, W:2 * W]
    v = cs[:, 2 * W:3 * W]
    lin = cs[:, 3 * W:]
    llane = lax.broadcasted_iota(jnp.int32, lin.shape, 1)
    lact = jnp.where(llane < DECAY_LORA, jnp.tanh(lin),
                     jnp.where(llane < DECAY_LORA + AAA_LORA, lin, _sigmoid(lin)))
    lo = _mm(lact, wl_ref[...])
    zw = -(w0_ref[...] + lo[:, 0:W])
    softplus = jnp.maximum(zw, 0.0) + jnp.log(1.0 + jnp.exp(-jnp.abs(zw)))
    ld = -jnp.exp(-softplus - 0.5)
    a = _sigmoid(a0_ref[...] + lo[:, W:2 * W])
    g = lo[:, 2 * W:3 * W]

    ones_bd = ones_ref[...]
    segsum = lambda t: _mm_exact_rhs(t, ones_bd)
    kk = k * kk_ref[...]
    kkn = kk / jnp.maximum(jnp.sqrt(segsum(kk * kk)), 1e-12)
    kmod = k * (1.0 + (a - 1.0) * ka_ref[...])

    cums = _mm_exact_lhs(cum_ref[...], ld)
    cum = cums[:R]
    tot = cums[R:]
    inv = jnp.exp(-cum)
    tail = jnp.exp(tot - cum)
    At = -kkn * jnp.exp(cum - ld)
    Rt = r * jnp.exp(cum)
    kka = kkn * a
    Bt = kka * inv
    Kt = kmod * inv
    Bend = kka * tail
    Kend = kmod * tail
    pc = jnp.exp(tot)

    lane = lax.broadcasted_iota(jnp.int32, (C, LANES), 1)
    h0 = lane < RWKV_HEAD_DIM
    split = lambda t: jnp.concatenate([jnp.where(h0, t, 0.0), jnp.where(h0, 0.0, t)], axis=0)
    grow = lax.broadcasted_iota(jnp.int32, (C, 4 * C), 0)
    gcol = lax.broadcasted_iota(jnp.int32, (C, 4 * C), 1) & (C - 1)
    r2 = lax.broadcasted_iota(jnp.int32, (2 * C, 2 * C), 0)
    c2 = lax.broadcasted_iota(jnp.int32, (2 * C, 2 * C), 1)
    eye = (r2 == c2).astype(F32)
    blk_shift = int(math.log2(RWKV_INV_BLOCK))
    same_blk = (r2 >> blk_shift) == (c2 >> blk_shift)
    same_head = (r2 < C) == (c2 < C)
    same_head2 = jnp.concatenate([same_head, same_head], axis=0)
    zeros_c = jnp.zeros((C, LANES), F32)
    zeros_2c = jnp.zeros((2 * C, LANES), F32)

    units = [(ci, p) for ci in range(nchunks) for p in range(npairs)]
    blk = lambda t, u: t[u[0] * C:(u[0] + 1) * C, u[1] * LANES:(u[1] + 1) * LANES]
    each = lambda f, *ls: [f(*xs) for xs in zip(*ls)]

    lhs = [jnp.concatenate([blk(At, u), blk(Rt, u)], axis=0) for u in units]
    rhs = [jnp.concatenate([split(blk(Bt, u)), split(blk(Kt, u))], axis=0) for u in units]
    G = each(_mm_nt, lhs, rhs)
    a_row = [jnp.where(gcol < grow, t[:C], 0.0) for t in G]
    m_row = [jnp.where(gcol <= grow, t[C:], 0.0) for t in G]
    a_bd = [split(t[:, :2 * C]) for t in a_row]

    a_d = [jnp.where(same_blk, t, 0.0) for t in a_bd]
    a_off = each(lambda x, y: x - y, a_bd, a_d)
    dinv = [eye + t for t in a_d]
    pw = a_d
    for _ in range(blk_shift - 1):
        pw = each(_mm, pw, pw)
        dinv = each(lambda x, y: x + _mm(x, y), dinv, pw)
    n1 = each(_mm, dinv, a_off)
    n2 = each(_mm, n1, n1)
    n3 = each(_mm, n1, n2)
    tinv = each(lambda x1, x2, x3, dv: _mm(eye + x1 + x2 + x3, dv), n1, n2, n3, dinv)

    vp = [blk(v, u) for u in units]
    v_st = [split(t) for t in vp]
    rhs0 = each(lambda ar, vs: _mm(ar[:, 2 * C:], vs), a_row, v_st)
    wu = each(lambda t, l, r0: _mm(t, jnp.concatenate([split(l[:C]), split(r0)], axis=1)),
              tinv, lhs, rhs0)
    wu_lp = [t[:C] + t[C:] for t in wu]
    mn = each(lambda w_, v_, u: _mm_tn(
        jnp.concatenate([w_, jnp.concatenate([zeros_c, v_], axis=1)], axis=0),
        jnp.concatenate([blk(Bend, u), blk(Kend, u)], axis=0)), wu_lp, vp, units)
    mn = [jnp.where(same_head2, t, 0.0) for t in mn]
    qy = each(lambda m_, w_, vs: _mm(m_, jnp.concatenate(
        [w_, jnp.concatenate([zeros_2c, vs], axis=1)], axis=0)), m_row, wu, v_st)

    ys = [[None] * npairs for _ in range(nchunks)]
    states = [s_ref[p] for p in range(npairs)]
    for i, (ci, p) in enumerate(units):
        s_old = states[p]
        q = lhs[i][C:] + qy[i][:, :LANES]
        ys[ci][p] = _mm_nt(q, s_old) + qy[i][:, LANES:]
        states[p] = s_old * blk(pc, (ci, p))[0:1, :] + _mm(s_old, mn[i][:LANES]) + mn[i][LANES:]
    for p in range(npairs):
        s_ref[p] = states[p]

    y = jnp.concatenate([jnp.concatenate(t, axis=1) for t in ys], axis=0)
    inv_n = 1.0 / RWKV_HEAD_DIM
    mean = segsum(y) * inv_n
    d = y - mean
    var = segsum(d * d) * inv_n
    yn = d * lax.rsqrt(var + GN_EPS) * lng_ref[...] + lnb_ref[...]
    bonus = segsum(r * kmod * rk_ref[...]) * v
    o_ref[...] = ((yn + bonus) * g).astype(o_ref.dtype)


def _rwkv_cum_matrix(rows):
    t = jnp.arange(rows)
    same = (t[:, None] // RWKV_CHUNK) == (t[None, :] // RWKV_CHUNK)
    return jnp.concatenate([same & (t[None, :] <= t[:, None]), same], axis=0).astype(BF16)


def _rwkv(crw3, mu, wl, w0, a0, k_k, k_a, r_k, ln_g, ln_b, ones_bd, cum_mat):
    b, s, _ = crw3.shape
    R = cum_mat.shape[1]
    W = RWKV_WIDTH
    vec = lambda n: pl.BlockSpec((1, n), lambda i, j: (0, 0))
    return pl.pallas_call(
        _rwkv_kernel,
        grid=(b, s // R),
        in_specs=[pl.BlockSpec((None, R, RWKV_COLS), lambda i, j: (i, j, 0)),
                  vec(RWKV_COLS),
                  pl.BlockSpec((LORA_COLS, 3 * W), lambda i, j: (0, 0)),
                  vec(W), vec(W), vec(W), vec(W), vec(W), vec(W), vec(W),
                  pl.BlockSpec((W, W), lambda i, j: (0, 0)),
                  pl.BlockSpec((2 * R, R), lambda i, j: (0, 0))],
        out_specs=pl.BlockSpec((None, R, W), lambda i, j: (i, j, 0)),
        out_shape=jax.ShapeDtypeStruct((b, s, W), BF16),
        scratch_shapes=[pltpu.VMEM((1, RWKV_COLS), F32),
                        pltpu.VMEM((W // LANES, LANES, LANES), F32)],
        compiler_params=pltpu.CompilerParams(dimension_semantics=("arbitrary", "arbitrary"),
                                             vmem_limit_bytes=VMEM_LIMIT),
        name="rwkv",
    )(crw3, mu, wl, w0, a0, k_k, k_a, r_k, ln_g, ln_b, ones_bd, cum_mat)


def _s5_mats(lam_re, lam_im, log_dt, b_re, b_im, c_re, c_im, d_skip):
    L = S5_CHUNK
    G, N = lam_re.shape
    ch = b_re.shape[-1]
    dt = jnp.exp(log_dt)[:, None]
    lr, li = lam_re, lam_im
    mag = jnp.exp(lr * dt)
    lb_re, lb_im = mag * jnp.cos(li * dt), mag * jnp.sin(li * dt)
    den = lr * lr + li * li
    nr, ni = lb_re - 1.0, lb_im
    coef_re = (nr * lr + ni * li) / den
    coef_im = (ni * lr - nr * li) / den
    bb_re = coef_re[..., None] * b_re - coef_im[..., None] * b_im
    bb_im = coef_re[..., None] * b_im + coef_im[..., None] * b_re
    prs, pis = [jnp.ones_like(lb_re)], [jnp.zeros_like(lb_im)]
    for _ in range(L):
        pr_, pi_ = prs[-1], pis[-1]
        prs.append(pr_ * lb_re - pi_ * lb_im)
        pis.append(pr_ * lb_im + pi_ * lb_re)
    pr = jnp.stack(prs)
    pi = jnp.stack(pis)
    eye_g = jnp.eye(G, dtype=F32)

    def lam_bb(qr, qi):
        return (qr[..., None] * bb_re[None] - qi[..., None] * bb_im[None],
                qr[..., None] * bb_im[None] + qi[..., None] * bb_re[None])

    lre, lim = lam_bb(pr[:L], pi[:L])
    hp = lax.Precision.HIGHEST
    kern = (jnp.einsum('gon,lgni->lgio', c_re, lre, precision=hp)
            - jnp.einsum('gon,lgni->lgio', c_im, lim, precision=hp))
    bdk = (kern[:, :, :, None, :] * eye_g[None, :, None, :, None]).reshape(L, G * ch, G * ch)
    bdk = bdk.at[0].add(jnp.diag(d_skip.reshape(-1)))
    wre, wim = lam_bb(pr[:L][::-1], pi[:L][::-1])
    expand_in = lambda t: jnp.swapaxes(t, 2, 3)[:, :, :, None, :] * eye_g[None, :, None, :, None]
    w_in = jnp.stack([expand_in(wre), expand_in(wim)], axis=3).reshape(L, G * ch, 2 * G * N)
    qr, qi = pr[1:L + 1][:, :, None, :], pi[1:L + 1][:, :, None, :]
    o_re = c_re[None] * qr - c_im[None] * qi
    o_im = -c_re[None] * qi - c_im[None] * qr
    expand_out = lambda t: jnp.swapaxes(t, 2, 3)[:, :, :, None, :] * eye_g[None, :, None, :, None]
    w_out = jnp.stack([expand_out(o_re), expand_out(o_im)], axis=1).reshape(L, 2 * G * N, G * ch)
    plr = pr[L].reshape(1, G * N)
    pli = pi[L].reshape(1, G * N)
    lag = jnp.arange(L)[None, :] - jnp.arange(L)[:, None]
    toep = jnp.where((lag >= 0)[:, :, None, None], bdk[jnp.maximum(lag, 0)], 0.0)
    toep = jnp.swapaxes(toep, 1, 2).reshape(L * G * ch, L * G * ch)
    w_in_flat = w_in.reshape(L * G * ch, 2 * G * N)
    w_out_flat = jnp.swapaxes(w_out, 0, 1).reshape(2 * G * N, L * G * ch)
    w_a = jnp.concatenate([w_in_flat, toep], axis=1)
    return w_a.astype(BF16), w_out_flat.astype(BF16), plr, pli


def _s5_kernel(u_ref, wa_ref, wo_ref, plr_ref, pli_ref, o_ref, wloc_ref, zprev_ref):
    nch = u_ref.shape[0]
    half = S5_ZW // 2

    r = _mm(u_ref[...], wa_ref[...])
    wloc_ref[...] = r[:, :S5_ZW]
    y_lag = r[:, S5_ZW:]

    plr = plr_ref[...]
    pli = pli_ref[...]

    def step(ci, z):
        zprev_ref[pl.ds(ci, 1), :] = z
        zr, zi = z[:, :half], z[:, half:]
        nz = jnp.concatenate([plr * zr - pli * zi, plr * zi + pli * zr], axis=1)
        return nz + wloc_ref[pl.ds(ci, 1), :]

    lax.fori_loop(0, nch, step, jnp.zeros((1, S5_ZW), F32))
    o_ref[...] = y_lag + _mm(zprev_ref[...], wo_ref[...])


def _s5(u_flat, w_a, w_o, plr, pli):
    b, nch, fw = u_flat.shape
    const = lambda a: pl.BlockSpec(a.shape, lambda i: (0,) * a.ndim, pipeline_mode=pl.Buffered(1))
    return pl.pallas_call(
        _s5_kernel,
        grid=(b,),
        in_specs=[pl.BlockSpec((None, nch, fw), lambda i: (i, 0, 0)),
                  const(w_a), const(w_o), const(plr), const(pli)],
        out_specs=pl.BlockSpec((None, nch, fw), lambda i: (i, 0, 0)),
        out_shape=jax.ShapeDtypeStruct((b, nch, fw), F32),
        scratch_shapes=[pltpu.VMEM((nch, S5_ZW), F32),
                        pltpu.VMEM((nch, S5_ZW), F32)],
        compiler_params=pltpu.CompilerParams(dimension_semantics=("arbitrary",),
                                             vmem_limit_bytes=VMEM_LIMIT),
        name="s5",
    )(u_flat, w_a, w_o, plr, pli)


def _merge_kernel(x_ref, ya_ref, yb_ref, gates_ref, gluw_ref, glub_ref, wba_ref, wbb_ref, wout_ref,
                  fng_ref, rw_ref, rb_ref, tri_ref, upper_ref, x1_ref, t_ref, rt_ref, pos_ref,
                  te_ref, nused_ref, cnt_ref, cur_ref):
    d = x_ref.shape[1]
    tm = x_ref.shape[0]

    @pl.when(pl.program_id(0) == 0)
    def _():
        cnt_ref[...] = jnp.zeros_like(cnt_ref)
        cur_ref[...] = jnp.zeros_like(cur_ref)
        te_ref[...] = jnp.zeros_like(te_ref)
        nused_ref[...] = jnp.zeros_like(nused_ref)

    y_a = jnp.dot(ya_ref[...], wba_ref[...], preferred_element_type=F32)
    ys = yb_ref[...]
    z = 0.5 * ys * (1.0 + jnp.tanh(math.sqrt(2.0 / math.pi) * (ys + 0.044715 * (ys * ys * ys))))
    z = z * _sigmoid(_mm(z, gluw_ref[...]) + glub_ref[...])
    y_b = _mm(z, wbb_ref[...])
    gates = gates_ref[...].astype(F32)
    merged = _sigmoid(gates[:, :d]) * y_a + _sigmoid(gates[:, d:]) * y_b
    x1 = x_ref[...] + _mm(merged, wout_ref[...])
    x1_ref[...] = x1
    t = _rms_norm(x1, fng_ref[...])
    t_hi = t.astype(BF16)
    t_ref[...] = t

    t_lo = (t - t_hi.astype(F32)).astype(BF16)
    hh_hl = jnp.dot(t_hi, rw_ref[...], preferred_element_type=F32)
    lh = jnp.dot(t_lo, rw_ref[:, :ROUTER_LANES], preferred_element_type=F32)
    logits = hh_hl[:, :ROUTER_LANES] + hh_hl[:, ROUTER_LANES:] + lh + rb_ref[...]
    lane = lax.broadcasted_iota(jnp.int32, logits.shape, 1)
    neg = -jnp.inf
    big = jnp.int32(1 << 20)
    is_g = (lane >= N_EXPERTS) & (lane < N_EXPERTS + N_GROUPS)
    gl = jnp.where(is_g, logits, neg)
    gmax = jnp.max(gl, axis=-1, keepdims=True)
    g_p = 1.0 / jnp.sum(jnp.exp(gl - gmax), axis=-1, keepdims=True)
    g_idx = jnp.min(jnp.where(gl == gmax, lane - N_EXPERTS, big), axis=-1, keepdims=True)
    el = jnp.where((lane < N_EXPERTS) & ((lane >> int(math.log2(EXPERTS_PER_GROUP))) == g_idx), logits, neg)
    t1 = jnp.max(el, axis=-1, keepdims=True)
    i1 = jnp.min(jnp.where(el == t1, lane, big), axis=-1, keepdims=True)
    el2 = jnp.where(lane == i1, neg, el)
    t2 = jnp.max(el2, axis=-1, keepdims=True)
    i2 = jnp.min(jnp.where(el2 == t2, lane, big), axis=-1, keepdims=True)
    e21 = jnp.exp(t2 - t1)
    w1 = g_p / (1.0 + e21)
    w2 = g_p * e21 / (1.0 + e21)
    rt_ref[...] = jnp.where(lane == 0, w1, jnp.where(lane == 1, w2, 0.0))

    sh = int(math.log2(MOE_TILE))
    oh1 = lane == i1
    oh2 = lane == i2
    ind = jnp.where(oh1 | oh2, 1.0, 0.0)
    lrank = jnp.dot(tri_ref[...], ind.astype(BF16), preferred_element_type=F32).astype(jnp.int32)
    n_new = lrank[tm - 1:tm, :] + ind[tm - 1:tm, :].astype(jnp.int32)
    cnt = cnt_ref[...]
    cur = cur_ref[...]
    nfree = nused_ref[...]
    tiles_before = (cnt + (MOE_TILE - 1)) >> sh
    newf = ((cnt + n_new + (MOE_TILE - 1)) >> sh) - tiles_before
    newf8 = jnp.broadcast_to(newf.astype(BF16), (8, newf.shape[1]))
    pre = jnp.dot(newf8, upper_ref[...], preferred_element_type=F32)[0:1, :].astype(jnp.int32)
    new_tile = nfree + pre
    grank = cnt + lrank
    ptile = jnp.where((grank >> sh) < tiles_before, cur, new_tile)
    posfull = ((ptile << sh) + (grank & (MOE_TILE - 1))).astype(F32)
    pos1 = jnp.sum(jnp.where(oh1, posfull, 0.0), axis=-1, keepdims=True).astype(jnp.int32)
    pos2 = jnp.sum(jnp.where(oh2, posfull, 0.0), axis=-1, keepdims=True).astype(jnp.int32)
    pos_ref[...] = jnp.where(lane == 0, pos1, jnp.where(lane == 1, pos2, 0))

    el_r = lax.broadcasted_iota(jnp.int32, (ROUTER_LANES, ROUTER_LANES), 0)
    el_c = lax.broadcasted_iota(jnp.int32, (ROUTER_LANES, ROUTER_LANES), 1)
    as_col = lambda v: jnp.sum(jnp.where(el_r == el_c, jnp.broadcast_to(v, el_r.shape), 0.0),
                               axis=1, keepdims=True)
    tile_col = as_col(jnp.where(newf > 0, new_tile, -1).astype(F32)).astype(jnp.int32)
    tlane = lax.broadcasted_iota(jnp.int32, (ROUTER_LANES, te_ref.shape[1]), 1)
    erow = lax.broadcasted_iota(jnp.int32, (ROUTER_LANES, te_ref.shape[1]), 0).astype(F32)
    te_ref[...] += jnp.sum(jnp.where(tlane == tile_col, erow, 0.0), axis=0,
                           keepdims=True).astype(jnp.int32)
    cnt_ref[...] = cnt + n_new
    cur_ref[...] = jnp.where(newf > 0, new_tile, cur)
    nused_ref[...] = nfree + jnp.sum(newf.astype(F32), axis=-1, keepdims=True).astype(jnp.int32)


def _moe_tiles(t):
    return (2 * t) // MOE_TILE + N_EXPERTS


def _merge(x2, ya, yb, gates, gluw, glub, wba, wbb, wout, fng, rw, rb):
    t, d = x2.shape
    tm = MOE_TILE
    assert t % tm == 0
    te_lanes = -(-_moe_tiles(t) // LANES) * LANES
    rr = jnp.arange(tm)
    tri = (rr[None, :] < rr[:, None]).astype(BF16)
    ll = jnp.arange(ROUTER_LANES)
    upper = (ll[:, None] < ll[None, :]).astype(BF16)
    full = lambda a: pl.BlockSpec(a.shape, lambda i: (0,) * a.ndim)
    rowblk = lambda n: pl.BlockSpec((tm, n), lambda i: (i, 0))
    fixed = lambda n: pl.BlockSpec((1, n), lambda i: (0, 0))
    return pl.pallas_call(
        _merge_kernel,
        grid=(t // tm,),
        in_specs=[rowblk(d), rowblk(ya.shape[1]), rowblk(yb.shape[1]), rowblk(gates.shape[1]),
                  full(gluw), full(glub), full(wba), full(wbb), full(wout), full(fng), full(rw),
                  full(rb), full(tri), full(upper)],
        out_specs=[rowblk(d), rowblk(d), rowblk(ROUTER_LANES), rowblk(ROUTER_LANES),
                   fixed(te_lanes), fixed(ROUTER_LANES)],
        out_shape=[jax.ShapeDtypeStruct((t, d), F32),
                   jax.ShapeDtypeStruct((t, d), F32),
                   jax.ShapeDtypeStruct((t, ROUTER_LANES), F32),
                   jax.ShapeDtypeStruct((t, ROUTER_LANES), jnp.int32),
                   jax.ShapeDtypeStruct((1, te_lanes), jnp.int32),
                   jax.ShapeDtypeStruct((1, ROUTER_LANES), jnp.int32)],
        scratch_shapes=[pltpu.VMEM((1, ROUTER_LANES), jnp.int32),
                        pltpu.VMEM((1, ROUTER_LANES), jnp.int32)],
        compiler_params=pltpu.CompilerParams(dimension_semantics=("arbitrary",),
                                             vmem_limit_bytes=VMEM_LIMIT),
        name="merge",
    )(x2, ya, yb, gates, gluw, glub, wba, wbb, wout, fng, rw, rb, tri, upper)


def _dispatch_kernel(pos1_ref, pos2_ref, t_ref, zeros_ref, xs_ref, sem):
    del zeros_ref
    tm = t_ref.shape[0]
    base = pl.program_id(0) * tm

    def row_copy(i, pos_ref):
        return pltpu.make_async_copy(t_ref.at[pl.ds(i, 1), :],
                                     xs_ref.at[pl.ds(pos_ref[base + i], 1), :], sem)

    def start(i, carry):
        row_copy(i, pos1_ref).start()
        row_copy(i, pos2_ref).start()
        return carry

    lax.fori_loop(0, tm, start, 0, unroll=DMA_UNROLL)

    def wait(i, carry):
        row_copy(i, pos1_ref).wait()
        row_copy(i, pos2_ref).wait()
        return carry

    lax.fori_loop(0, tm, wait, 0, unroll=DMA_UNROLL)


def _dispatch(pos1, pos2, t_pack, n_rows, tm):
    t, hw = t_pack.shape
    zeros = jnp.zeros((n_rows, hw), t_pack.dtype)
    return pl.pallas_call(
        _dispatch_kernel,
        grid_spec=pltpu.PrefetchScalarGridSpec(
            num_scalar_prefetch=2,
            grid=(t // tm,),
            in_specs=[pl.BlockSpec((tm, hw), lambda i, p1, p2: (i, 0)),
                      pl.BlockSpec(memory_space=pl.ANY)],
            out_specs=pl.BlockSpec(memory_space=pl.ANY),
            scratch_shapes=[pltpu.SemaphoreType.DMA(())]),
        out_shape=jax.ShapeDtypeStruct((n_rows, hw), t_pack.dtype),
        input_output_aliases={3: 0},
        compiler_params=pltpu.CompilerParams(dimension_semantics=("arbitrary",),
                                             vmem_limit_bytes=VMEM_LIMIT),
        name="dispatch",
    )(pos1, pos2, t_pack, zeros)


def _experts_kernel(te_ref, nused_ref, xs_ref, wg_ref, wu_ref, wd_ref, y_ref):
    j = pl.program_id(0)

    @pl.when(j < nused_ref[0])
    def _():
        x = xs_ref[...].astype(BF16)
        hg = jnp.dot(x, wg_ref[...], preferred_element_type=F32)
        hid = hg * _sigmoid(hg) * jnp.dot(x, wu_ref[...], preferred_element_type=F32)
        y_ref[...] = _mm(hid, wd_ref[...])

    @pl.when(j >= nused_ref[0])
    def _():
        y_ref[...] = jnp.zeros_like(y_ref)


def _experts(tile_expert, n_used, xs, wg, wu, wd):
    n_rows, hw = xs.shape
    ne, d, de = wg.shape
    return pl.pallas_call(
        _experts_kernel,
        grid_spec=pltpu.PrefetchScalarGridSpec(
            num_scalar_prefetch=2,
            grid=(n_rows // MOE_TILE,),
            in_specs=[pl.BlockSpec((MOE_TILE, hw), lambda j, te, nu: (j, 0)),
                      pl.BlockSpec((None, d, de), lambda j, te, nu: (te[j], 0, 0)),
                      pl.BlockSpec((None, d, de), lambda j, te, nu: (te[j], 0, 0)),
                      pl.BlockSpec((None, de, d), lambda j, te, nu: (te[j], 0, 0))],
            out_specs=pl.BlockSpec((MOE_TILE, hw), lambda j, te, nu: (j, 0))),
        out_shape=jax.ShapeDtypeStruct((n_rows, hw), F32),
        compiler_params=pltpu.CompilerParams(dimension_semantics=("arbitrary",),
                                             vmem_limit_bytes=VMEM_LIMIT),
        name="experts",
    )(tile_expert, n_used, xs, wg, wu, wd)


def _ple_kernel(pos1_ref, pos2_ref, x_ref, rt_ref, p_ref, png_ref, wg_ref, wp_ref, fng_ref, y_ref,
                o_ref, ybuf, sem):
    i = pl.program_id(0)
    tm = x_ref.shape[0]

    def row_copy(tile, r, k, pos_ref):
        slot = tile % 2
        return pltpu.make_async_copy(y_ref.at[pl.ds(pos_ref[tile * tm + r], 1), :],
                                     ybuf.at[slot, k, pl.ds(r, 1), :], sem.at[slot])

    def gather(tile, op):
        def body(r, carry):
            op(row_copy(tile, r, 0, pos1_ref))
            op(row_copy(tile, r, 1, pos2_ref))
            return carry
        lax.fori_loop(0, tm, body, 0, unroll=DMA_UNROLL)

    @pl.when(i == 0)
    def _():
        gather(i, lambda c: c.start())

    @pl.when(i + 1 < pl.num_programs(0))
    def _():
        gather(i + 1, lambda c: c.start())

    gather(i, lambda c: c.wait())
    slot = i % 2
    rt = rt_ref[...]
    x2 = x_ref[...] + rt[:, 0:1] * ybuf[slot, 0] + rt[:, 1:2] * ybuf[slot, 1]
    hp = _rms_norm(x2, png_ref[...])
    gate = _sigmoid(_mm(hp, wg_ref[...]))
    x3 = x2 + gate * _mm(p_ref[...], wp_ref[...])
    o_ref[...] = _rms_norm(x3, fng_ref[...])


def _ple(pos1, pos2, x1, rt, p2, png, wg, wp, fng, y_pack, tm):
    t, d = x1.shape
    hw = y_pack.shape[1]
    full = lambda a: pl.BlockSpec(a.shape, lambda i, p1, p2_: (0,) * a.ndim)
    rowblk = lambda n: pl.BlockSpec((tm, n), lambda i, p1, p2_: (i, 0))
    return pl.pallas_call(
        _ple_kernel,
        grid_spec=pltpu.PrefetchScalarGridSpec(
            num_scalar_prefetch=2,
            grid=(t // tm,),
            in_specs=[rowblk(d), rowblk(rt.shape[1]), rowblk(p2.shape[1]),
                      full(png), full(wg), full(wp), full(fng),
                      pl.BlockSpec(memory_space=pl.ANY)],
            out_specs=rowblk(d),
            scratch_shapes=[pltpu.VMEM((2, 2, tm, hw), F32),
                            pltpu.SemaphoreType.DMA((2,))]),
        out_shape=jax.ShapeDtypeStruct((t, d), F32),
        compiler_params=pltpu.CompilerParams(dimension_semantics=("arbitrary",),
                                             vmem_limit_bytes=VMEM_LIMIT),
        name="ple",
    )(pos1, pos2, x1, rt, p2, png, wg, wp, fng, y_pack)


def _row_tile(t, want):
    tm = min(want, t)
    while t % tm:
        tm //= 2
    return tm


def _layer(x, p, mix_norm, w_in, mu_shift, rk_w0, rk_w_up, rk_a0, rk_a_up, rk_g_up,
           rk_k_k, rk_k_a, rk_r_k, rk_ln_g, rk_ln_b, s5_lam_re, s5_lam_im, s5_log_dt,
           s5_b_re, s5_b_im, s5_c_re, s5_c_im, s5_d, s5_glu_w, s5_glu_b,
           w_branch_a, w_branch_b, w_out, ffn_norm, router_group_w, router_group_b,
           router_expert_w, router_expert_b, exp_w_gate, exp_w_up, exp_w_down,
           ple_norm, ple_gate_w, ple_proj):
    b, s, d = x.shape
    t = b * s
    W = RWKV_WIDTH
    row = lambda a: a.reshape(1, -1).astype(F32)
    x2 = x.reshape(t, d)

    crw, us5, gates = _in_proj(x2, row(mix_norm), w_in.astype(BF16), _row_tile(t, 256))

    wl = jnp.zeros((LORA_COLS, 3 * W), F32)
    wl = wl.at[:DECAY_LORA, :W].set(rk_w_up)
    wl = wl.at[DECAY_LORA:DECAY_LORA + AAA_LORA, W:2 * W].set(rk_a_up)
    wl = wl.at[DECAY_LORA + AAA_LORA:, 2 * W:].set(rk_g_up)
    hid = jnp.arange(W) // RWKV_HEAD_DIM
    ones_bd = (hid[:, None] == hid[None, :]).astype(BF16)
    ya = _rwkv(crw.reshape(b, s, RWKV_COLS), row(mu_shift), wl.astype(BF16), row(rk_w0), row(rk_a0),
               row(rk_k_k), row(rk_k_a), row(rk_r_k), row(rk_ln_g), row(rk_ln_b), ones_bd,
               _rwkv_cum_matrix(_row_tile(s, RWKV_BLOCK)))

    s5_wa, s5_wo, plr, pli = _s5_mats(s5_lam_re, s5_lam_im, s5_log_dt, s5_b_re, s5_b_im,
                                      s5_c_re, s5_c_im, s5_d)
    yb = _s5(us5.reshape(b, s // S5_CHUNK, S5_CHUNK * S5_WIDTH), s5_wa, s5_wo, plr, pli)

    rw = jnp.zeros((d, ROUTER_LANES), F32)
    rw = rw.at[:, :N_EXPERTS].set(router_expert_w).at[:, N_EXPERTS:N_EXPERTS + N_GROUPS].set(router_group_w)
    rb = jnp.zeros((1, ROUTER_LANES), F32)
    rb = rb.at[0, :N_EXPERTS].set(router_expert_b).at[0, N_EXPERTS:N_EXPERTS + N_GROUPS].set(router_group_b)
    rw_hi = rw.astype(BF16)
    rw = jnp.concatenate([rw_hi, (rw - rw_hi.astype(F32)).astype(BF16)], axis=1)
    x1, t_ffn, rt, pos, tile_expert, n_used = _merge(
        x2, ya.reshape(t, W), yb.reshape(t, S5_WIDTH), gates, s5_glu_w.astype(BF16), row(s5_glu_b),
        w_branch_a.astype(BF16), w_branch_b.astype(BF16), w_out.astype(BF16), row(ffn_norm), rw, rb)

    n_tiles = _moe_tiles(t)
    pos1, pos2 = pos[:, 0], pos[:, 1]
    xs = _dispatch(pos1, pos2, t_ffn, n_tiles * MOE_TILE, _row_tile(t, 512))
    y_pack = _experts(tile_expert[0, :n_tiles], n_used[0, :1], xs, exp_w_gate.astype(BF16),
                      exp_w_up.astype(BF16), exp_w_down.astype(BF16))
    return (pos1, pos2, x1, rt, p.reshape(t, -1), row(ple_norm), ple_gate_w.astype(BF16),
            ple_proj.astype(BF16), y_pack)


def kernel(x, p, mix_norm, w_in, mu_shift, rk_w0, rk_w_up, rk_a0, rk_a_up, rk_g_up, rk_k_k, rk_k_a,
           rk_r_k, rk_ln_g, rk_ln_b, s5_lam_re, s5_lam_im, s5_log_dt, s5_b_re, s5_b_im, s5_c_re,
           s5_c_im, s5_d, s5_glu_w, s5_glu_b, w_branch_a, w_branch_b, w_out, ffn_norm,
           router_group_w, router_group_b, router_expert_w, router_expert_b, exp_w_gate, exp_w_up,
           exp_w_down, ple_norm, ple_gate_w, ple_proj, final_norm):
    b, s, d = x.shape
    depth = w_in.shape[0]
    assert depth == 1, "the final norm is fused into the last layer's PLE kernel"
    i = 0
    pos1, pos2, x1, rt, p2, png, wpg, wpp, y_pack = _layer(
        x, p[i], mix_norm[i], w_in[i], mu_shift[i], rk_w0[i], rk_w_up[i], rk_a0[i], rk_a_up[i],
        rk_g_up[i], rk_k_k[i], rk_k_a[i], rk_r_k[i], rk_ln_g[i], rk_ln_b[i], s5_lam_re[i],
        s5_lam_im[i], s5_log_dt[i], s5_b_re[i], s5_b_im[i], s5_c_re[i], s5_c_im[i], s5_d[i],
        s5_glu_w[i], s5_glu_b[i], w_branch_a[i], w_branch_b[i], w_out[i], ffn_norm[i],
        router_group_w[i], router_group_b[i], router_expert_w[i], router_expert_b[i],
        exp_w_gate[i], exp_w_up[i], exp_w_down[i], ple_norm[i], ple_gate_w[i], ple_proj[i])
    out = _ple(pos1, pos2, x1, rt, p2, png, wpg, wpp, final_norm.reshape(1, -1).astype(F32), y_pack,
               _row_tile(b * s, 256))
    return out.reshape(b, s, d)
```

```python
import functools
import math

import jax
import jax.numpy as jnp
from jax import lax
from jax.experimental import pallas as pl
from jax.experimental.pallas import tpu as pltpu

F32 = jnp.float32
BF16 = jnp.bfloat16

NORM_EPS = 1e-6
GN_EPS = 64e-5

RWKV_HEADS = 8
RWKV_HEAD_DIM = 64
RWKV_WIDTH = RWKV_HEADS * RWKV_HEAD_DIM
DECAY_LORA = 64
AAA_LORA = 64
GATE_LORA = 128
LORA_COLS = DECAY_LORA + AAA_LORA + GATE_LORA
RWKV_COLS = 3 * RWKV_WIDTH + LORA_COLS
S5_GROUPS = 16
S5_GROUP_CH = 16
S5_WIDTH = S5_GROUPS * S5_GROUP_CH
S5_STATE = 64
S5_ZW = 2 * S5_GROUPS * S5_STATE
N_GROUPS = 4
EXPERTS_PER_GROUP = 8
N_EXPERTS = N_GROUPS * EXPERTS_PER_GROUP

LANES = 128
RWKV_CHUNK = 64
RWKV_INV_BLOCK = 16
RWKV_BLOCK = 256
S5_CHUNK = 8
ROUTER_LANES = 128
MOE_TILE = 256
DMA_UNROLL = 8
VMEM_LIMIT = 56 * 1024 * 1024


def _mm(a, b):
    return jnp.dot(a.astype(BF16), b.astype(BF16), preferred_element_type=F32)


def _mm_nt(a, b):
    return lax.dot_general(a.astype(BF16), b.astype(BF16), (((1,), (1,)), ((), ())),
                           preferred_element_type=F32)


def _mm_tn(a, b):
    return lax.dot_general(a.astype(BF16), b.astype(BF16), (((0,), (0,)), ((), ())),
                           preferred_element_type=F32)


def _split2(x):
    hi = x.astype(BF16)
    return hi, (x - hi.astype(F32)).astype(BF16)


def _mm_split_lhs(a_bf16, x):
    hi, lo = _split2(x)
    d = lambda t: jnp.dot(a_bf16, t, preferred_element_type=F32)
    return d(hi) + d(lo)


def _mm_split_rhs(x, b_bf16):
    hi, lo = _split2(x)
    d = lambda t: jnp.dot(t, b_bf16, preferred_element_type=F32)
    return d(hi) + d(lo)


def _sigmoid(x):
    return 1.0 / (1.0 + jnp.exp(-x))


def _rms_norm(x, g):
    ms = jnp.mean(x * x, axis=-1, keepdims=True)
    return x * lax.rsqrt(ms + NORM_EPS) * g


def _in_proj_kernel(x_ref, g_ref, w_ref, crw_ref, us5_ref, gates_ref):
    h = _rms_norm(x_ref[...], g_ref[...]).astype(BF16)
    c0, c1 = RWKV_COLS, RWKV_COLS + S5_WIDTH
    crw_ref[...] = jnp.dot(h, w_ref[:, :c0], preferred_element_type=F32)
    us5_ref[...] = jnp.dot(h, w_ref[:, c0:c1], preferred_element_type=F32)
    gates_ref[...] = jnp.dot(h, w_ref[:, c1:], preferred_element_type=F32).astype(BF16)


def _in_proj(x2, g, w, tm):
    t, d = x2.shape
    n = w.shape[1]
    ng = n - RWKV_COLS - S5_WIDTH
    return pl.pallas_call(
        _in_proj_kernel,
        grid=(t // tm,),
        in_specs=[pl.BlockSpec((tm, d), lambda i: (i, 0)),
                  pl.BlockSpec((1, d), lambda i: (0, 0)),
                  pl.BlockSpec((d, n), lambda i: (0, 0))],
        out_specs=[pl.BlockSpec((tm, RWKV_COLS), lambda i: (i, 0)),
                   pl.BlockSpec((tm, S5_WIDTH), lambda i: (i, 0)),
                   pl.BlockSpec((tm, ng), lambda i: (i, 0))],
        out_shape=[jax.ShapeDtypeStruct((t, RWKV_COLS), F32),
                   jax.ShapeDtypeStruct((t, S5_WIDTH), F32),
                   jax.ShapeDtypeStruct((t, ng), BF16)],
        compiler_params=pltpu.CompilerParams(dimension_semantics=("arbitrary",),
                                             vmem_limit_bytes=VMEM_LIMIT),
        name="in_proj",
    )(x2, g, w)


def _rwkv_kernel(c_ref, mu_ref, wl_ref, w0_ref, a0_ref, kk_ref, ka_ref, rk_ref, lng_ref, lnb_ref,
                 ones_ref, cum_ref, o_ref, carry_ref, s_ref):
    C = RWKV_CHUNK
    W = RWKV_WIDTH
    npairs = W // LANES

    @pl.when(pl.program_id(1) == 0)
    def _():
        carry_ref[...] = jnp.zeros_like(carry_ref)
        s_ref[...] = jnp.zeros_like(s_ref)

    c = c_ref[...]
    R = c.shape[0]
    nchunks = R // C
    row = lax.broadcasted_iota(jnp.int32, (R, 1), 0)
    prev = jnp.where(row == 0, carry_ref[...], pltpu.roll(c, 1, 0))
    carry_ref[...] = c[R - 1:R, :]
    cs = c + (prev - c) * mu_ref[...]

    r = cs[:, 0:W]
    k = cs[:, W:2 * W]
    v = cs[:, 2 * W:3 * W]
    lin = cs[:, 3 * W:]
    llane = lax.broadcasted_iota(jnp.int32, lin.shape, 1)
    lact = jnp.where(llane < DECAY_LORA, jnp.tanh(lin),
                     jnp.where(llane < DECAY_LORA + AAA_LORA, lin, _sigmoid(lin)))
    lo = _mm(lact, wl_ref[...])
    zw = -(w0_ref[...] + lo[:, 0:W])
    softplus = jnp.maximum(zw, 0.0) + jnp.log(1.0 + jnp.exp(-jnp.abs(zw)))
    ld = -jnp.exp(-softplus - 0.5)
    a = _sigmoid(a0_ref[...] + lo[:, W:2 * W])
    g = lo[:, 2 * W:3 * W]

    ones_bd = ones_ref[...]
    segsum = lambda t: jnp.concatenate(
        [_mm_split_rhs(t[:, p * LANES:(p + 1) * LANES], ones_bd) for p in range(npairs)], axis=1)
    kk = k * kk_ref[...]
    kkn = kk / jnp.maximum(jnp.sqrt(segsum(kk * kk)), 1e-12)
    kmod = k * (1.0 + (a - 1.0) * ka_ref[...])

    cums = _mm_split_lhs(cum_ref[...], ld)
    cum = cums[:R]
    tot = cums[R:]
    inv = jnp.exp(-cum)
    tail = jnp.exp(tot - cum)
    At = -kkn * jnp.exp(cum - ld)
    Rt = r * jnp.exp(cum)
    kka = kkn * a
    Bt = kka * inv
    Kt = kmod * inv
    Bend = kka * tail
    Kend = kmod * tail
    pc = jnp.exp(tot)

    lane = lax.broadcasted_iota(jnp.int32, (C, LANES), 1)
    h0 = lane < RWKV_HEAD_DIM
    split = lambda t: jnp.concatenate([jnp.where(h0, t, 0.0), jnp.where(h0, 0.0, t)], axis=0)
    grow = lax.broadcasted_iota(jnp.int32, (C, 4 * C), 0)
    gcol = lax.broadcasted_iota(jnp.int32, (C, 4 * C), 1) & (C - 1)
    r2 = lax.broadcasted_iota(jnp.int32, (2 * C, 2 * C), 0)
    c2 = lax.broadcasted_iota(jnp.int32, (2 * C, 2 * C), 1)
    eye = (r2 == c2).astype(F32)
    blk_shift = int(math.log2(RWKV_INV_BLOCK))
    same_blk = (r2 >> blk_shift) == (c2 >> blk_shift)
    same_head = (r2 < C) == (c2 < C)
    same_head2 = jnp.concatenate([same_head, same_head], axis=0)
    zeros_c = jnp.zeros((C, LANES), F32)
    zeros_2c = jnp.zeros((2 * C, LANES), F32)

    units = [(ci, p) for ci in range(nchunks) for p in range(npairs)]
    blk = lambda t, u: t[u[0] * C:(u[0] + 1) * C, u[1] * LANES:(u[1] + 1) * LANES]
    each = lambda f, *ls: [f(*xs) for xs in zip(*ls)]

    lhs = [jnp.concatenate([blk(At, u), blk(Rt, u)], axis=0) for u in units]
    rhs = [jnp.concatenate([split(blk(Bt, u)), split(blk(Kt, u))], axis=0) for u in units]
    G = each(_mm_nt, lhs, rhs)
    a_row = [jnp.where(gcol < grow, t[:C], 0.0) for t in G]
    m_row = [jnp.where(gcol <= grow, t[C:], 0.0) for t in G]
    a_bd = [split(t[:, :2 * C]) for t in a_row]

    a_d = [jnp.where(same_blk, t, 0.0) for t in a_bd]
    a_off = each(lambda x, y: x - y, a_bd, a_d)
    dinv = [eye + t for t in a_d]
    pw = a_d
    for _ in range(blk_shift - 1):
        pw = each(_mm, pw, pw)
        dinv = each(lambda x, y: x + _mm(x, y), dinv, pw)
    n1 = each(_mm, dinv, a_off)
    n2 = each(_mm, n1, n1)
    n3 = each(_mm, n1, n2)
    tinv = each(lambda x1, x2, x3, dv: _mm(eye + x1 + x2 + x3, dv), n1, n2, n3, dinv)

    vp = [blk(v, u) for u in units]
    v_st = [split(t) for t in vp]
    rhs0 = each(lambda ar, vs: _mm(ar[:, 2 * C:], vs), a_row, v_st)
    wu = each(lambda t, l, r0: _mm(t, jnp.concatenate([split(l[:C]), split(r0)], axis=1)),
              tinv, lhs, rhs0)
    wu_lp = [t[:C] + t[C:] for t in wu]
    mn = each(lambda w_, v_, u: _mm_tn(
        jnp.concatenate([w_, jnp.concatenate([zeros_c, v_], axis=1)], axis=0),
        jnp.concatenate([blk(Bend, u), blk(Kend, u)], axis=0)), wu_lp, vp, units)
    mn = [jnp.where(same_head2, t, 0.0) for t in mn]
    qy = each(lambda m_, w_, vs: _mm(m_, jnp.concatenate(
        [w_, jnp.concatenate([zeros_2c, vs], axis=1)], axis=0)), m_row, wu, v_st)

    ys = [[None] * npairs for _ in range(nchunks)]
    states = [s_ref[p] for p in range(npairs)]
    for i, (ci, p) in enumerate(units):
        s_old = states[p]
        q = lhs[i][C:] + qy[i][:, :LANES]
        ys[ci][p] = _mm_nt(q, s_old) + qy[i][:, LANES:]
        states[p] = s_old * blk(pc, (ci, p))[0:1, :] + _mm(s_old, mn[i][:LANES]) + mn[i][LANES:]
    for p in range(npairs):
        s_ref[p] = states[p]

    y = jnp.concatenate([jnp.concatenate(t, axis=1) for t in ys], axis=0)
    inv_n = 1.0 / RWKV_HEAD_DIM
    mean = segsum(y) * inv_n
    d = y - mean
    var = segsum(d * d) * inv_n
    yn = d * lax.rsqrt(var + GN_EPS) * lng_ref[...] + lnb_ref[...]
    bonus = segsum(r * kmod * rk_ref[...]) * v
    o_ref[...] = ((yn + bonus) * g).astype(o_ref.dtype)


def _rwkv_cum_matrix(rows):
    t = jnp.arange(rows)
    same = (t[:, None] // RWKV_CHUNK) == (t[None, :] // RWKV_CHUNK)
    return jnp.concatenate([same & (t[None, :] <= t[:, None]), same], axis=0).astype(BF16)


def _rwkv(crw3, mu, wl, w0, a0, k_k, k_a, r_k, ln_g, ln_b, ones_bd, cum_mat):
    b, s, _ = crw3.shape
    R = cum_mat.shape[1]
    W = RWKV_WIDTH
    vec = lambda n: pl.BlockSpec((1, n), lambda i, j: (0, 0))
    return pl.pallas_call(
        _rwkv_kernel,
        grid=(b, s // R),
        in_specs=[pl.BlockSpec((None, R, RWKV_COLS), lambda i, j: (i, j, 0)),
                  vec(RWKV_COLS),
                  pl.BlockSpec((LORA_COLS, 3 * W), lambda i, j: (0, 0)),
                  vec(W), vec(W), vec(W), vec(W), vec(W), vec(W), vec(W),
                  pl.BlockSpec((LANES, LANES), lambda i, j: (0, 0)),
                  pl.BlockSpec((2 * R, R), lambda i, j: (0, 0))],
        out_specs=pl.BlockSpec((None, R, W), lambda i, j: (i, j, 0)),
        out_shape=jax.ShapeDtypeStruct((b, s, W), BF16),
        scratch_shapes=[pltpu.VMEM((1, RWKV_COLS), F32),
                        pltpu.VMEM((W // LANES, LANES, LANES), F32)],
        compiler_params=pltpu.CompilerParams(dimension_semantics=("arbitrary", "arbitrary"),
                                             vmem_limit_bytes=VMEM_LIMIT),
        name="rwkv",
    )(crw3, mu, wl, w0, a0, k_k, k_a, r_k, ln_g, ln_b, ones_bd, cum_mat)


def _s5_mats(lam_re, lam_im, log_dt, b_re, b_im, c_re, c_im, d_skip):
    L = S5_CHUNK
    G, N = lam_re.shape
    ch = b_re.shape[-1]
    dt = jnp.exp(log_dt)[:, None]
    lr, li = lam_re, lam_im
    mag = jnp.exp(lr * dt)
    lb_re, lb_im = mag * jnp.cos(li * dt), mag * jnp.sin(li * dt)
    den = lr * lr + li * li
    nr, ni = lb_re - 1.0, lb_im
    coef_re = (nr * lr + ni * li) / den
    coef_im = (ni * lr - nr * li) / den
    bb_re = coef_re[..., None] * b_re - coef_im[..., None] * b_im
    bb_im = coef_re[..., None] * b_im + coef_im[..., None] * b_re
    prs, pis = [jnp.ones_like(lb_re)], [jnp.zeros_like(lb_im)]
    for _ in range(L):
        pr_, pi_ = prs[-1], pis[-1]
        prs.append(pr_ * lb_re - pi_ * lb_im)
        pis.append(pr_ * lb_im + pi_ * lb_re)
    pr = jnp.stack(prs)
    pi = jnp.stack(pis)
    eye_g = jnp.eye(G, dtype=F32)

    def lam_bb(qr, qi):
        return (qr[..., None] * bb_re[None] - qi[..., None] * bb_im[None],
                qr[..., None] * bb_im[None] + qi[..., None] * bb_re[None])

    lre, lim = lam_bb(pr[:L], pi[:L])
    hp = lax.Precision.HIGHEST
    kern = (jnp.einsum('gon,lgni->lgio', c_re, lre, precision=hp)
            - jnp.einsum('gon,lgni->lgio', c_im, lim, precision=hp))
    bdk = (kern[:, :, :, None, :] * eye_g[None, :, None, :, None]).reshape(L, G * ch, G * ch)
    bdk = bdk.at[0].add(jnp.diag(d_skip.reshape(-1)))
    wre, wim = lam_bb(pr[:L][::-1], pi[:L][::-1])
    expand_in = lambda t: jnp.swapaxes(t, 2, 3)[:, :, :, None, :] * eye_g[None, :, None, :, None]
    w_in = jnp.stack([expand_in(wre), expand_in(wim)], axis=3).reshape(L, G * ch, 2 * G * N)
    qr, qi = pr[1:L + 1][:, :, None, :], pi[1:L + 1][:, :, None, :]
    o_re = c_re[None] * qr - c_im[None] * qi
    o_im = -c_re[None] * qi - c_im[None] * qr
    expand_out = lambda t: jnp.swapaxes(t, 2, 3)[:, :, :, None, :] * eye_g[None, :, None, :, None]
    w_out = jnp.stack([expand_out(o_re), expand_out(o_im)], axis=1).reshape(L, 2 * G * N, G * ch)
    plr = pr[L].reshape(1, G * N)
    pli = pi[L].reshape(1, G * N)
    lag = jnp.arange(L)[None, :] - jnp.arange(L)[:, None]
    toep = jnp.where((lag >= 0)[:, :, None, None], bdk[jnp.maximum(lag, 0)], 0.0)
    toep = jnp.swapaxes(toep, 1, 2).reshape(L * G * ch, L * G * ch)
    w_in_flat = w_in.reshape(L * G * ch, 2 * G * N)
    w_out_flat = jnp.swapaxes(w_out, 0, 1).reshape(2 * G * N, L * G * ch)
    w_a = jnp.concatenate([w_in_flat, toep], axis=1)
    return w_a.astype(BF16), w_out_flat.astype(BF16), plr, pli


def _s5_kernel(u_ref, wa_ref, wo_ref, plr_ref, pli_ref, o_ref, wloc_ref, zprev_ref):
    nch, L, w = u_ref.shape
    half = S5_ZW // 2

    r = _mm(u_ref[:, 0, :], wa_ref[0:w, :])
    for j in range(1, L):
        r = r + _mm(u_ref[:, j, :], wa_ref[j * w:(j + 1) * w, :])
    wloc_ref[...] = r[:, :S5_ZW]
    y_lag = r[:, S5_ZW:]

    plr = plr_ref[...]
    pli = pli_ref[...]

    def step(ci, z):
        zprev_ref[pl.ds(ci, 1), :] = z
        zr, zi = z[:, :half], z[:, half:]
        nz = jnp.concatenate([plr * zr - pli * zi, plr * zi + pli * zr], axis=1)
        return nz + wloc_ref[pl.ds(ci, 1), :]

    lax.fori_loop(0, nch, step, jnp.zeros((1, S5_ZW), F32))
    y = y_lag + _mm(zprev_ref[...], wo_ref[...])
    for j in range(L):
        o_ref[:, j, :] = y[:, j * w:(j + 1) * w]


def _s5(u4, w_a, w_o, plr, pli):
    b, nch, L, w = u4.shape
    const = lambda a: pl.BlockSpec(a.shape, lambda i: (0,) * a.ndim, pipeline_mode=pl.Buffered(1))
    blk = pl.BlockSpec((None, nch, L, w), lambda i: (i, 0, 0, 0))
    return pl.pallas_call(
        _s5_kernel,
        grid=(b,),
        in_specs=[blk, const(w_a), const(w_o), const(plr), const(pli)],
        out_specs=blk,
        out_shape=jax.ShapeDtypeStruct(u4.shape, F32),
        scratch_shapes=[pltpu.VMEM((nch, S5_ZW), F32),
                        pltpu.VMEM((nch, S5_ZW), F32)],
        compiler_params=pltpu.CompilerParams(dimension_semantics=("arbitrary",),
                                             vmem_limit_bytes=VMEM_LIMIT),
        name="s5",
    )(u4, w_a, w_o, plr, pli)


def _merge_kernel(x_ref, ya_ref, yb_ref, gates_ref, gluw_ref, glub_ref, wba_ref, wbb_ref, wout_ref,
                  fng_ref, rw_ref, rb_ref, tri_ref, upper_ref, x1_ref, t_ref, rt_ref, pos_ref,
                  te_ref, fill_ref, nused_ref, cnt_ref, cur_ref):
    d = x_ref.shape[1]
    tm = x_ref.shape[0]

    @pl.when(pl.program_id(0) == 0)
    def _():
        cnt_ref[...] = jnp.zeros_like(cnt_ref)
        cur_ref[...] = jnp.zeros_like(cur_ref)
        te_ref[...] = jnp.zeros_like(te_ref)
        fill_ref[...] = jnp.zeros_like(fill_ref)
        nused_ref[...] = jnp.zeros_like(nused_ref)

    y_a = jnp.dot(ya_ref[...], wba_ref[...], preferred_element_type=F32)
    ys = yb_ref[...]
    z = 0.5 * ys * (1.0 + jnp.tanh(math.sqrt(2.0 / math.pi) * (ys + 0.044715 * (ys * ys * ys))))
    z = z * _sigmoid(_mm(z, gluw_ref[...]) + glub_ref[...])
    y_b = _mm(z, wbb_ref[...])
    gates = gates_ref[...].astype(F32)
    merged = _sigmoid(gates[:, :d]) * y_a + _sigmoid(gates[:, d:]) * y_b
    x1 = x_ref[...] + _mm(merged, wout_ref[...])
    x1_ref[...] = x1
    t = _rms_norm(x1, fng_ref[...])
    t_hi = t.astype(BF16)
    t_ref[...] = t

    t_lo = (t - t_hi.astype(F32)).astype(BF16)
    hh_hl = jnp.dot(t_hi, rw_ref[...], preferred_element_type=F32)
    lh = jnp.dot(t_lo, rw_ref[:, :ROUTER_LANES], preferred_element_type=F32)
    logits = hh_hl[:, :ROUTER_LANES] + hh_hl[:, ROUTER_LANES:] + lh + rb_ref[...]
    lane = lax.broadcasted_iota(jnp.int32, logits.shape, 1)
    neg = -jnp.inf
    big = jnp.int32(1 << 20)
    is_g = (lane >= N_EXPERTS) & (lane < N_EXPERTS + N_GROUPS)
    gl = jnp.where(is_g, logits, neg)
    gmax = jnp.max(gl, axis=-1, keepdims=True)
    g_p = 1.0 / jnp.sum(jnp.exp(gl - gmax), axis=-1, keepdims=True)
    g_idx = jnp.min(jnp.where(gl == gmax, lane - N_EXPERTS, big), axis=-1, keepdims=True)
    el = jnp.where((lane < N_EXPERTS) & ((lane >> int(math.log2(EXPERTS_PER_GROUP))) == g_idx), logits, neg)
    t1 = jnp.max(el, axis=-1, keepdims=True)
    i1 = jnp.min(jnp.where(el == t1, lane, big), axis=-1, keepdims=True)
    el2 = jnp.where(lane == i1, neg, el)
    t2 = jnp.max(el2, axis=-1, keepdims=True)
    i2 = jnp.min(jnp.where(el2 == t2, lane, big), axis=-1, keepdims=True)
    e21 = jnp.exp(t2 - t1)
    w1 = g_p / (1.0 + e21)
    w2 = g_p * e21 / (1.0 + e21)
    rt_ref[...] = jnp.where(lane == 0, w1, jnp.where(lane == 1, w2, 0.0))

    sh = int(math.log2(MOE_TILE))
    oh1 = lane == i1
    oh2 = lane == i2
    ind = jnp.where(oh1 | oh2, 1.0, 0.0)
    lrank = jnp.dot(tri_ref[...], ind.astype(BF16), preferred_element_type=F32).astype(jnp.int32)
    n_new = lrank[tm - 1:tm, :] + ind[tm - 1:tm, :].astype(jnp.int32)
    cnt = cnt_ref[...]
    cur = cur_ref[...]
    nfree = nused_ref[...]
    tiles_before = (cnt + (MOE_TILE - 1)) >> sh
    newf = ((cnt + n_new + (MOE_TILE - 1)) >> sh) - tiles_before
    newf8 = jnp.broadcast_to(newf.astype(BF16), (8, newf.shape[1]))
    pre = jnp.dot(newf8, upper_ref[...], preferred_element_type=F32)[0:1, :].astype(jnp.int32)
    new_tile = nfree + pre
    grank = cnt + lrank
    ptile = jnp.where((grank >> sh) < tiles_before, cur, new_tile)
    posfull = ((ptile << sh) + (grank & (MOE_TILE - 1))).astype(F32)
    pos1 = jnp.sum(jnp.where(oh1, posfull, 0.0), axis=-1, keepdims=True)
    pos2 = jnp.sum(jnp.where(oh2, posfull, 0.0), axis=-1, keepdims=True)
    tr = lax.broadcasted_iota(jnp.int32, (tm, tm), 0)
    tc = lax.broadcasted_iota(jnp.int32, (tm, tm), 1)
    as_row = lambda v: jnp.sum(jnp.where(tr == tc, jnp.broadcast_to(v, (tm, tm)), 0.0),
                               axis=0, keepdims=True)
    sub = lax.broadcasted_iota(jnp.int32, pos_ref.shape, 0)
    pos_ref[...] = jnp.where(sub == 0, as_row(pos1),
                             jnp.where(sub == 1, as_row(pos2), 0.0)).astype(jnp.int32)

    el_r = lax.broadcasted_iota(jnp.int32, (ROUTER_LANES, ROUTER_LANES), 0)
    el_c = lax.broadcasted_iota(jnp.int32, (ROUTER_LANES, ROUTER_LANES), 1)
    as_col = lambda v: jnp.sum(jnp.where(el_r == el_c, jnp.broadcast_to(v, el_r.shape), 0.0),
                               axis=1, keepdims=True)
    tile_col = as_col(jnp.where(newf > 0, new_tile, -1).astype(F32)).astype(jnp.int32)
    tlane = lax.broadcasted_iota(jnp.int32, (ROUTER_LANES, te_ref.shape[1]), 1)
    erow = lax.broadcasted_iota(jnp.int32, (ROUTER_LANES, te_ref.shape[1]), 0).astype(F32)
    te_ref[...] += jnp.sum(jnp.where(tlane == tile_col, erow, 0.0), axis=0,
                           keepdims=True).astype(jnp.int32)
    to_cur = jnp.minimum(n_new, (tiles_before << sh) - cnt)
    cur_col = as_col(cur.astype(F32)).astype(jnp.int32)
    added = (jnp.where(tlane == cur_col, as_col(to_cur.astype(F32)), 0.0)
             + jnp.where(tlane == tile_col, as_col((n_new - to_cur).astype(F32)), 0.0))
    fill_ref[...] += jnp.sum(added, axis=0, keepdims=True).astype(jnp.int32)
    cnt_ref[...] = cnt + n_new
    cur_ref[...] = jnp.where(newf > 0, new_tile, cur)
    nused_ref[...] = nfree + jnp.sum(newf.astype(F32), axis=-1, keepdims=True).astype(jnp.int32)


def _moe_tiles(t):
    return (2 * t) // MOE_TILE + N_EXPERTS


def _merge(x2, ya, yb, gates, gluw, glub, wba, wbb, wout, fng, rw, rb):
    t, d = x2.shape
    tm = MOE_TILE
    assert t % tm == 0
    te_lanes = -(-_moe_tiles(t) // LANES) * LANES
    rr = jnp.arange(tm)
    tri = (rr[None, :] < rr[:, None]).astype(BF16)
    ll = jnp.arange(ROUTER_LANES)
    upper = (ll[:, None] < ll[None, :]).astype(BF16)
    full = lambda a: pl.BlockSpec(a.shape, lambda i: (0,) * a.ndim)
    rowblk = lambda n: pl.BlockSpec((tm, n), lambda i: (i, 0))
    fixed = lambda n: pl.BlockSpec((1, n), lambda i: (0, 0))
    return pl.pallas_call(
        _merge_kernel,
        grid=(t // tm,),
        in_specs=[rowblk(d), rowblk(ya.shape[1]), rowblk(yb.shape[1]), rowblk(gates.shape[1]),
                  full(gluw), full(glub), full(wba), full(wbb), full(wout), full(fng), full(rw),
                  full(rb), full(tri), full(upper)],
        out_specs=[rowblk(d), rowblk(d), rowblk(ROUTER_LANES),
                   pl.BlockSpec((None, 8, tm), lambda i: (i, 0, 0)),
                   fixed(te_lanes), fixed(te_lanes), fixed(ROUTER_LANES)],
        out_shape=[jax.ShapeDtypeStruct((t, d), F32),
                   jax.ShapeDtypeStruct((t, d), F32),
                   jax.ShapeDtypeStruct((t, ROUTER_LANES), F32),
                   jax.ShapeDtypeStruct((t // tm, 8, tm), jnp.int32),
                   jax.ShapeDtypeStruct((1, te_lanes), jnp.int32),
                   jax.ShapeDtypeStruct((1, te_lanes), jnp.int32),
                   jax.ShapeDtypeStruct((1, ROUTER_LANES), jnp.int32)],
        scratch_shapes=[pltpu.VMEM((1, ROUTER_LANES), jnp.int32),
                        pltpu.VMEM((1, ROUTER_LANES), jnp.int32)],
        compiler_params=pltpu.CompilerParams(dimension_semantics=("arbitrary",),
                                             vmem_limit_bytes=VMEM_LIMIT),
        name="merge",
    )(x2, ya, yb, gates, gluw, glub, wba, wbb, wout, fng, rw, rb, tri, upper)


def _dispatch_kernel(pos1_ref, pos2_ref, fill_ref, t_ref, xs_ref, zrow, sem, zsem):
    tm = t_ref.shape[0]
    base = pl.program_id(0) * tm

    @pl.when(pl.program_id(0) == pl.num_programs(0) - 1)
    def _():
        zrow[...] = jnp.zeros_like(zrow)

        def over_tail_rows(op):
            def tile(j, carry):
                def one(r, c):
                    op(pltpu.make_async_copy(zrow.at[pl.ds(0, 1), :],
                                             xs_ref.at[pl.ds(j * MOE_TILE + r, 1), :], zsem))
                    return c
                return lax.fori_loop(fill_ref[j], MOE_TILE, one, carry)
            lax.fori_loop(0, fill_ref.shape[0], tile, 0)

        over_tail_rows(lambda c: c.start())
        over_tail_rows(lambda c: c.wait())

    def start(i, carry):
        for pos_ref in (pos1_ref, pos2_ref):
            pltpu.make_async_copy(t_ref.at[pl.ds(i, 1), :],
                                  xs_ref.at[pl.ds(pos_ref[base + i], 1), :], sem).start()
        return carry

    lax.fori_loop(0, tm, start, 0, unroll=DMA_UNROLL)
    for _ in range(2):
        pltpu.make_async_copy(t_ref, xs_ref.at[pl.ds(0, tm), :], sem).wait()


def _dispatch(pos1, pos2, tile_fill, t_ffn, tm):
    t, d = t_ffn.shape
    n_rows = tile_fill.shape[0] * MOE_TILE
    return pl.pallas_call(
        _dispatch_kernel,
        grid_spec=pltpu.PrefetchScalarGridSpec(
            num_scalar_prefetch=3,
            grid=(t // tm,),
            in_specs=[pl.BlockSpec((tm, d), lambda i, p1, p2, fl: (i, 0))],
            out_specs=pl.BlockSpec(memory_space=pl.ANY),
            scratch_shapes=[pltpu.VMEM((8, d), t_ffn.dtype),
                            pltpu.SemaphoreType.DMA(()),
                            pltpu.SemaphoreType.DMA(())]),
        out_shape=jax.ShapeDtypeStruct((n_rows, d), t_ffn.dtype),
        compiler_params=pltpu.CompilerParams(dimension_semantics=("arbitrary",),
                                             vmem_limit_bytes=VMEM_LIMIT),
        name="dispatch",
    )(pos1, pos2, tile_fill, t_ffn)


def _experts_kernel(te_ref, nused_ref, xs_ref, wg_ref, wu_ref, wd_ref, y_ref):
    j = pl.program_id(0)

    @pl.when(j < nused_ref[0])
    def _():
        x = xs_ref[...].astype(BF16)
        hg = _mm(x, wg_ref[...])
        hid = hg * _sigmoid(hg) * _mm(x, wu_ref[...])
        y_ref[...] = _mm(hid, wd_ref[...])

    @pl.when(j >= nused_ref[0])
    def _():
        y_ref[...] = jnp.zeros_like(y_ref)


def _experts(tile_expert, n_used, xs, wg, wu, wd):
    n_rows, hw = xs.shape
    ne, d, de = wg.shape
    return pl.pallas_call(
        _experts_kernel,
        grid_spec=pltpu.PrefetchScalarGridSpec(
            num_scalar_prefetch=2,
            grid=(n_rows // MOE_TILE,),
            in_specs=[pl.BlockSpec((MOE_TILE, hw), lambda j, te, nu: (j, 0)),
                      pl.BlockSpec((None, d, de), lambda j, te, nu: (te[j], 0, 0)),
                      pl.BlockSpec((None, d, de), lambda j, te, nu: (te[j], 0, 0)),
                      pl.BlockSpec((None, de, d), lambda j, te, nu: (te[j], 0, 0))],
            out_specs=pl.BlockSpec((MOE_TILE, hw), lambda j, te, nu: (j, 0))),
        out_shape=jax.ShapeDtypeStruct((n_rows, hw), F32),
        compiler_params=pltpu.CompilerParams(dimension_semantics=("arbitrary",),
                                             vmem_limit_bytes=VMEM_LIMIT),
        name="experts",
    )(tile_expert, n_used, xs, wg, wu, wd)


def _ple_kernel(pos1_ref, pos2_ref, x_ref, rt_ref, p_ref, png_ref, wg_ref, wp_ref, fng_ref, y_ref,
                o_ref, ybuf, sem):
    i = pl.program_id(0)
    tm = x_ref.shape[0]

    def gather(tile):
        slot = tile % 2

        def body(r, carry):
            for k, pos_ref in enumerate((pos1_ref, pos2_ref)):
                pltpu.make_async_copy(y_ref.at[pl.ds(pos_ref[tile * tm + r], 1), :],
                                      ybuf.at[slot, k, pl.ds(r, 1), :], sem.at[slot]).start()
            return carry
        lax.fori_loop(0, tm, body, 0, unroll=DMA_UNROLL)

    @pl.when(i == 0)
    def _():
        gather(i)

    @pl.when(i + 1 < pl.num_programs(0))
    def _():
        gather(i + 1)

    slot = i % 2
    for k in range(2):
        pltpu.make_async_copy(y_ref.at[pl.ds(0, tm), :], ybuf.at[slot, k], sem.at[slot]).wait()
    rt = rt_ref[...]
    x2 = x_ref[...] + rt[:, 0:1] * ybuf[slot, 0] + rt[:, 1:2] * ybuf[slot, 1]
    hp = _rms_norm(x2, png_ref[...])
    gate = _sigmoid(_mm(hp, wg_ref[...]))
    x3 = x2 + gate * _mm(p_ref[...], wp_ref[...])
    o_ref[...] = _rms_norm(x3, fng_ref[...])


def _ple(pos1, pos2, x1, rt, p2, png, wg, wp, fng, y_pack, tm):
    t, d = x1.shape
    hw = y_pack.shape[1]
    full = lambda a: pl.BlockSpec(a.shape, lambda i, p1, p2_: (0,) * a.ndim)
    rowblk = lambda n: pl.BlockSpec((tm, n), lambda i, p1, p2_: (i, 0))
    return pl.pallas_call(
        _ple_kernel,
        grid_spec=pltpu.PrefetchScalarGridSpec(
            num_scalar_prefetch=2,
            grid=(t // tm,),
            in_specs=[rowblk(d), rowblk(rt.shape[1]), rowblk(p2.shape[1]),
                      full(png), full(wg), full(wp), full(fng),
                      pl.BlockSpec(memory_space=pl.ANY)],
            out_specs=rowblk(d),
            scratch_shapes=[pltpu.VMEM((2, 2, tm, hw), F32),
                            pltpu.SemaphoreType.DMA((2,))]),
        out_shape=jax.ShapeDtypeStruct((t, d), F32),
        compiler_params=pltpu.CompilerParams(dimension_semantics=("arbitrary",),
                                             vmem_limit_bytes=VMEM_LIMIT),
        name="ple",
    )(pos1, pos2, x1, rt, p2, png, wg, wp, fng, y_pack)


def _row_tile(t, want):
    tm = min(want, t)
    while t % tm:
        tm //= 2
    return tm


def _layer(x, p, mix_norm, w_in, mu_shift, rk_w0, rk_w_up, rk_a0, rk_a_up, rk_g_up,
           rk_k_k, rk_k_a, rk_r_k, rk_ln_g, rk_ln_b, s5_lam_re, s5_lam_im, s5_log_dt,
           s5_b_re, s5_b_im, s5_c_re, s5_c_im, s5_d, s5_glu_w, s5_glu_b,
           w_branch_a, w_branch_b, w_out, ffn_norm, router_group_w, router_group_b,
           router_expert_w, router_expert_b, exp_w_gate, exp_w_up, exp_w_down,
           ple_norm, ple_gate_w, ple_proj):
    b, s, d = x.shape
    t = b * s
    W = RWKV_WIDTH
    row = lambda a: a.reshape(1, -1).astype(F32)
    x2 = x.reshape(t, d)

    crw, us5, gates = _in_proj(x2, row(mix_norm), w_in.astype(BF16), _row_tile(t, 256))

    wl = jnp.zeros((LORA_COLS, 3 * W), F32)
    wl = wl.at[:DECAY_LORA, :W].set(rk_w_up)
    wl = wl.at[DECAY_LORA:DECAY_LORA + AAA_LORA, W:2 * W].set(rk_a_up)
    wl = wl.at[DECAY_LORA + AAA_LORA:, 2 * W:].set(rk_g_up)
    hid = jnp.arange(LANES) // RWKV_HEAD_DIM
    ones_bd = (hid[:, None] == hid[None, :]).astype(BF16)
    ya = _rwkv(crw.reshape(b, s, RWKV_COLS), row(mu_shift), wl.astype(BF16), row(rk_w0), row(rk_a0),
               row(rk_k_k), row(rk_k_a), row(rk_r_k), row(rk_ln_g), row(rk_ln_b), ones_bd,
               _rwkv_cum_matrix(_row_tile(s, RWKV_BLOCK)))

    s5_wa, s5_wo, plr, pli = _s5_mats(s5_lam_re, s5_lam_im, s5_log_dt, s5_b_re, s5_b_im,
                                      s5_c_re, s5_c_im, s5_d)
    yb = _s5(us5.reshape(b, s // S5_CHUNK, S5_CHUNK, S5_WIDTH), s5_wa, s5_wo, plr, pli)

    rw = jnp.zeros((d, ROUTER_LANES), F32)
    rw = rw.at[:, :N_EXPERTS].set(router_expert_w).at[:, N_EXPERTS:N_EXPERTS + N_GROUPS].set(router_group_w)
    rb = jnp.zeros((1, ROUTER_LANES), F32)
    rb = rb.at[0, :N_EXPERTS].set(router_expert_b).at[0, N_EXPERTS:N_EXPERTS + N_GROUPS].set(router_group_b)
    rw_hi = rw.astype(BF16)
    rw = jnp.concatenate([rw_hi, (rw - rw_hi.astype(F32)).astype(BF16)], axis=1)
    x1, t_ffn, rt, pos, tile_expert, tile_fill, n_used = _merge(
        x2, ya.reshape(t, W), yb.reshape(t, S5_WIDTH), gates, s5_glu_w.astype(BF16), row(s5_glu_b),
        w_branch_a.astype(BF16), w_branch_b.astype(BF16), w_out.astype(BF16), row(ffn_norm), rw, rb)

    n_tiles = _moe_tiles(t)
    pos1, pos2 = pos[:, 0, :].reshape(t), pos[:, 1, :].reshape(t)
    xs = _dispatch(pos1, pos2, tile_fill[0, :n_tiles], t_ffn, _row_tile(t, 512))
    y_pack = _experts(tile_expert[0, :n_tiles], n_used[0, :1], xs, exp_w_gate, exp_w_up, exp_w_down)
    return (pos1, pos2, x1, rt, p.reshape(t, -1), row(ple_norm), ple_gate_w.astype(BF16),
            ple_proj.astype(BF16), y_pack)


def kernel(x, p, mix_norm, w_in, mu_shift, rk_w0, rk_w_up, rk_a0, rk_a_up, rk_g_up, rk_k_k, rk_k_a,
           rk_r_k, rk_ln_g, rk_ln_b, s5_lam_re, s5_lam_im, s5_log_dt, s5_b_re, s5_b_im, s5_c_re,
           s5_c_im, s5_d, s5_glu_w, s5_glu_b, w_branch_a, w_branch_b, w_out, ffn_norm,
           router_group_w, router_group_b, router_expert_w, router_expert_b, exp_w_gate, exp_w_up,
           exp_w_down, ple_norm, ple_gate_w, ple_proj, final_norm):
    b, s, d = x.shape
    depth = w_in.shape[0]
    assert depth == 1, "the final norm is fused into the last layer's PLE kernel"
    i = 0
    pos1, pos2, x1, rt, p2, png, wpg, wpp, y_pack = _layer(
        x, p[i], mix_norm[i], w_in[i], mu_shift[i], rk_w0[i], rk_w_up[i], rk_a0[i], rk_a_up[i],
        rk_g_up[i], rk_k_k[i], rk_k_a[i], rk_r_k[i], rk_ln_g[i], rk_ln_b[i], s5_lam_re[i],
        s5_lam_im[i], s5_log_dt[i], s5_b_re[i], s5_b_im[i], s5_c_re[i], s5_c_im[i], s5_d[i],
        s5_glu_w[i], s5_glu_b[i], w_branch_a[i], w_branch_b[i], w_out[i], ffn_norm[i],
        router_group_w[i], router_group_b[i], router_expert_w[i], router_expert_b[i],
        exp_w_gate[i], exp_w_up[i], exp_w_down[i], ple_norm[i], ple_gate_w[i], ple_proj[i])
    out = _ple(pos1, pos2, x1, rt, p2, png, wpg, wpp, final_norm.reshape(1, -1).astype(F32), y_pack,
               _row_tile(b * s, 256))
    return out.reshape(b, s, d)
```

```python
import functools
import math

import jax
import jax.numpy as jnp
from jax import lax
from jax.experimental import pallas as pl
from jax.experimental.pallas import tpu as pltpu

F32 = jnp.float32
BF16 = jnp.bfloat16

NORM_EPS = 1e-6
GN_EPS = 64e-5

RWKV_HEADS = 8
RWKV_HEAD_DIM = 64
RWKV_WIDTH = RWKV_HEADS * RWKV_HEAD_DIM
DECAY_LORA = 64
AAA_LORA = 64
GATE_LORA = 128
LORA_COLS = DECAY_LORA + AAA_LORA + GATE_LORA
RWKV_COLS = 3 * RWKV_WIDTH + LORA_COLS
S5_GROUPS = 16
S5_GROUP_CH = 16
S5_WIDTH = S5_GROUPS * S5_GROUP_CH
S5_STATE = 64
S5_ZW = 2 * S5_GROUPS * S5_STATE
N_GROUPS = 4
EXPERTS_PER_GROUP = 8
N_EXPERTS = N_GROUPS * EXPERTS_PER_GROUP

LANES = 128
SUBLANES = 8
RWKV_CHUNK = 64
RWKV_INV_BLOCK = 16
RWKV_BLOCK = 256
S5_CHUNK = 8
ROUTER_LANES = 128
MOE_TILE = 512
MERGE_TILE = 256
DMA_UNROLL = 8
VMEM_LIMIT = 56 * 1024 * 1024


def _mm(a, b):
    return jnp.dot(a.astype(BF16), b.astype(BF16), preferred_element_type=F32)


def _mm_nt(a, b):
    return lax.dot_general(a.astype(BF16), b.astype(BF16), (((1,), (1,)), ((), ())),
                           preferred_element_type=F32)


def _mm_tn(a, b):
    return lax.dot_general(a.astype(BF16), b.astype(BF16), (((0,), (0,)), ((), ())),
                           preferred_element_type=F32)


def _split2(x):
    hi = x.astype(BF16)
    return hi, (x - hi.astype(F32)).astype(BF16)


def _mm_split_lhs(a_bf16, x):
    hi, lo = _split2(x)
    d = lambda t: jnp.dot(a_bf16, t, preferred_element_type=F32)
    return d(hi) + d(lo)


def _mm_split_rhs(x, b_bf16):
    hi, lo = _split2(x)
    d = lambda t: jnp.dot(t, b_bf16, preferred_element_type=F32)
    return d(hi) + d(lo)


def _sigmoid(x):
    return 0.5 * jnp.tanh(0.5 * x) + 0.5


def _rms_norm(x, g):
    ms = jnp.mean(x * x, axis=-1, keepdims=True)
    return x * lax.rsqrt(ms + NORM_EPS) * g


def _in_proj_kernel(x_ref, g_ref, w_ref, crw_ref, us5_ref, gates_ref):
    h = _rms_norm(x_ref[...], g_ref[...]).astype(BF16)
    c0, c1 = RWKV_COLS, RWKV_COLS + S5_WIDTH
    crw_ref[...] = jnp.dot(h, w_ref[:, :c0], preferred_element_type=F32)
    us5_ref[...] = jnp.dot(h, w_ref[:, c0:c1], preferred_element_type=F32)
    gates_ref[...] = jnp.dot(h, w_ref[:, c1:], preferred_element_type=F32).astype(BF16)


def _in_proj(x2, g, w, tm):
    t, d = x2.shape
    n = w.shape[1]
    ng = n - RWKV_COLS - S5_WIDTH
    return pl.pallas_call(
        _in_proj_kernel,
        grid=(t // tm,),
        in_specs=[pl.BlockSpec((tm, d), lambda i: (i, 0)),
                  pl.BlockSpec((1, d), lambda i: (0, 0)),
                  pl.BlockSpec((d, n), lambda i: (0, 0))],
        out_specs=[pl.BlockSpec((tm, RWKV_COLS), lambda i: (i, 0)),
                   pl.BlockSpec((tm, S5_WIDTH), lambda i: (i, 0)),
                   pl.BlockSpec((tm, ng), lambda i: (i, 0))],
        out_shape=[jax.ShapeDtypeStruct((t, RWKV_COLS), F32),
                   jax.ShapeDtypeStruct((t, S5_WIDTH), F32),
                   jax.ShapeDtypeStruct((t, ng), BF16)],
        compiler_params=pltpu.CompilerParams(dimension_semantics=("arbitrary",),
                                             vmem_limit_bytes=VMEM_LIMIT),
        name="in_proj",
    )(x2, g, w)


def _rwkv_kernel(c_ref, mu_ref, wl_ref, w0_ref, a0_ref, kk_ref, ka_ref, rk_ref, lng_ref, lnb_ref,
                 ones_ref, cum_ref, o_ref, carry_ref, s_ref):
    C = RWKV_CHUNK
    W = RWKV_WIDTH
    npairs = W // LANES

    @pl.when(pl.program_id(1) == 0)
    def _():
        carry_ref[...] = jnp.zeros_like(carry_ref)
        s_ref[...] = jnp.zeros_like(s_ref)

    c = c_ref[...]
    R = c.shape[0]
    nchunks = R // C
    row = lax.broadcasted_iota(jnp.int32, (R, 1), 0)
    prev = jnp.where(row == 0, carry_ref[...], pltpu.roll(c, 1, 0))
    carry_ref[...] = c[R - 1:R, :]
    cs = c + (prev - c) * mu_ref[...]

    r = cs[:, 0:W]
    k = cs[:, W:2 * W]
    v = cs[:, 2 * W:3 * W]
    lin = cs[:, 3 * W:]
    llane = lax.broadcasted_iota(jnp.int32, lin.shape, 1)
    lact = jnp.where(llane < DECAY_LORA, jnp.tanh(lin),
                     jnp.where(llane < DECAY_LORA + AAA_LORA, lin, _sigmoid(lin)))
    lo = _mm(lact, wl_ref[...])
    zw = -(w0_ref[...] + lo[:, 0:W])
    softplus = jnp.maximum(zw, 0.0) + jnp.log(1.0 + jnp.exp(-jnp.abs(zw)))
    ld = -jnp.exp(-softplus - 0.5)
    a = _sigmoid(a0_ref[...] + lo[:, W:2 * W])
    g = lo[:, 2 * W:3 * W]

    ones_bd = ones_ref[...]
    segsum = lambda t: jnp.concatenate(
        [_mm_split_rhs(t[:, p * LANES:(p + 1) * LANES], ones_bd) for p in range(npairs)], axis=1)
    kk = k * kk_ref[...]
    kkn = kk / jnp.maximum(jnp.sqrt(segsum(kk * kk)), 1e-12)
    kmod = k * (1.0 + (a - 1.0) * ka_ref[...])

    cums = _mm_split_lhs(cum_ref[...], ld)
    cum = cums[:R]
    tot = cums[R:]
    inv = jnp.exp(-cum)
    tail = jnp.exp(tot - cum)
    At = -kkn * jnp.exp(cum - ld)
    Rt = r * jnp.exp(cum)
    kka = kkn * a
    Bt = kka * inv
    Kt = kmod * inv
    Bend = kka * tail
    Kend = kmod * tail
    pc = jnp.exp(tot)

    lane = lax.broadcasted_iota(jnp.int32, (C, LANES), 1)
    h0 = lane < RWKV_HEAD_DIM
    split = lambda t: jnp.concatenate([jnp.where(h0, t, 0.0), jnp.where(h0, 0.0, t)], axis=0)
    grow = lax.broadcasted_iota(jnp.int32, (C, 4 * C), 0)
    gcol = lax.broadcasted_iota(jnp.int32, (C, 4 * C), 1) & (C - 1)
    r2 = lax.broadcasted_iota(jnp.int32, (2 * C, 2 * C), 0)
    c2 = lax.broadcasted_iota(jnp.int32, (2 * C, 2 * C), 1)
    eye = (r2 == c2).astype(F32)
    blk_shift = int(math.log2(RWKV_INV_BLOCK))
    same_blk = (r2 >> blk_shift) == (c2 >> blk_shift)
    same_head = (r2 < C) == (c2 < C)
    same_head2 = jnp.concatenate([same_head, same_head], axis=0)
    zeros_c = jnp.zeros((C, LANES), F32)
    zeros_2c = jnp.zeros((2 * C, LANES), F32)

    units = [(ci, p) for ci in range(nchunks) for p in range(npairs)]
    blk = lambda t, u: t[u[0] * C:(u[0] + 1) * C, u[1] * LANES:(u[1] + 1) * LANES]
    each = lambda f, *ls: [f(*xs) for xs in zip(*ls)]

    lhs = [jnp.concatenate([blk(At, u), blk(Rt, u)], axis=0) for u in units]
    rhs = [jnp.concatenate([split(blk(Bt, u)), split(blk(Kt, u))], axis=0) for u in units]
    G = each(_mm_nt, lhs, rhs)
    a_row = [jnp.where(gcol < grow, t[:C], 0.0) for t in G]
    m_row = [jnp.where(gcol <= grow, t[C:], 0.0) for t in G]
    a_bd = [split(t[:, :2 * C]) for t in a_row]

    a_d = [jnp.where(same_blk, t, 0.0) for t in a_bd]
    a_off = each(lambda x, y: x - y, a_bd, a_d)
    dinv = [eye + t for t in a_d]
    pw = a_d
    for _ in range(blk_shift - 1):
        pw = each(_mm, pw, pw)
        dinv = each(lambda x, y: x + _mm(x, y), dinv, pw)
    n1 = each(_mm, dinv, a_off)
    n2 = each(_mm, n1, n1)
    n3 = each(_mm, n1, n2)
    tinv = each(lambda x1, x2, x3, dv: _mm(eye + x1 + x2 + x3, dv), n1, n2, n3, dinv)

    vp = [blk(v, u) for u in units]
    v_st = [split(t) for t in vp]
    rhs0 = each(lambda ar, vs: _mm(ar[:, 2 * C:], vs), a_row, v_st)
    wu = each(lambda t, l, r0: _mm(t, jnp.concatenate([split(l[:C]), split(r0)], axis=1)),
              tinv, lhs, rhs0)
    wu_lp = [t[:C] + t[C:] for t in wu]
    mn = each(lambda w_, v_, u: _mm_tn(
        jnp.concatenate([w_, jnp.concatenate([zeros_c, v_], axis=1)], axis=0),
        jnp.concatenate([blk(Bend, u), blk(Kend, u)], axis=0)), wu_lp, vp, units)
    mn = [jnp.where(same_head2, t, 0.0) for t in mn]
    qy = each(lambda m_, w_, vs: _mm(m_, jnp.concatenate(
        [w_, jnp.concatenate([zeros_2c, vs], axis=1)], axis=0)), m_row, wu, v_st)

    ys = [[None] * npairs for _ in range(nchunks)]
    states = [s_ref[p] for p in range(npairs)]
    for i, (ci, p) in enumerate(units):
        s_old = states[p]
        q = lhs[i][C:] + qy[i][:, :LANES]
        ys[ci][p] = _mm_nt(q, s_old) + qy[i][:, LANES:]
        states[p] = s_old * blk(pc, (ci, p))[0:1, :] + _mm(s_old, mn[i][:LANES]) + mn[i][LANES:]
    for p in range(npairs):
        s_ref[p] = states[p]

    y = jnp.concatenate([jnp.concatenate(t, axis=1) for t in ys], axis=0)
    inv_n = 1.0 / RWKV_HEAD_DIM
    mean = segsum(y) * inv_n
    d = y - mean
    var = segsum(d * d) * inv_n
    yn = d * lax.rsqrt(var + GN_EPS) * lng_ref[...] + lnb_ref[...]
    bonus = segsum(r * kmod * rk_ref[...]) * v
    o_ref[...] = ((yn + bonus) * g).astype(o_ref.dtype)


def _rwkv_cum_matrix(rows):
    t = jnp.arange(rows)
    same = (t[:, None] // RWKV_CHUNK) == (t[None, :] // RWKV_CHUNK)
    return jnp.concatenate([same & (t[None, :] <= t[:, None]), same], axis=0).astype(BF16)


def _rwkv(crw3, mu, wl, w0, a0, k_k, k_a, r_k, ln_g, ln_b, ones_bd, cum_mat):
    b, s, _ = crw3.shape
    R = cum_mat.shape[1]
    W = RWKV_WIDTH
    vec = lambda n: pl.BlockSpec((1, n), lambda i, j: (0, 0))
    return pl.pallas_call(
        _rwkv_kernel,
        grid=(b, s // R),
        in_specs=[pl.BlockSpec((None, R, RWKV_COLS), lambda i, j: (i, j, 0)),
                  vec(RWKV_COLS),
                  pl.BlockSpec((LORA_COLS, 3 * W), lambda i, j: (0, 0)),
                  vec(W), vec(W), vec(W), vec(W), vec(W), vec(W), vec(W),
                  pl.BlockSpec((LANES, LANES), lambda i, j: (0, 0)),
                  pl.BlockSpec((2 * R, R), lambda i, j: (0, 0))],
        out_specs=pl.BlockSpec((None, R, W), lambda i, j: (i, j, 0)),
        out_shape=jax.ShapeDtypeStruct((b, s, W), BF16),
        scratch_shapes=[pltpu.VMEM((1, RWKV_COLS), F32),
                        pltpu.VMEM((W // LANES, LANES, LANES), F32)],
        compiler_params=pltpu.CompilerParams(dimension_semantics=("arbitrary", "arbitrary"),
                                             vmem_limit_bytes=VMEM_LIMIT),
        name="rwkv",
    )(crw3, mu, wl, w0, a0, k_k, k_a, r_k, ln_g, ln_b, ones_bd, cum_mat)


def _s5_mats(lam_re, lam_im, log_dt, b_re, b_im, c_re, c_im, d_skip):
    L = S5_CHUNK
    G, N = lam_re.shape
    ch = b_re.shape[-1]
    dt = jnp.exp(log_dt)[:, None]
    lr, li = lam_re, lam_im
    mag = jnp.exp(lr * dt)
    lb_re, lb_im = mag * jnp.cos(li * dt), mag * jnp.sin(li * dt)
    den = lr * lr + li * li
    nr, ni = lb_re - 1.0, lb_im
    coef_re = (nr * lr + ni * li) / den
    coef_im = (ni * lr - nr * li) / den
    bb_re = coef_re[..., None] * b_re - coef_im[..., None] * b_im
    bb_im = coef_re[..., None] * b_im + coef_im[..., None] * b_re
    prs, pis = [jnp.ones_like(lb_re)], [jnp.zeros_like(lb_im)]
    for _ in range(L):
        pr_, pi_ = prs[-1], pis[-1]
        prs.append(pr_ * lb_re - pi_ * lb_im)
        pis.append(pr_ * lb_im + pi_ * lb_re)
    pr = jnp.stack(prs)
    pi = jnp.stack(pis)
    hp = lax.Precision.HIGHEST

    def lam_bb(qr, qi):
        return (qr[..., None] * bb_re[None] - qi[..., None] * bb_im[None],
                qr[..., None] * bb_im[None] + qi[..., None] * bb_re[None])

    def repeat_lanes(a, reps):
        k = a.shape[-1]
        rep = (jnp.arange(k)[:, None] == (jnp.arange(k * reps)[None, :] % k)).astype(F32)
        return jnp.dot(a, rep, precision=hp)

    def group_expand(a_rep, row_group, groups_per_tile, per_group):
        rows, mid, _ = a_rep.shape
        tiles = G // groups_per_tile
        lane_group = (jnp.arange(tiles)[:, None] * groups_per_tile
                      + jnp.arange(LANES)[None, :] // per_group)
        mask = row_group[:, None, None, None] == lane_group[None, None]
        out = jnp.where(mask, a_rep[:, :, None, :], 0.0).astype(BF16)
        return out.reshape(rows, mid * tiles * LANES)

    in_rows = L * G * ch
    in_group = (jnp.arange(in_rows) // ch) % G
    st_rows = 2 * G * N
    st_group = (jnp.arange(st_rows) // N) % G

    wre, wim = lam_bb(pr[:L][::-1], pi[:L][::-1])
    w_parts = []
    for part in (wre, wim):
        a = jnp.swapaxes(part, 2, 3).reshape(in_rows, N)
        a = repeat_lanes(a, LANES // N)[:, None, :]
        w_parts.append(group_expand(a, in_group, LANES // N, N))
    lre, lim = lam_bb(pr[:L], pi[:L])
    kern = (jnp.einsum('gon,lgni->lgio', c_re, lre, precision=hp)
            - jnp.einsum('gon,lgni->lgio', c_im, lim, precision=hp))
    kern = kern.at[0].add(d_skip[:, :, None] * jnp.eye(ch, dtype=F32)[None])
    lag = jnp.arange(L)[None, :] - jnp.arange(L)[:, None]
    kst = jnp.where((lag >= 0)[:, :, None, None, None], kern[jnp.maximum(lag, 0)], 0.0)
    kf = jnp.transpose(kst, (0, 2, 3, 1, 4)).reshape(in_rows * L, ch)
    kf = repeat_lanes(kf, LANES // ch).reshape(in_rows, L, LANES)
    toep = group_expand(kf, in_group, LANES // ch, ch)
    qr, qi = pr[1:L + 1][:, :, None, :], pi[1:L + 1][:, :, None, :]
    o_re = c_re[None] * qr - c_im[None] * qi
    o_im = -c_re[None] * qi - c_im[None] * qr
    of = jnp.transpose(jnp.stack([o_re, o_im]), (0, 2, 4, 1, 3))
    of = repeat_lanes(of.reshape(st_rows * L, ch), LANES // ch).reshape(st_rows, L, LANES)
    w_out_flat = group_expand(of, st_group, LANES // ch, ch)
    w_a = jnp.concatenate(w_parts + [toep], axis=1)
    plr = pr[L].reshape(1, G * N)
    pli = pi[L].reshape(1, G * N)
    return w_a, w_out_flat, plr, pli


def _s5_kernel(u_ref, wa_ref, wo_ref, plr_ref, pli_ref, o_ref, wloc_ref, zprev_ref):
    nch, L, w = u_ref.shape
    half = S5_ZW // 2

    r = _mm(u_ref[:, 0, :], wa_ref[0:w, :])
    for j in range(1, L):
        r = r + _mm(u_ref[:, j, :], wa_ref[j * w:(j + 1) * w, :])
    wloc_ref[...] = r[:, :S5_ZW]
    y_lag = r[:, S5_ZW:]

    plr = plr_ref[...]
    pli = pli_ref[...]

    def step(ci, z):
        zprev_ref[pl.ds(ci, 1), :] = z
        zr, zi = z[:, :half], z[:, half:]
        nz = jnp.concatenate([plr * zr - pli * zi, plr * zi + pli * zr], axis=1)
        return nz + wloc_ref[pl.ds(ci, 1), :]

    lax.fori_loop(0, nch, step, jnp.zeros((1, S5_ZW), F32))
    y = y_lag + _mm(zprev_ref[...], wo_ref[...])
    for j in range(L):
        o_ref[:, j, :] = y[:, j * w:(j + 1) * w]


def _s5(u4, w_a, w_o, plr, pli):
    b, nch, L, w = u4.shape
    const = lambda a: pl.BlockSpec(a.shape, lambda i: (0,) * a.ndim, pipeline_mode=pl.Buffered(1))
    blk = pl.BlockSpec((None, nch, L, w), lambda i: (i, 0, 0, 0))
    return pl.pallas_call(
        _s5_kernel,
        grid=(b,),
        in_specs=[blk, const(w_a), const(w_o), const(plr), const(pli)],
        out_specs=blk,
        out_shape=jax.ShapeDtypeStruct(u4.shape, F32),
        scratch_shapes=[pltpu.VMEM((nch, S5_ZW), F32),
                        pltpu.VMEM((nch, S5_ZW), F32)],
        compiler_params=pltpu.CompilerParams(dimension_semantics=("arbitrary",),
                                             vmem_limit_bytes=VMEM_LIMIT),
        name="s5",
    )(u4, w_a, w_o, plr, pli)


def _merge_kernel(x_ref, ya_ref, yb_ref, gates_ref, gluw_ref, glub_ref, wba_ref, wbb_ref, wout_ref,
                  fng_ref, rw_ref, rb_ref, tri_ref, upper_ref, x1_ref, t_ref, rt_ref, pos_ref,
                  te_ref, fill_ref, nused_ref, cnt_ref, cur_ref):
    d = x_ref.shape[1]
    tm = x_ref.shape[0]

    @pl.when(pl.program_id(0) == 0)
    def _():
        cnt_ref[...] = jnp.zeros_like(cnt_ref)
        cur_ref[...] = jnp.zeros_like(cur_ref)
        te_ref[...] = jnp.zeros_like(te_ref)
        fill_ref[...] = jnp.zeros_like(fill_ref)
        nused_ref[...] = jnp.zeros_like(nused_ref)

    y_a = jnp.dot(ya_ref[...], wba_ref[...], preferred_element_type=F32)
    ys = yb_ref[...]
    z = 0.5 * ys * (1.0 + jnp.tanh(math.sqrt(2.0 / math.pi) * (ys + 0.044715 * (ys * ys * ys))))
    z = z * _sigmoid(_mm(z, gluw_ref[...]) + glub_ref[...])
    y_b = _mm(z, wbb_ref[...])
    gates = gates_ref[...].astype(F32)
    merged = _sigmoid(gates[:, :d]) * y_a + _sigmoid(gates[:, d:]) * y_b
    x1 = x_ref[...] + _mm(merged, wout_ref[...])
    x1_ref[...] = x1
    t = _rms_norm(x1, fng_ref[...])
    t_hi = t.astype(BF16)
    t_ref[...] = t

    t_lo = (t - t_hi.astype(F32)).astype(BF16)
    hh_hl = jnp.dot(t_hi, rw_ref[...], preferred_element_type=F32)
    lh = jnp.dot(t_lo, rw_ref[:, :ROUTER_LANES], preferred_element_type=F32)
    logits = hh_hl[:, :ROUTER_LANES] + hh_hl[:, ROUTER_LANES:] + lh + rb_ref[...]
    lane = lax.broadcasted_iota(jnp.int32, logits.shape, 1)
    neg = -jnp.inf
    big = jnp.int32(1 << 20)
    is_g = (lane >= N_EXPERTS) & (lane < N_EXPERTS + N_GROUPS)
    gl = jnp.where(is_g, logits, neg)
    gmax = jnp.max(gl, axis=-1, keepdims=True)
    g_p = 1.0 / jnp.sum(jnp.exp(gl - gmax), axis=-1, keepdims=True)
    g_idx = jnp.min(jnp.where(gl == gmax, lane - N_EXPERTS, big), axis=-1, keepdims=True)
    el = jnp.where((lane < N_EXPERTS) & ((lane >> int(math.log2(EXPERTS_PER_GROUP))) == g_idx), logits, neg)
    t1 = jnp.max(el, axis=-1, keepdims=True)
    i1 = jnp.min(jnp.where(el == t1, lane, big), axis=-1, keepdims=True)
    el2 = jnp.where(lane == i1, neg, el)
    t2 = jnp.max(el2, axis=-1, keepdims=True)
    i2 = jnp.min(jnp.where(el2 == t2, lane, big), axis=-1, keepdims=True)
    e21 = jnp.exp(t2 - t1)
    w1 = g_p / (1.0 + e21)
    w2 = g_p * e21 / (1.0 + e21)
    rt_ref[...] = jnp.where(lane == 0, w1, jnp.where(lane == 1, w2, 0.0))

    sh = int(math.log2(MOE_TILE))
    oh1 = lane == i1
    oh2 = lane == i2
    ind = jnp.where(oh1 | oh2, 1.0, 0.0)
    lrank = jnp.dot(tri_ref[...], ind.astype(BF16), preferred_element_type=F32).astype(jnp.int32)
    n_new = lrank[tm - 1:tm, :] + ind[tm - 1:tm, :].astype(jnp.int32)
    cnt = cnt_ref[...]
    cur = cur_ref[...]
    nfree = nused_ref[...]
    tiles_before = (cnt + (MOE_TILE - 1)) >> sh
    newf = ((cnt + n_new + (MOE_TILE - 1)) >> sh) - tiles_before
    newf8 = jnp.broadcast_to(newf.astype(BF16), (8, newf.shape[1]))
    pre = jnp.dot(newf8, upper_ref[...], preferred_element_type=F32)[0:1, :].astype(jnp.int32)
    new_tile = nfree + pre
    grank = cnt + lrank
    ptile = jnp.where((grank >> sh) < tiles_before, cur, new_tile)
    posfull = ((ptile << sh) + (grank & (MOE_TILE - 1))).astype(F32)
    pos1 = jnp.sum(jnp.where(oh1, posfull, 0.0), axis=-1, keepdims=True)
    pos2 = jnp.sum(jnp.where(oh2, posfull, 0.0), axis=-1, keepdims=True)
    tr = lax.broadcasted_iota(jnp.int32, (tm, tm), 0)
    tc = lax.broadcasted_iota(jnp.int32, (tm, tm), 1)
    as_row = lambda v: jnp.sum(jnp.where(tr == tc, jnp.broadcast_to(v, (tm, tm)), 0.0),
                               axis=0, keepdims=True)
    sub = lax.broadcasted_iota(jnp.int32, pos_ref.shape, 0)
    pos_ref[...] = jnp.where(sub == 0, as_row(pos1),
                             jnp.where(sub == 1, as_row(pos2), 0.0)).astype(jnp.int32)

    el_r = lax.broadcasted_iota(jnp.int32, (ROUTER_LANES, ROUTER_LANES), 0)
    el_c = lax.broadcasted_iota(jnp.int32, (ROUTER_LANES, ROUTER_LANES), 1)
    as_col = lambda v: jnp.sum(jnp.where(el_r == el_c, jnp.broadcast_to(v, el_r.shape), 0.0),
                               axis=1, keepdims=True)
    tile_col = as_col(jnp.where(newf > 0, new_tile, -1).astype(F32)).astype(jnp.int32)
    tlane = lax.broadcasted_iota(jnp.int32, (ROUTER_LANES, te_ref.shape[1]), 1)
    erow = lax.broadcasted_iota(jnp.int32, (ROUTER_LANES, te_ref.shape[1]), 0).astype(F32)
    te_ref[...] += jnp.sum(jnp.where(tlane == tile_col, erow, 0.0), axis=0,
                           keepdims=True).astype(jnp.int32)
    to_cur = jnp.minimum(n_new, (tiles_before << sh) - cnt)
    cur_col = as_col(cur.astype(F32)).astype(jnp.int32)
    added = (jnp.where(tlane == cur_col, as_col(to_cur.astype(F32)), 0.0)
             + jnp.where(tlane == tile_col, as_col((n_new - to_cur).astype(F32)), 0.0))
    fill_ref[...] += jnp.sum(added, axis=0, keepdims=True).astype(jnp.int32)
    cnt_ref[...] = cnt + n_new
    cur_ref[...] = jnp.where(newf > 0, new_tile, cur)
    nused_ref[...] = nfree + jnp.sum(newf.astype(F32), axis=-1, keepdims=True).astype(jnp.int32)


def _moe_tiles(t):
    return (2 * t) // MOE_TILE + N_EXPERTS


def _merge(x2, ya, yb, gates, gluw, glub, wba, wbb, wout, fng, rw, rb):
    t, d = x2.shape
    tm = MERGE_TILE
    assert t % tm == 0 and tm <= MOE_TILE
    te_lanes = -(-_moe_tiles(t) // LANES) * LANES
    rr = jnp.arange(tm)
    tri = (rr[None, :] < rr[:, None]).astype(BF16)
    ll = jnp.arange(ROUTER_LANES)
    upper = (ll[:, None] < ll[None, :]).astype(BF16)
    full = lambda a: pl.BlockSpec(a.shape, lambda i: (0,) * a.ndim)
    rowblk = lambda n: pl.BlockSpec((tm, n), lambda i: (i, 0))
    fixed = lambda n: pl.BlockSpec((1, n), lambda i: (0, 0))
    return pl.pallas_call(
        _merge_kernel,
        grid=(t // tm,),
        in_specs=[rowblk(d), rowblk(ya.shape[1]), rowblk(yb.shape[1]), rowblk(gates.shape[1]),
                  full(gluw), full(glub), full(wba), full(wbb), full(wout), full(fng), full(rw),
                  full(rb), full(tri), full(upper)],
        out_specs=[rowblk(d), rowblk(d), rowblk(ROUTER_LANES),
                   pl.BlockSpec((None, 8, tm), lambda i: (i, 0, 0)),
                   fixed(te_lanes), fixed(te_lanes), fixed(ROUTER_LANES)],
        out_shape=[jax.ShapeDtypeStruct((t, d), F32),
                   jax.ShapeDtypeStruct((t, d), F32),
                   jax.ShapeDtypeStruct((t, ROUTER_LANES), F32),
                   jax.ShapeDtypeStruct((t // tm, 8, tm), jnp.int32),
                   jax.ShapeDtypeStruct((1, te_lanes), jnp.int32),
                   jax.ShapeDtypeStruct((1, te_lanes), jnp.int32),
                   jax.ShapeDtypeStruct((1, ROUTER_LANES), jnp.int32)],
        scratch_shapes=[pltpu.VMEM((1, ROUTER_LANES), jnp.int32),
                        pltpu.VMEM((1, ROUTER_LANES), jnp.int32)],
        compiler_params=pltpu.CompilerParams(dimension_semantics=("arbitrary",),
                                             vmem_limit_bytes=VMEM_LIMIT),
        name="merge",
    )(x2, ya, yb, gates, gluw, glub, wba, wbb, wout, fng, rw, rb, tri, upper)


def _dispatch_kernel(pos1_ref, pos2_ref, fill_ref, t_ref, xs_ref, zblk, sem, zsem):
    tm = t_ref.shape[0]
    base = pl.program_id(0) * tm

    @pl.when(pl.program_id(0) == pl.num_programs(0) - 1)
    def _():
        zblk[...] = jnp.zeros_like(zblk)

        def over_tails(op):
            def tile(j, carry):
                fill = fill_ref[j]
                head = (-fill) & (SUBLANES - 1)
                for r in range(SUBLANES - 1):
                    @pl.when(r < head)
                    def _(r=r):
                        op(pltpu.make_async_copy(zblk.at[pl.ds(0, 1), :],
                                                 xs_ref.at[pl.ds(j * MOE_TILE + fill + r, 1), :], zsem))
                start = fill + head
                todo = MOE_TILE - start
                size = MOE_TILE
                while size >= SUBLANES:
                    @pl.when((todo & size) != 0)
                    def _(start=start, size=size):
                        off = pl.multiple_of(j * MOE_TILE + start, SUBLANES)
                        op(pltpu.make_async_copy(zblk.at[pl.ds(0, size), :],
                                                 xs_ref.at[pl.ds(off, size), :], zsem))
                    start = start + (todo & size)
                    size //= 2
                return carry
            lax.fori_loop(0, fill_ref.shape[0], tile, 0)

        over_tails(lambda c: c.start())
        over_tails(lambda c: c.wait())

    def start(i, carry):
        for pos_ref in (pos1_ref, pos2_ref):
            pltpu.make_async_copy(t_ref.at[pl.ds(i, 1), :],
                                  xs_ref.at[pl.ds(pos_ref[base + i], 1), :], sem).start()
        return carry

    lax.fori_loop(0, tm, start, 0, unroll=DMA_UNROLL)
    for _ in range(2):
        pltpu.make_async_copy(t_ref, xs_ref.at[pl.ds(0, tm), :], sem).wait()


def _dispatch(pos1, pos2, tile_fill, t_ffn, tm):
    t, d = t_ffn.shape
    n_rows = tile_fill.shape[0] * MOE_TILE
    return pl.pallas_call(
        _dispatch_kernel,
        grid_spec=pltpu.PrefetchScalarGridSpec(
            num_scalar_prefetch=3,
            grid=(t // tm,),
            in_specs=[pl.BlockSpec((tm, d), lambda i, p1, p2, fl: (i, 0))],
            out_specs=pl.BlockSpec(memory_space=pl.ANY),
            scratch_shapes=[pltpu.VMEM((MOE_TILE, d), t_ffn.dtype),
                            pltpu.SemaphoreType.DMA(()),
                            pltpu.SemaphoreType.DMA(())]),
        out_shape=jax.ShapeDtypeStruct((n_rows, d), t_ffn.dtype),
        compiler_params=pltpu.CompilerParams(dimension_semantics=("arbitrary",),
                                             vmem_limit_bytes=VMEM_LIMIT),
        name="dispatch",
    )(pos1, pos2, tile_fill, t_ffn)


def _experts_kernel(te_ref, nused_ref, xs_ref, wg_ref, wu_ref, wd_ref, y_ref):
    j = pl.program_id(0)

    @pl.when(j < nused_ref[0])
    def _():
        x = xs_ref[...].astype(BF16)
        hg = _mm(x, wg_ref[...])
        hid = hg * _sigmoid(hg) * _mm(x, wu_ref[...])
        y_ref[...] = _mm(hid, wd_ref[...])

    @pl.when(j >= nused_ref[0])
    def _():
        y_ref[...] = jnp.zeros_like(y_ref)


def _experts(tile_expert, n_used, xs, wg, wu, wd):
    n_rows, hw = xs.shape
    ne, d, de = wg.shape
    return pl.pallas_call(
        _experts_kernel,
        grid_spec=pltpu.PrefetchScalarGridSpec(
            num_scalar_prefetch=2,
            grid=(n_rows // MOE_TILE,),
            in_specs=[pl.BlockSpec((MOE_TILE, hw), lambda j, te, nu: (j, 0)),
                      pl.BlockSpec((None, d, de), lambda j, te, nu: (te[j], 0, 0)),
                      pl.BlockSpec((None, d, de), lambda j, te, nu: (te[j], 0, 0)),
                      pl.BlockSpec((None, de, d), lambda j, te, nu: (te[j], 0, 0))],
            out_specs=pl.BlockSpec((MOE_TILE, hw), lambda j, te, nu: (j, 0))),
        out_shape=jax.ShapeDtypeStruct((n_rows, hw), F32),
        compiler_params=pltpu.CompilerParams(dimension_semantics=("arbitrary",),
                                             vmem_limit_bytes=VMEM_LIMIT),
        name="experts",
    )(tile_expert, n_used, xs, wg, wu, wd)


def _ple_kernel(pos1_ref, pos2_ref, x_ref, rt_ref, p_ref, png_ref, wg_ref, wp_ref, fng_ref, y_ref,
                o_ref, ybuf, sem):
    i = pl.program_id(0)
    tm = x_ref.shape[0]

    def gather(tile, unroll):
        slot = tile % 2

        def body(r, carry):
            for k, pos_ref in enumerate((pos1_ref, pos2_ref)):
                pltpu.make_async_copy(y_ref.at[pl.ds(pos_ref[tile * tm + r], 1), :],
                                      ybuf.at[slot, k, pl.ds(r, 1), :], sem.at[slot]).start()
            return carry
        lax.fori_loop(0, tm, body, 0, unroll=unroll)

    @pl.when(i == 0)
    def _():
        gather(i, DMA_UNROLL)

    def step(prefetch_next):
        slot = i % 2
        for k in range(2):
            pltpu.make_async_copy(y_ref.at[pl.ds(0, tm), :], ybuf.at[slot, k], sem.at[slot]).wait()
        if prefetch_next:
            gather(i + 1, True)
        rt = rt_ref[...]
        x2 = x_ref[...] + rt[:, 0:1] * ybuf[slot, 0] + rt[:, 1:2] * ybuf[slot, 1]
        hp = _rms_norm(x2, png_ref[...])
        gate = _sigmoid(_mm(hp, wg_ref[...]))
        x3 = x2 + gate * _mm(p_ref[...], wp_ref[...])
        o_ref[...] = _rms_norm(x3, fng_ref[...])

    last = pl.num_programs(0) - 1
    pl.when(i < last)(lambda: step(True))
    pl.when(i == last)(lambda: step(False))


def _ple(pos1, pos2, x1, rt, p2, png, wg, wp, fng, y_pack, tm):
    t, d = x1.shape
    hw = y_pack.shape[1]
    full = lambda a: pl.BlockSpec(a.shape, lambda i, p1, p2_: (0,) * a.ndim)
    rowblk = lambda n: pl.BlockSpec((tm, n), lambda i, p1, p2_: (i, 0))
    return pl.pallas_call(
        _ple_kernel,
        grid_spec=pltpu.PrefetchScalarGridSpec(
            num_scalar_prefetch=2,
            grid=(t // tm,),
            in_specs=[rowblk(d), rowblk(rt.shape[1]), rowblk(p2.shape[1]),
                      full(png), full(wg), full(wp), full(fng),
                      pl.BlockSpec(memory_space=pl.ANY)],
            out_specs=rowblk(d),
            scratch_shapes=[pltpu.VMEM((2, 2, tm, hw), F32),
                            pltpu.SemaphoreType.DMA((2,))]),
        out_shape=jax.ShapeDtypeStruct((t, d), F32),
        compiler_params=pltpu.CompilerParams(dimension_semantics=("arbitrary",),
                                             vmem_limit_bytes=VMEM_LIMIT),
        name="ple",
    )(pos1, pos2, x1, rt, p2, png, wg, wp, fng, y_pack)


def _row_tile(t, want):
    tm = min(want, t)
    while t % tm:
        tm //= 2
    return tm


def _layer(x, p, mix_norm, w_in, mu_shift, rk_w0, rk_w_up, rk_a0, rk_a_up, rk_g_up,
           rk_k_k, rk_k_a, rk_r_k, rk_ln_g, rk_ln_b, s5_lam_re, s5_lam_im, s5_log_dt,
           s5_b_re, s5_b_im, s5_c_re, s5_c_im, s5_d, s5_glu_w, s5_glu_b,
           w_branch_a, w_branch_b, w_out, ffn_norm, router_group_w, router_group_b,
           router_expert_w, router_expert_b, exp_w_gate, exp_w_up, exp_w_down,
           ple_norm, ple_gate_w, ple_proj):
    b, s, d = x.shape
    t = b * s
    W = RWKV_WIDTH
    row = lambda a: a.reshape(1, -1).astype(F32)
    x2 = x.reshape(t, d)

    crw, us5, gates = _in_proj(x2, row(mix_norm), w_in.astype(BF16), _row_tile(t, 256))

    wl = jnp.zeros((LORA_COLS, 3 * W), F32)
    wl = wl.at[:DECAY_LORA, :W].set(rk_w_up)
    wl = wl.at[DECAY_LORA:DECAY_LORA + AAA_LORA, W:2 * W].set(rk_a_up)
    wl = wl.at[DECAY_LORA + AAA_LORA:, 2 * W:].set(rk_g_up)
    hid = jnp.arange(LANES) // RWKV_HEAD_DIM
    ones_bd = (hid[:, None] == hid[None, :]).astype(BF16)
    ya = _rwkv(crw.reshape(b, s, RWKV_COLS), row(mu_shift), wl.astype(BF16), row(rk_w0), row(rk_a0),
               row(rk_k_k), row(rk_k_a), row(rk_r_k), row(rk_ln_g), row(rk_ln_b), ones_bd,
               _rwkv_cum_matrix(_row_tile(s, RWKV_BLOCK)))

    s5_wa, s5_wo, plr, pli = _s5_mats(s5_lam_re, s5_lam_im, s5_log_dt, s5_b_re, s5_b_im,
                                      s5_c_re, s5_c_im, s5_d)
    yb = _s5(us5.reshape(b, s // S5_CHUNK, S5_CHUNK, S5_WIDTH), s5_wa, s5_wo, plr, pli)

    rw = jnp.zeros((d, ROUTER_LANES), F32)
    rw = rw.at[:, :N_EXPERTS].set(router_expert_w).at[:, N_EXPERTS:N_EXPERTS + N_GROUPS].set(router_group_w)
    rb = jnp.zeros((1, ROUTER_LANES), F32)
    rb = rb.at[0, :N_EXPERTS].set(router_expert_b).at[0, N_EXPERTS:N_EXPERTS + N_GROUPS].set(router_group_b)
    rw_hi = rw.astype(BF16)
    rw = jnp.concatenate([rw_hi, (rw - rw_hi.astype(F32)).astype(BF16)], axis=1)
    x1, t_ffn, rt, pos, tile_expert, tile_fill, n_used = _merge(
        x2, ya.reshape(t, W), yb.reshape(t, S5_WIDTH), gates, s5_glu_w.astype(BF16), row(s5_glu_b),
        w_branch_a.astype(BF16), w_branch_b.astype(BF16), w_out.astype(BF16), row(ffn_norm), rw, rb)

    n_tiles = _moe_tiles(t)
    pos1, pos2 = pos[:, 0, :].reshape(t), pos[:, 1, :].reshape(t)
    xs = _dispatch(pos1, pos2, tile_fill[0, :n_tiles], t_ffn, _row_tile(t, 512))
    y_pack = _experts(tile_expert[0, :n_tiles], n_used[0, :1], xs, exp_w_gate, exp_w_up, exp_w_down)
    return (pos1, pos2, x1, rt, p.reshape(t, -1), row(ple_norm), ple_gate_w.astype(BF16),
            ple_proj.astype(BF16), y_pack)


def kernel(x, p, mix_norm, w_in, mu_shift, rk_w0, rk_w_up, rk_a0, rk_a_up, rk_g_up, rk_k_k, rk_k_a,
           rk_r_k, rk_ln_g, rk_ln_b, s5_lam_re, s5_lam_im, s5_log_dt, s5_b_re, s5_b_im, s5_c_re,
           s5_c_im, s5_d, s5_glu_w, s5_glu_b, w_branch_a, w_branch_b, w_out, ffn_norm,
           router_group_w, router_group_b, router_expert_w, router_expert_b, exp_w_gate, exp_w_up,
           exp_w_down, ple_norm, ple_gate_w, ple_proj, final_norm):
    b, s, d = x.shape
    depth = w_in.shape[0]
    assert depth == 1, "the final norm is fused into the last layer's PLE kernel"
    i = 0
    pos1, pos2, x1, rt, p2, png, wpg, wpp, y_pack = _layer(
        x, p[i], mix_norm[i], w_in[i], mu_shift[i], rk_w0[i], rk_w_up[i], rk_a0[i], rk_a_up[i],
        rk_g_up[i], rk_k_k[i], rk_k_a[i], rk_r_k[i], rk_ln_g[i], rk_ln_b[i], s5_lam_re[i],
        s5_lam_im[i], s5_log_dt[i], s5_b_re[i], s5_b_im[i], s5_c_re[i], s5_c_im[i], s5_d[i],
        s5_glu_w[i], s5_glu_b[i], w_branch_a[i], w_branch_b[i], w_out[i], ffn_norm[i],
        router_group_w[i], router_group_b[i], router_expert_w[i], router_expert_b[i],
        exp_w_gate[i], exp_w_up[i], exp_w_down[i], ple_norm[i], ple_gate_w[i], ple_proj[i])
    out = _ple(pos1, pos2, x1, rt, p2, png, wpg, wpp, final_norm.reshape(1, -1).astype(F32), y_pack,
               _row_tile(b * s, MERGE_TILE))
    return out.reshape(b, s, d)
```

```python
import functools
import math

import jax
import jax.numpy as jnp
from jax import lax
from jax.experimental import pallas as pl
from jax.experimental.pallas import tpu as pltpu

F32 = jnp.float32
BF16 = jnp.bfloat16

NORM_EPS = 1e-6
GN_EPS = 64e-5

RWKV_HEADS = 8
RWKV_HEAD_DIM = 64
RWKV_WIDTH = RWKV_HEADS * RWKV_HEAD_DIM
DECAY_LORA = 64
AAA_LORA = 64
GATE_LORA = 128
LORA_COLS = DECAY_LORA + AAA_LORA + GATE_LORA
RWKV_COLS = 3 * RWKV_WIDTH + LORA_COLS
S5_GROUPS = 16
S5_GROUP_CH = 16
S5_WIDTH = S5_GROUPS * S5_GROUP_CH
S5_STATE = 64
S5_ZW = 2 * S5_GROUPS * S5_STATE
N_GROUPS = 4
EXPERTS_PER_GROUP = 8
N_EXPERTS = N_GROUPS * EXPERTS_PER_GROUP

LANES = 128
SUBLANES = 8
RWKV_CHUNK = 64
RWKV_INV_BLOCK = 16
RWKV_BLOCK = 256
S5_CHUNK = 8
ROUTER_LANES = 128
MOE_TILE = 512
MERGE_TILE = 256
DMA_UNROLL = 8
VMEM_LIMIT = 56 * 1024 * 1024


def _mm(a, b):
    return jnp.dot(a.astype(BF16), b.astype(BF16), preferred_element_type=F32)


def _mm_nt(a, b):
    return lax.dot_general(a.astype(BF16), b.astype(BF16), (((1,), (1,)), ((), ())),
                           preferred_element_type=F32)


def _mm_tn(a, b):
    return lax.dot_general(a.astype(BF16), b.astype(BF16), (((0,), (0,)), ((), ())),
                           preferred_element_type=F32)


def _split2(x):
    hi = x.astype(BF16)
    return hi, (x - hi.astype(F32)).astype(BF16)


def _mm_split_lhs(a_bf16, x):
    hi, lo = _split2(x)
    d = lambda t: jnp.dot(a_bf16, t, preferred_element_type=F32)
    return d(hi) + d(lo)


def _mm_split_rhs(x, b_bf16):
    hi, lo = _split2(x)
    d = lambda t: jnp.dot(t, b_bf16, preferred_element_type=F32)
    return d(hi) + d(lo)


def _sigmoid(x):
    return 0.5 * jnp.tanh(0.5 * x) + 0.5


def _rms_norm(x, g):
    ms = jnp.mean(x * x, axis=-1, keepdims=True)
    return x * lax.rsqrt(ms + NORM_EPS) * g


def _in_proj_kernel(x_ref, g_ref, w_ref, crw_ref, us5_ref, gates_ref):
    h = _rms_norm(x_ref[...], g_ref[...]).astype(BF16)
    c0, c1 = RWKV_COLS, RWKV_COLS + S5_WIDTH
    crw_ref[...] = jnp.dot(h, w_ref[:, :c0], preferred_element_type=F32)
    us5_ref[...] = jnp.dot(h, w_ref[:, c0:c1], preferred_element_type=F32)
    gates_ref[...] = jnp.dot(h, w_ref[:, c1:], preferred_element_type=F32).astype(BF16)


def _in_proj(x2, g, w, tm):
    t, d = x2.shape
    n = w.shape[1]
    ng = n - RWKV_COLS - S5_WIDTH
    return pl.pallas_call(
        _in_proj_kernel,
        grid=(t // tm,),
        in_specs=[pl.BlockSpec((tm, d), lambda i: (i, 0)),
                  pl.BlockSpec((1, d), lambda i: (0, 0)),
                  pl.BlockSpec((d, n), lambda i: (0, 0))],
        out_specs=[pl.BlockSpec((tm, RWKV_COLS), lambda i: (i, 0)),
                   pl.BlockSpec((tm, S5_WIDTH), lambda i: (i, 0)),
                   pl.BlockSpec((tm, ng), lambda i: (i, 0))],
        out_shape=[jax.ShapeDtypeStruct((t, RWKV_COLS), F32),
                   jax.ShapeDtypeStruct((t, S5_WIDTH), F32),
                   jax.ShapeDtypeStruct((t, ng), BF16)],
        compiler_params=pltpu.CompilerParams(dimension_semantics=("arbitrary",),
                                             vmem_limit_bytes=VMEM_LIMIT),
        name="in_proj",
    )(x2, g, w)


def _rwkv_kernel(c_ref, mu_ref, wl_ref, w0_ref, a0_ref, kk_ref, ka_ref, rk_ref, lng_ref, lnb_ref,
                 ones_ref, cum_ref, o_ref, carry_ref, s_ref):
    C = RWKV_CHUNK
    W = RWKV_WIDTH
    npairs = W // LANES

    @pl.when(pl.program_id(1) == 0)
    def _():
        carry_ref[...] = jnp.zeros_like(carry_ref)
        s_ref[...] = jnp.zeros_like(s_ref)

    c = c_ref[...]
    R = c.shape[0]
    nchunks = R // C
    row = lax.broadcasted_iota(jnp.int32, (R, 1), 0)
    prev = jnp.where(row == 0, carry_ref[...], pltpu.roll(c, 1, 0))
    carry_ref[...] = c[R - 1:R, :]
    cs = c + (prev - c) * mu_ref[...]

    r = cs[:, 0:W]
    k = cs[:, W:2 * W]
    v = cs[:, 2 * W:3 * W]
    lin = cs[:, 3 * W:]
    llane = lax.broadcasted_iota(jnp.int32, lin.shape, 1)
    lact = jnp.where(llane < DECAY_LORA, jnp.tanh(lin),
                     jnp.where(llane < DECAY_LORA + AAA_LORA, lin, _sigmoid(lin)))
    lo = _mm(lact, wl_ref[...])
    zw = -(w0_ref[...] + lo[:, 0:W])
    softplus = jnp.maximum(zw, 0.0) + jnp.log(1.0 + jnp.exp(-jnp.abs(zw)))
    ld = -jnp.exp(-softplus - 0.5)
    a = _sigmoid(a0_ref[...] + lo[:, W:2 * W])
    g = lo[:, 2 * W:3 * W]

    ones_bd = ones_ref[...]
    segsum = lambda t: jnp.concatenate(
        [_mm_split_rhs(t[:, p * LANES:(p + 1) * LANES], ones_bd) for p in range(npairs)], axis=1)
    kk = k * kk_ref[...]
    kkn = kk / jnp.maximum(jnp.sqrt(segsum(kk * kk)), 1e-12)
    kmod = k * (1.0 + (a - 1.0) * ka_ref[...])

    cums = _mm_split_lhs(cum_ref[...], ld)
    cum = cums[:R]
    tot = cums[R:]
    inv = jnp.exp(-cum)
    tail = jnp.exp(tot - cum)
    At = -kkn * jnp.exp(cum - ld)
    Rt = r * jnp.exp(cum)
    kka = kkn * a
    Bt = kka * inv
    Kt = kmod * inv
    Bend = kka * tail
    Kend = kmod * tail
    pc = jnp.exp(tot)

    lane = lax.broadcasted_iota(jnp.int32, (C, LANES), 1)
    h0 = lane < RWKV_HEAD_DIM
    split = lambda t: jnp.concatenate([jnp.where(h0, t, 0.0), jnp.where(h0, 0.0, t)], axis=0)
    grow = lax.broadcasted_iota(jnp.int32, (C, 4 * C), 0)
    gcol = lax.broadcasted_iota(jnp.int32, (C, 4 * C), 1) & (C - 1)
    r2 = lax.broadcasted_iota(jnp.int32, (2 * C, 2 * C), 0)
    c2 = lax.broadcasted_iota(jnp.int32, (2 * C, 2 * C), 1)
    eye = (r2 == c2).astype(F32)
    blk_shift = int(math.log2(RWKV_INV_BLOCK))
    same_blk = (r2 >> blk_shift) == (c2 >> blk_shift)
    same_head = (r2 < C) == (c2 < C)
    same_head2 = jnp.concatenate([same_head, same_head], axis=0)
    zeros_c = jnp.zeros((C, LANES), F32)
    zeros_2c = jnp.zeros((2 * C, LANES), F32)

    units = [(ci, p) for ci in range(nchunks) for p in range(npairs)]
    blk = lambda t, u: t[u[0] * C:(u[0] + 1) * C, u[1] * LANES:(u[1] + 1) * LANES]
    each = lambda f, *ls: [f(*xs) for xs in zip(*ls)]

    lhs = [jnp.concatenate([blk(At, u), blk(Rt, u)], axis=0) for u in units]
    rhs = [jnp.concatenate([split(blk(Bt, u)), split(blk(Kt, u))], axis=0) for u in units]
    G = each(_mm_nt, lhs, rhs)
    a_row = [jnp.where(gcol < grow, t[:C], 0.0) for t in G]
    m_row = [jnp.where(gcol <= grow, t[C:], 0.0) for t in G]
    a_bd = [split(t[:, :2 * C]) for t in a_row]

    a_d = [jnp.where(same_blk, t, 0.0) for t in a_bd]
    a_off = each(lambda x, y: x - y, a_bd, a_d)
    dinv = [eye + t for t in a_d]
    pw = a_d
    for _ in range(blk_shift - 1):
        pw = each(_mm, pw, pw)
        dinv = each(lambda x, y: x + _mm(x, y), dinv, pw)
    n1 = each(_mm, dinv, a_off)
    n2 = each(_mm, n1, n1)
    n3 = each(_mm, n1, n2)
    tinv = each(lambda x1, x2, x3, dv: _mm(eye + x1 + x2 + x3, dv), n1, n2, n3, dinv)

    vp = [blk(v, u) for u in units]
    v_st = [split(t) for t in vp]
    rhs0 = each(lambda ar, vs: _mm(ar[:, 2 * C:], vs), a_row, v_st)
    wu = each(lambda t, l, r0: _mm(t, jnp.concatenate([split(l[:C]), split(r0)], axis=1)),
              tinv, lhs, rhs0)
    wu_lp = [t[:C] + t[C:] for t in wu]
    mn = each(lambda w_, v_, u: _mm_tn(
        jnp.concatenate([w_, jnp.concatenate([zeros_c, v_], axis=1)], axis=0),
        jnp.concatenate([blk(Bend, u), blk(Kend, u)], axis=0)), wu_lp, vp, units)
    mn = [jnp.where(same_head2, t, 0.0) for t in mn]
    qy = each(lambda m_, w_, vs: _mm(m_, jnp.concatenate(
        [w_, jnp.concatenate([zeros_2c, vs], axis=1)], axis=0)), m_row, wu, v_st)

    ys = [[None] * npairs for _ in range(nchunks)]
    states = [s_ref[p] for p in range(npairs)]
    for i, (ci, p) in enumerate(units):
        s_old = states[p]
        q = lhs[i][C:] + qy[i][:, :LANES]
        ys[ci][p] = _mm_nt(q, s_old) + qy[i][:, LANES:]
        states[p] = s_old * blk(pc, (ci, p))[0:1, :] + _mm(s_old, mn[i][:LANES]) + mn[i][LANES:]
    for p in range(npairs):
        s_ref[p] = states[p]

    y = jnp.concatenate([jnp.concatenate(t, axis=1) for t in ys], axis=0)
    inv_n = 1.0 / RWKV_HEAD_DIM
    mean = segsum(y) * inv_n
    d = y - mean
    var = segsum(d * d) * inv_n
    yn = d * lax.rsqrt(var + GN_EPS) * lng_ref[...] + lnb_ref[...]
    bonus = segsum(r * kmod * rk_ref[...]) * v
    o_ref[...] = ((yn + bonus) * g).astype(o_ref.dtype)


def _rwkv_cum_matrix(rows):
    t = jnp.arange(rows)
    same = (t[:, None] // RWKV_CHUNK) == (t[None, :] // RWKV_CHUNK)
    return jnp.concatenate([same & (t[None, :] <= t[:, None]), same], axis=0).astype(BF16)


def _rwkv(crw3, mu, wl, w0, a0, k_k, k_a, r_k, ln_g, ln_b, ones_bd, cum_mat):
    b, s, _ = crw3.shape
    R = cum_mat.shape[1]
    W = RWKV_WIDTH
    vec = lambda n: pl.BlockSpec((1, n), lambda i, j: (0, 0))
    return pl.pallas_call(
        _rwkv_kernel,
        grid=(b, s // R),
        in_specs=[pl.BlockSpec((None, R, RWKV_COLS), lambda i, j: (i, j, 0)),
                  vec(RWKV_COLS),
                  pl.BlockSpec((LORA_COLS, 3 * W), lambda i, j: (0, 0)),
                  vec(W), vec(W), vec(W), vec(W), vec(W), vec(W), vec(W),
                  pl.BlockSpec((LANES, LANES), lambda i, j: (0, 0)),
                  pl.BlockSpec((2 * R, R), lambda i, j: (0, 0))],
        out_specs=pl.BlockSpec((None, R, W), lambda i, j: (i, j, 0)),
        out_shape=jax.ShapeDtypeStruct((b, s, W), BF16),
        scratch_shapes=[pltpu.VMEM((1, RWKV_COLS), F32),
                        pltpu.VMEM((W // LANES, LANES, LANES), F32)],
        compiler_params=pltpu.CompilerParams(dimension_semantics=("arbitrary", "arbitrary"),
                                             vmem_limit_bytes=VMEM_LIMIT),
        name="rwkv",
    )(crw3, mu, wl, w0, a0, k_k, k_a, r_k, ln_g, ln_b, ones_bd, cum_mat)


def _s5_mats(lam_re, lam_im, log_dt, b_re, b_im, c_re, c_im, d_skip):
    L = S5_CHUNK
    G, N = lam_re.shape
    ch = b_re.shape[-1]
    dt = jnp.exp(log_dt)[:, None]
    lr, li = lam_re, lam_im
    mag = jnp.exp(lr * dt)
    lb_re, lb_im = mag * jnp.cos(li * dt), mag * jnp.sin(li * dt)
    den = lr * lr + li * li
    nr, ni = lb_re - 1.0, lb_im
    coef_re = (nr * lr + ni * li) / den
    coef_im = (ni * lr - nr * li) / den
    bb_re = coef_re[..., None] * b_re - coef_im[..., None] * b_im
    bb_im = coef_re[..., None] * b_im + coef_im[..., None] * b_re
    prs, pis = [jnp.ones_like(lb_re)], [jnp.zeros_like(lb_im)]
    for _ in range(L):
        pr_, pi_ = prs[-1], pis[-1]
        prs.append(pr_ * lb_re - pi_ * lb_im)
        pis.append(pr_ * lb_im + pi_ * lb_re)
    pr = jnp.stack(prs)
    pi = jnp.stack(pis)
    hp = lax.Precision.HIGHEST

    def lam_bb(qr, qi):
        return (qr[..., None] * bb_re[None] - qi[..., None] * bb_im[None],
                qr[..., None] * bb_im[None] + qi[..., None] * bb_re[None])

    def repeat_lanes(a, reps):
        k = a.shape[-1]
        rep = (jnp.arange(k)[:, None] == (jnp.arange(k * reps)[None, :] % k)).astype(F32)
        return jnp.dot(a, rep, precision=hp)

    def group_expand(a_rep, row_group, groups_per_tile, per_group):
        rows, mid, _ = a_rep.shape
        tiles = G // groups_per_tile
        lane_group = (jnp.arange(tiles)[:, None] * groups_per_tile
                      + jnp.arange(LANES)[None, :] // per_group)
        mask = row_group[:, None, None, None] == lane_group[None, None]
        out = jnp.where(mask, a_rep[:, :, None, :], 0.0).astype(BF16)
        return out.reshape(rows, mid * tiles * LANES)

    in_rows = L * G * ch
    in_group = (jnp.arange(in_rows) // ch) % G
    st_rows = 2 * G * N
    st_group = (jnp.arange(st_rows) // N) % G

    wre, wim = lam_bb(pr[:L][::-1], pi[:L][::-1])
    w_parts = []
    for part in (wre, wim):
        a = jnp.swapaxes(part, 2, 3).reshape(in_rows, N)
        a = repeat_lanes(a, LANES // N)[:, None, :]
        w_parts.append(group_expand(a, in_group, LANES // N, N))
    lre, lim = lam_bb(pr[:L], pi[:L])
    kern = (jnp.einsum('gon,lgni->lgio', c_re, lre, precision=hp)
            - jnp.einsum('gon,lgni->lgio', c_im, lim, precision=hp))
    kern = kern.at[0].add(d_skip[:, :, None] * jnp.eye(ch, dtype=F32)[None])
    lag = jnp.arange(L)[None, :] - jnp.arange(L)[:, None]
    kst = jnp.where((lag >= 0)[:, :, None, None, None], kern[jnp.maximum(lag, 0)], 0.0)
    kf = jnp.transpose(kst, (0, 2, 3, 1, 4)).reshape(in_rows * L, ch)
    kf = repeat_lanes(kf, LANES // ch).reshape(in_rows, L, LANES)
    toep = group_expand(kf, in_group, LANES // ch, ch)
    qr, qi = pr[1:L + 1][:, :, None, :], pi[1:L + 1][:, :, None, :]
    o_re = c_re[None] * qr - c_im[None] * qi
    o_im = -c_re[None] * qi - c_im[None] * qr
    of = jnp.transpose(jnp.stack([o_re, o_im]), (0, 2, 4, 1, 3))
    of = repeat_lanes(of.reshape(st_rows * L, ch), LANES // ch).reshape(st_rows, L, LANES)
    w_out_flat = group_expand(of, st_group, LANES // ch, ch)
    w_a = jnp.concatenate(w_parts + [toep], axis=1)
    plr = pr[L].reshape(1, G * N)
    pli = pi[L].reshape(1, G * N)
    return w_a, w_out_flat, plr, pli


def _s5_kernel(u_ref, wa_ref, wo_ref, plr_ref, pli_ref, o_ref, wloc_ref, zprev_ref):
    nch, L, w = u_ref.shape
    half = S5_ZW // 2

    r = _mm(u_ref[:, 0, :], wa_ref[0:w, :])
    for j in range(1, L):
        r = r + _mm(u_ref[:, j, :], wa_ref[j * w:(j + 1) * w, :])
    wloc_ref[...] = r[:, :S5_ZW]
    y_lag = r[:, S5_ZW:]

    plr = plr_ref[...]
    pli = pli_ref[...]

    def step(ci, z):
        zprev_ref[pl.ds(ci, 1), :] = z
        zr, zi = z[:, :half], z[:, half:]
        nz = jnp.concatenate([plr * zr - pli * zi, plr * zi + pli * zr], axis=1)
        return nz + wloc_ref[pl.ds(ci, 1), :]

    lax.fori_loop(0, nch, step, jnp.zeros((1, S5_ZW), F32))
    y = y_lag + _mm(zprev_ref[...], wo_ref[...])
    for j in range(L):
        o_ref[:, j, :] = y[:, j * w:(j + 1) * w]


def _s5(u4, w_a, w_o, plr, pli):
    b, nch, L, w = u4.shape
    const = lambda a: pl.BlockSpec(a.shape, lambda i: (0,) * a.ndim, pipeline_mode=pl.Buffered(1))
    blk = pl.BlockSpec((None, nch, L, w), lambda i: (i, 0, 0, 0))
    return pl.pallas_call(
        _s5_kernel,
        grid=(b,),
        in_specs=[blk, const(w_a), const(w_o), const(plr), const(pli)],
        out_specs=blk,
        out_shape=jax.ShapeDtypeStruct(u4.shape, F32),
        scratch_shapes=[pltpu.VMEM((nch, S5_ZW), F32),
                        pltpu.VMEM((nch, S5_ZW), F32)],
        compiler_params=pltpu.CompilerParams(dimension_semantics=("arbitrary",),
                                             vmem_limit_bytes=VMEM_LIMIT),
        name="s5",
    )(u4, w_a, w_o, plr, pli)


def _merge_tile_math(x_ref, ya_ref, yb_ref, gates_ref, gluw_ref, glub_ref, wba_ref, wbb_ref, wout_ref,
                     fng_ref, rw_ref, rb_ref, tri_ref, upper_ref, x1_ref, t_ref, rt_ref, pos_ref,
                     te_ref, fill_ref, nused_ref, cnt_ref, cur_ref):
    d = x_ref.shape[1]
    tm = x_ref.shape[0]
    y_a = jnp.dot(ya_ref[...], wba_ref[...], preferred_element_type=F32)
    ys = yb_ref[...]
    z = 0.5 * ys * (1.0 + jnp.tanh(math.sqrt(2.0 / math.pi) * (ys + 0.044715 * (ys * ys * ys))))
    z = z * _sigmoid(_mm(z, gluw_ref[...]) + glub_ref[...])
    y_b = _mm(z, wbb_ref[...])
    gates = gates_ref[...].astype(F32)
    merged = _sigmoid(gates[:, :d]) * y_a + _sigmoid(gates[:, d:]) * y_b
    x1 = x_ref[...] + _mm(merged, wout_ref[...])
    x1_ref[...] = x1
    t = _rms_norm(x1, fng_ref[...])
    t_hi = t.astype(BF16)
    t_ref[...] = t

    t_lo = (t - t_hi.astype(F32)).astype(BF16)
    hh_hl = jnp.dot(t_hi, rw_ref[...], preferred_element_type=F32)
    lh = jnp.dot(t_lo, rw_ref[:, :ROUTER_LANES], preferred_element_type=F32)
    logits = hh_hl[:, :ROUTER_LANES] + hh_hl[:, ROUTER_LANES:] + lh + rb_ref[...]
    lane = lax.broadcasted_iota(jnp.int32, logits.shape, 1)
    neg = -jnp.inf
    lane_f = lane.astype(F32)
    big = float(1 << 20)
    is_g = (lane >= N_EXPERTS) & (lane < N_EXPERTS + N_GROUPS)
    gl = jnp.where(is_g, logits, neg)
    gmax = jnp.max(gl, axis=-1, keepdims=True)
    g_p = 1.0 / jnp.sum(jnp.exp(gl - gmax), axis=-1, keepdims=True)
    g_idx = jnp.min(jnp.where(gl == gmax, lane_f - N_EXPERTS, big), axis=-1,
                    keepdims=True).astype(jnp.int32)
    el = jnp.where((lane < N_EXPERTS) & ((lane >> int(math.log2(EXPERTS_PER_GROUP))) == g_idx), logits, neg)
    t1 = jnp.max(el, axis=-1, keepdims=True)
    i1 = jnp.min(jnp.where(el == t1, lane_f, big), axis=-1, keepdims=True).astype(jnp.int32)
    el2 = jnp.where(lane == i1, neg, el)
    t2 = jnp.max(el2, axis=-1, keepdims=True)
    i2 = jnp.min(jnp.where(el2 == t2, lane_f, big), axis=-1, keepdims=True).astype(jnp.int32)
    e21 = jnp.exp(t2 - t1)
    w1 = g_p / (1.0 + e21)
    w2 = g_p * e21 / (1.0 + e21)
    rt_ref[...] = jnp.where(lane == 0, w1, jnp.where(lane == 1, w2, 0.0))

    sh = int(math.log2(MOE_TILE))
    oh1 = lane == i1
    oh2 = lane == i2
    ind = jnp.where(oh1 | oh2, 1.0, 0.0)
    lrank = jnp.dot(tri_ref[...], ind.astype(BF16), preferred_element_type=F32).astype(jnp.int32)
    n_new = lrank[tm - 1:tm, :] + ind[tm - 1:tm, :].astype(jnp.int32)
    cnt = cnt_ref[...]
    cur = cur_ref[...]
    nfree = nused_ref[...]
    tiles_before = (cnt + (MOE_TILE - 1)) >> sh
    newf = ((cnt + n_new + (MOE_TILE - 1)) >> sh) - tiles_before
    newf8 = jnp.broadcast_to(newf.astype(BF16), (8, newf.shape[1]))
    pre = jnp.dot(newf8, upper_ref[...], preferred_element_type=F32)[0:1, :].astype(jnp.int32)
    new_tile = nfree + pre
    grank = cnt + lrank
    ptile = jnp.where((grank >> sh) < tiles_before, cur, new_tile)
    posfull = ((ptile << sh) + (grank & (MOE_TILE - 1))).astype(F32)
    pos1 = jnp.sum(jnp.where(oh1, posfull, 0.0), axis=-1, keepdims=True)
    pos2 = jnp.sum(jnp.where(oh2, posfull, 0.0), axis=-1, keepdims=True)
    tr = lax.broadcasted_iota(jnp.int32, (tm, tm), 0)
    tc = lax.broadcasted_iota(jnp.int32, (tm, tm), 1)
    as_row = lambda v: jnp.sum(jnp.where(tr == tc, jnp.broadcast_to(v, (tm, tm)), 0.0),
                               axis=0, keepdims=True)
    sub = lax.broadcasted_iota(jnp.int32, pos_ref.shape, 0)
    pos_ref[...] = jnp.where(sub == 0, as_row(pos1),
                             jnp.where(sub == 1, as_row(pos2), 0.0)).astype(jnp.int32)

    el_r = lax.broadcasted_iota(jnp.int32, (ROUTER_LANES, ROUTER_LANES), 0)
    el_c = lax.broadcasted_iota(jnp.int32, (ROUTER_LANES, ROUTER_LANES), 1)
    as_col = lambda v: jnp.sum(jnp.where(el_r == el_c, jnp.broadcast_to(v, el_r.shape), 0.0),
                               axis=1, keepdims=True)
    tile_col = as_col(jnp.where(newf > 0, new_tile, -1).astype(F32)).astype(jnp.int32)
    tlane = lax.broadcasted_iota(jnp.int32, (ROUTER_LANES, te_ref.shape[1]), 1)
    erow = lax.broadcasted_iota(jnp.int32, (ROUTER_LANES, te_ref.shape[1]), 0).astype(F32)
    te_ref[...] += jnp.sum(jnp.where(tlane == tile_col, erow, 0.0), axis=0,
                           keepdims=True).astype(jnp.int32)
    to_cur = jnp.minimum(n_new, (tiles_before << sh) - cnt)
    cur_col = as_col(cur.astype(F32)).astype(jnp.int32)
    added = (jnp.where(tlane == cur_col, as_col(to_cur.astype(F32)), 0.0)
             + jnp.where(tlane == tile_col, as_col((n_new - to_cur).astype(F32)), 0.0))
    fill_ref[...] += jnp.sum(added, axis=0, keepdims=True).astype(jnp.int32)
    cnt_ref[...] = cnt + n_new
    cur_ref[...] = jnp.where(newf > 0, new_tile, cur)
    nused_ref[...] = nfree + jnp.sum(newf.astype(F32), axis=-1, keepdims=True).astype(jnp.int32)


def _merge_kernel(x_ref, ya_ref, yb_ref, gates_ref, gluw_ref, glub_ref, wba_ref, wbb_ref, wout_ref,
                  fng_ref, rw_ref, rb_ref, tri_ref, upper_ref, x1_ref, rt_ref, pos_ref,
                  te_ref, fill_ref, nused_ref, xs_ref, cnt_ref, cur_ref, tbuf, posv, pos_smem, fill_smem,
                  zblk, rsem, psem, zsem):
    i = pl.program_id(0)
    last = pl.num_programs(0) - 1
    tm = x_ref.shape[0]

    @pl.when(i == 0)
    def _():
        cnt_ref[...] = jnp.zeros_like(cnt_ref)
        cur_ref[...] = jnp.zeros_like(cur_ref)
        te_ref[...] = jnp.zeros_like(te_ref)
        fill_ref[...] = jnp.zeros_like(fill_ref)
        nused_ref[...] = jnp.zeros_like(nused_ref)

    def row_copies(tile, unroll, op):
        slot = tile % 2

        def body(r, carry):
            for k in range(2):
                op(pltpu.make_async_copy(tbuf.at[slot, pl.ds(r, 1), :],
                                         xs_ref.at[pl.ds(pos_smem[slot, k, r], 1), :], rsem.at[slot]))
            return carry
        lax.fori_loop(0, tm, body, 0, unroll=unroll)

    def drain_rows(tile):
        slot = tile % 2
        for _ in range(2):
            pltpu.make_async_copy(tbuf.at[slot], xs_ref.at[pl.ds(0, tm), :], rsem.at[slot]).wait()

    def pos_to_smem(tile):
        return pltpu.make_async_copy(posv, pos_smem.at[pl.ds(tile % 2, 1)], psem)

    @pl.when(i >= 1)
    def _():
        pos_to_smem(i - 1).wait()

    @pl.when(i >= 2)
    def _():
        drain_rows(i - 2)

    def step(dispatch_prev):
        if dispatch_prev:
            row_copies(i - 1, True, lambda c: c.start())
        _merge_tile_math(x_ref, ya_ref, yb_ref, gates_ref, gluw_ref, glub_ref, wba_ref, wbb_ref,
                         wout_ref, fng_ref, rw_ref, rb_ref, tri_ref, upper_ref, x1_ref,
                         tbuf.at[i % 2], rt_ref, pos_ref.at[0], te_ref, fill_ref, nused_ref, cnt_ref,
                         cur_ref)
        posv[...] = pos_ref[...]

    pl.when(i == 0)(lambda: step(False))
    pl.when(i > 0)(lambda: step(True))
    pos_to_smem(i).start()

    @pl.when(i == last)
    def _():
        pos_to_smem(i).wait()
        row_copies(i, DMA_UNROLL, lambda c: c.start())

        @pl.when(i >= 1)
        def _():
            drain_rows(i - 1)
        drain_rows(i)

        fill_copy = pltpu.make_async_copy(fill_ref, fill_smem, psem)
        fill_copy.start()
        zblk[...] = jnp.zeros_like(zblk)
        fill_copy.wait()

        def over_tails(op):
            def tile(j, carry):
                fill = fill_smem[0, j]
                head = (-fill) & (SUBLANES - 1)
                for r in range(SUBLANES - 1):
                    @pl.when(r < head)
                    def _(r=r):
                        op(pltpu.make_async_copy(zblk.at[pl.ds(0, 1), :],
                                                 xs_ref.at[pl.ds(j * MOE_TILE + fill + r, 1), :], zsem))
                start = fill + head
                todo = MOE_TILE - start
                size = MOE_TILE
                while size >= SUBLANES:
                    @pl.when((todo & size) != 0)
                    def _(start=start, size=size):
                        off = pl.multiple_of(j * MOE_TILE + start, SUBLANES)
                        op(pltpu.make_async_copy(zblk.at[pl.ds(0, size), :],
                                                 xs_ref.at[pl.ds(off, size), :], zsem))
                    start = start + (todo & size)
                    size //= 2
                return carry
            lax.fori_loop(0, xs_ref.shape[0] // MOE_TILE, tile, 0)

        over_tails(lambda c: c.start())
        over_tails(lambda c: c.wait())


def _moe_tiles(t):
    return (2 * t) // MOE_TILE + N_EXPERTS


def _merge(x2, ya, yb, gates, gluw, glub, wba, wbb, wout, fng, rw, rb):
    t, d = x2.shape
    tm = MERGE_TILE
    assert t % tm == 0 and tm <= MOE_TILE
    n_tiles = _moe_tiles(t)
    te_lanes = -(-n_tiles // LANES) * LANES
    rr = jnp.arange(tm)
    tri = (rr[None, :] < rr[:, None]).astype(BF16)
    ll = jnp.arange(ROUTER_LANES)
    upper = (ll[:, None] < ll[None, :]).astype(BF16)
    full = lambda a: pl.BlockSpec(a.shape, lambda i: (0,) * a.ndim)
    rowblk = lambda n: pl.BlockSpec((tm, n), lambda i: (i, 0))
    fixed = lambda n: pl.BlockSpec((1, n), lambda i: (0, 0))
    return pl.pallas_call(
        _merge_kernel,
        grid=(t // tm,),
        in_specs=[rowblk(d), rowblk(ya.shape[1]), rowblk(yb.shape[1]), rowblk(gates.shape[1]),
                  full(gluw), full(glub), full(wba), full(wbb), full(wout), full(fng), full(rw),
                  full(rb), full(tri), full(upper)],
        out_specs=[rowblk(d), rowblk(ROUTER_LANES),
                   pl.BlockSpec((1, SUBLANES, tm), lambda i: (i, 0, 0)),
                   fixed(te_lanes), fixed(te_lanes), fixed(ROUTER_LANES),
                   pl.BlockSpec(memory_space=pl.ANY)],
        out_shape=[jax.ShapeDtypeStruct((t, d), F32),
                   jax.ShapeDtypeStruct((t, ROUTER_LANES), F32),
                   jax.ShapeDtypeStruct((t // tm, SUBLANES, tm), jnp.int32),
                   jax.ShapeDtypeStruct((1, te_lanes), jnp.int32),
                   jax.ShapeDtypeStruct((1, te_lanes), jnp.int32),
                   jax.ShapeDtypeStruct((1, ROUTER_LANES), jnp.int32),
                   jax.ShapeDtypeStruct((n_tiles * MOE_TILE, d), F32)],
        scratch_shapes=[pltpu.VMEM((1, ROUTER_LANES), jnp.int32),
                        pltpu.VMEM((1, ROUTER_LANES), jnp.int32),
                        pltpu.VMEM((2, tm, d), F32),
                        pltpu.VMEM((1, SUBLANES, tm), jnp.int32),
                        pltpu.SMEM((2, SUBLANES, tm), jnp.int32),
                        pltpu.SMEM((1, te_lanes), jnp.int32),
                        pltpu.VMEM((MOE_TILE, d), F32),
                        pltpu.SemaphoreType.DMA((2,)),
                        pltpu.SemaphoreType.DMA(()),
                        pltpu.SemaphoreType.DMA(())],
        compiler_params=pltpu.CompilerParams(dimension_semantics=("arbitrary",),
                                             vmem_limit_bytes=VMEM_LIMIT),
        name="merge",
    )(x2, ya, yb, gates, gluw, glub, wba, wbb, wout, fng, rw, rb, tri, upper)


def _experts_kernel(te_ref, nused_ref, xs_ref, wg_ref, wu_ref, wd_ref, y_ref):
    j = pl.program_id(0)

    @pl.when(j < nused_ref[0])
    def _():
        x = xs_ref[...].astype(BF16)
        hg = _mm(x, wg_ref[...])
        hid = hg * _sigmoid(hg) * _mm(x, wu_ref[...])
        y_ref[...] = _mm(hid, wd_ref[...])

    @pl.when(j >= nused_ref[0])
    def _():
        y_ref[...] = jnp.zeros_like(y_ref)


def _experts(tile_expert, n_used, xs, wg, wu, wd):
    n_rows, hw = xs.shape
    ne, d, de = wg.shape
    return pl.pallas_call(
        _experts_kernel,
        grid_spec=pltpu.PrefetchScalarGridSpec(
            num_scalar_prefetch=2,
            grid=(n_rows // MOE_TILE,),
            in_specs=[pl.BlockSpec((MOE_TILE, hw), lambda j, te, nu: (j, 0)),
                      pl.BlockSpec((None, d, de), lambda j, te, nu: (te[j], 0, 0)),
                      pl.BlockSpec((None, d, de), lambda j, te, nu: (te[j], 0, 0)),
                      pl.BlockSpec((None, de, d), lambda j, te, nu: (te[j], 0, 0))],
            out_specs=pl.BlockSpec((MOE_TILE, hw), lambda j, te, nu: (j, 0))),
        out_shape=jax.ShapeDtypeStruct((n_rows, hw), F32),
        compiler_params=pltpu.CompilerParams(dimension_semantics=("arbitrary",),
                                             vmem_limit_bytes=VMEM_LIMIT),
        name="experts",
    )(tile_expert, n_used, xs, wg, wu, wd)


def _ple_kernel(pos1_ref, pos2_ref, x_ref, rt_ref, p_ref, png_ref, wg_ref, wp_ref, fng_ref, y_ref,
                o_ref, ybuf, sem):
    i = pl.program_id(0)
    tm = x_ref.shape[0]

    def gather(tile, unroll):
        slot = tile % 2

        def body(r, carry):
            for k, pos_ref in enumerate((pos1_ref, pos2_ref)):
                pltpu.make_async_copy(y_ref.at[pl.ds(pos_ref[tile * tm + r], 1), :],
                                      ybuf.at[slot, k, pl.ds(r, 1), :], sem.at[slot]).start()
            return carry
        lax.fori_loop(0, tm, body, 0, unroll=unroll)

    @pl.when(i == 0)
    def _():
        gather(i, DMA_UNROLL)

    def step(prefetch_next):
        slot = i % 2
        for k in range(2):
            pltpu.make_async_copy(y_ref.at[pl.ds(0, tm), :], ybuf.at[slot, k], sem.at[slot]).wait()
        if prefetch_next:
            gather(i + 1, True)
        rt = rt_ref[...]
        x2 = x_ref[...] + rt[:, 0:1] * ybuf[slot, 0] + rt[:, 1:2] * ybuf[slot, 1]
        hp = _rms_norm(x2, png_ref[...])
        gate = _sigmoid(_mm(hp, wg_ref[...]))
        x3 = x2 + gate * _mm(p_ref[...], wp_ref[...])
        o_ref[...] = _rms_norm(x3, fng_ref[...])

    last = pl.num_programs(0) - 1
    pl.when(i < last)(lambda: step(True))
    pl.when(i == last)(lambda: step(False))


def _ple(pos1, pos2, x1, rt, p2, png, wg, wp, fng, y_pack, tm):
    t, d = x1.shape
    hw = y_pack.shape[1]
    full = lambda a: pl.BlockSpec(a.shape, lambda i, p1, p2_: (0,) * a.ndim)
    rowblk = lambda n: pl.BlockSpec((tm, n), lambda i, p1, p2_: (i, 0))
    return pl.pallas_call(
        _ple_kernel,
        grid_spec=pltpu.PrefetchScalarGridSpec(
            num_scalar_prefetch=2,
            grid=(t // tm,),
            in_specs=[rowblk(d), rowblk(rt.shape[1]), rowblk(p2.shape[1]),
                      full(png), full(wg), full(wp), full(fng),
                      pl.BlockSpec(memory_space=pl.ANY)],
            out_specs=rowblk(d),
            scratch_shapes=[pltpu.VMEM((2, 2, tm, hw), F32),
                            pltpu.SemaphoreType.DMA((2,))]),
        out_shape=jax.ShapeDtypeStruct((t, d), F32),
        compiler_params=pltpu.CompilerParams(dimension_semantics=("arbitrary",),
                                             vmem_limit_bytes=VMEM_LIMIT),
        name="ple",
    )(pos1, pos2, x1, rt, p2, png, wg, wp, fng, y_pack)


def _row_tile(t, want):
    tm = min(want, t)
    while t % tm:
        tm //= 2
    return tm


def _layer(x, p, mix_norm, w_in, mu_shift, rk_w0, rk_w_up, rk_a0, rk_a_up, rk_g_up,
           rk_k_k, rk_k_a, rk_r_k, rk_ln_g, rk_ln_b, s5_lam_re, s5_lam_im, s5_log_dt,
           s5_b_re, s5_b_im, s5_c_re, s5_c_im, s5_d, s5_glu_w, s5_glu_b,
           w_branch_a, w_branch_b, w_out, ffn_norm, router_group_w, router_group_b,
           router_expert_w, router_expert_b, exp_w_gate, exp_w_up, exp_w_down,
           ple_norm, ple_gate_w, ple_proj):
    b, s, d = x.shape
    t = b * s
    W = RWKV_WIDTH
    row = lambda a: a.reshape(1, -1).astype(F32)
    x2 = x.reshape(t, d)

    crw, us5, gates = _in_proj(x2, row(mix_norm), w_in.astype(BF16), _row_tile(t, 256))

    wl = jnp.zeros((LORA_COLS, 3 * W), F32)
    wl = wl.at[:DECAY_LORA, :W].set(rk_w_up)
    wl = wl.at[DECAY_LORA:DECAY_LORA + AAA_LORA, W:2 * W].set(rk_a_up)
    wl = wl.at[DECAY_LORA + AAA_LORA:, 2 * W:].set(rk_g_up)
    hid = jnp.arange(LANES) // RWKV_HEAD_DIM
    ones_bd = (hid[:, None] == hid[None, :]).astype(BF16)
    ya = _rwkv(crw.reshape(b, s, RWKV_COLS), row(mu_shift), wl.astype(BF16), row(rk_w0), row(rk_a0),
               row(rk_k_k), row(rk_k_a), row(rk_r_k), row(rk_ln_g), row(rk_ln_b), ones_bd,
               _rwkv_cum_matrix(_row_tile(s, RWKV_BLOCK)))

    s5_wa, s5_wo, plr, pli = _s5_mats(s5_lam_re, s5_lam_im, s5_log_dt, s5_b_re, s5_b_im,
                                      s5_c_re, s5_c_im, s5_d)
    yb = _s5(us5.reshape(b, s // S5_CHUNK, S5_CHUNK, S5_WIDTH), s5_wa, s5_wo, plr, pli)

    rw = jnp.zeros((d, ROUTER_LANES), F32)
    rw = rw.at[:, :N_EXPERTS].set(router_expert_w).at[:, N_EXPERTS:N_EXPERTS + N_GROUPS].set(router_group_w)
    rb = jnp.zeros((1, ROUTER_LANES), F32)
    rb = rb.at[0, :N_EXPERTS].set(router_expert_b).at[0, N_EXPERTS:N_EXPERTS + N_GROUPS].set(router_group_b)
    rw_hi = rw.astype(BF16)
    rw = jnp.concatenate([rw_hi, (rw - rw_hi.astype(F32)).astype(BF16)], axis=1)
    x1, rt, pos, tile_expert, _, n_used, xs = _merge(
        x2, ya.reshape(t, W), yb.reshape(t, S5_WIDTH), gates, s5_glu_w.astype(BF16), row(s5_glu_b),
        w_branch_a.astype(BF16), w_branch_b.astype(BF16), w_out.astype(BF16), row(ffn_norm), rw, rb)
    n_tiles = _moe_tiles(t)
    pos1, pos2 = pos[:, 0, :].reshape(t), pos[:, 1, :].reshape(t)
    y_pack = _experts(tile_expert[0, :n_tiles], n_used[0, :1], xs, exp_w_gate, exp_w_up, exp_w_down)
    return (pos1, pos2, x1, rt, p.reshape(t, -1), row(ple_norm), ple_gate_w.astype(BF16),
            ple_proj.astype(BF16), y_pack)


def kernel(x, p, mix_norm, w_in, mu_shift, rk_w0, rk_w_up, rk_a0, rk_a_up, rk_g_up, rk_k_k, rk_k_a,
           rk_r_k, rk_ln_g, rk_ln_b, s5_lam_re, s5_lam_im, s5_log_dt, s5_b_re, s5_b_im, s5_c_re,
           s5_c_im, s5_d, s5_glu_w, s5_glu_b, w_branch_a, w_branch_b, w_out, ffn_norm,
           router_group_w, router_group_b, router_expert_w, router_expert_b, exp_w_gate, exp_w_up,
           exp_w_down, ple_norm, ple_gate_w, ple_proj, final_norm):
    b, s, d = x.shape
    depth = w_in.shape[0]
    assert depth == 1, "the final norm is fused into the last layer's PLE kernel"
    i = 0
    pos1, pos2, x1, rt, p2, png, wpg, wpp, y_pack = _layer(
        x, p[i], mix_norm[i], w_in[i], mu_shift[i], rk_w0[i], rk_w_up[i], rk_a0[i], rk_a_up[i],
        rk_g_up[i], rk_k_k[i], rk_k_a[i], rk_r_k[i], rk_ln_g[i], rk_ln_b[i], s5_lam_re[i],
        s5_lam_im[i], s5_log_dt[i], s5_b_re[i], s5_b_im[i], s5_c_re[i], s5_c_im[i], s5_d[i],
        s5_glu_w[i], s5_glu_b[i], w_branch_a[i], w_branch_b[i], w_out[i], ffn_norm[i],
        router_group_w[i], router_group_b[i], router_expert_w[i], router_expert_b[i],
        exp_w_gate[i], exp_w_up[i], exp_w_down[i], ple_norm[i], ple_gate_w[i], ple_proj[i])
    out = _ple(pos1, pos2, x1, rt, p2, png, wpg, wpp, final_norm.reshape(1, -1).astype(F32), y_pack,
               _row_tile(b * s, MERGE_TILE))
    return out.reshape(b, s, d)
```

```python
import functools
import math

import jax
import jax.numpy as jnp
from jax import lax
from jax.experimental import pallas as pl
from jax.experimental.pallas import tpu as pltpu

F32 = jnp.float32
BF16 = jnp.bfloat16

NORM_EPS = 1e-6
GN_EPS = 64e-5

RWKV_HEADS = 8
RWKV_HEAD_DIM = 64
RWKV_WIDTH = RWKV_HEADS * RWKV_HEAD_DIM
DECAY_LORA = 64
AAA_LORA = 64
GATE_LORA = 128
LORA_COLS = DECAY_LORA + AAA_LORA + GATE_LORA
RWKV_COLS = 3 * RWKV_WIDTH + LORA_COLS
S5_GROUPS = 16
S5_GROUP_CH = 16
S5_WIDTH = S5_GROUPS * S5_GROUP_CH
S5_STATE = 64
S5_ZW = 2 * S5_GROUPS * S5_STATE
N_GROUPS = 4
EXPERTS_PER_GROUP = 8
N_EXPERTS = N_GROUPS * EXPERTS_PER_GROUP

LANES = 128
SUBLANES = 8
RWKV_CHUNK = 64
RWKV_INV_BLOCK = 16
RWKV_BLOCK = 256
S5_CHUNK = 8
ROUTER_LANES = 128
MOE_TILE = 512
MERGE_TILE = 256
DMA_UNROLL = 8
VMEM_LIMIT = 56 * 1024 * 1024


def _mm(a, b):
    return jnp.dot(a.astype(BF16), b.astype(BF16), preferred_element_type=F32)


def _mm_nt(a, b):
    return lax.dot_general(a.astype(BF16), b.astype(BF16), (((1,), (1,)), ((), ())),
                           preferred_element_type=F32)


def _mm_tn(a, b):
    return lax.dot_general(a.astype(BF16), b.astype(BF16), (((0,), (0,)), ((), ())),
                           preferred_element_type=F32)


def _split2(x):
    hi = x.astype(BF16)
    return hi, (x - hi.astype(F32)).astype(BF16)


def _mm_split_lhs(a_bf16, x):
    hi, lo = _split2(x)
    d = lambda t: jnp.dot(a_bf16, t, preferred_element_type=F32)
    return d(hi) + d(lo)


def _mm_split_rhs(x, b_bf16):
    hi, lo = _split2(x)
    d = lambda t: jnp.dot(t, b_bf16, preferred_element_type=F32)
    return d(hi) + d(lo)


def _sigmoid(x):
    return 0.5 * jnp.tanh(0.5 * x) + 0.5


def _rms_norm(x, g):
    ms = jnp.mean(x * x, axis=-1, keepdims=True)
    return x * lax.rsqrt(ms + NORM_EPS) * g


def _in_proj_kernel(x_ref, g_ref, w_ref, crw_ref, us5_ref, gates_ref):
    h = _rms_norm(x_ref[...], g_ref[...]).astype(BF16)
    c0, c1 = RWKV_COLS, RWKV_COLS + S5_WIDTH
    crw_ref[...] = jnp.dot(h, w_ref[:, :c0], preferred_element_type=F32)
    us5_ref[...] = jnp.dot(h, w_ref[:, c0:c1], preferred_element_type=F32)
    gates_ref[...] = jnp.dot(h, w_ref[:, c1:], preferred_element_type=F32).astype(BF16)


def _in_proj(x2, g, w, tm):
    t, d = x2.shape
    n = w.shape[1]
    ng = n - RWKV_COLS - S5_WIDTH
    return pl.pallas_call(
        _in_proj_kernel,
        grid=(t // tm,),
        in_specs=[pl.BlockSpec((tm, d), lambda i: (i, 0)),
                  pl.BlockSpec((1, d), lambda i: (0, 0)),
                  pl.BlockSpec((d, n), lambda i: (0, 0))],
        out_specs=[pl.BlockSpec((tm, RWKV_COLS), lambda i: (i, 0)),
                   pl.BlockSpec((tm, S5_WIDTH), lambda i: (i, 0)),
                   pl.BlockSpec((tm, ng), lambda i: (i, 0))],
        out_shape=[jax.ShapeDtypeStruct((t, RWKV_COLS), F32),
                   jax.ShapeDtypeStruct((t, S5_WIDTH), F32),
                   jax.ShapeDtypeStruct((t, ng), BF16)],
        compiler_params=pltpu.CompilerParams(dimension_semantics=("arbitrary",),
                                             vmem_limit_bytes=VMEM_LIMIT),
        name="in_proj",
    )(x2, g, w)


def _rwkv_kernel(c_ref, mu_ref, wl_ref, w0_ref, a0_ref, kk_ref, ka_ref, rk_ref, lng_ref, lnb_ref,
                 ones_ref, cum_ref, o_ref, carry_ref, s_ref):
    C = RWKV_CHUNK
    W = RWKV_WIDTH
    npairs = W // LANES

    @pl.when(pl.program_id(1) == 0)
    def _():
        carry_ref[...] = jnp.zeros_like(carry_ref)
        s_ref[...] = jnp.zeros_like(s_ref)

    c = c_ref[...]
    R = c.shape[0]
    nchunks = R // C
    row = lax.broadcasted_iota(jnp.int32, (R, 1), 0)
    prev = jnp.where(row == 0, carry_ref[...], pltpu.roll(c, 1, 0))
    carry_ref[...] = c[R - 1:R, :]
    cs = c + (prev - c) * mu_ref[...]

    r = cs[:, 0:W]
    k = cs[:, W:2 * W]
    v = cs[:, 2 * W:3 * W]
    lin = cs[:, 3 * W:]
    llane = lax.broadcasted_iota(jnp.int32, lin.shape, 1)
    lact = jnp.where(llane < DECAY_LORA, jnp.tanh(lin),
                     jnp.where(llane < DECAY_LORA + AAA_LORA, lin, _sigmoid(lin)))
    lo = _mm(lact, wl_ref[...])
    zw = -(w0_ref[...] + lo[:, 0:W])
    softplus = jnp.maximum(zw, 0.0) + jnp.log(1.0 + jnp.exp(-jnp.abs(zw)))
    ld = -jnp.exp(-softplus - 0.5)
    a = _sigmoid(a0_ref[...] + lo[:, W:2 * W])
    g = lo[:, 2 * W:3 * W]

    ones_bd = ones_ref[...]
    segsum = lambda t: jnp.concatenate(
        [_mm_split_rhs(t[:, p * LANES:(p + 1) * LANES], ones_bd) for p in range(npairs)], axis=1)
    kk = k * kk_ref[...]
    kkn = kk / jnp.maximum(jnp.sqrt(segsum(kk * kk)), 1e-12)
    kmod = k * (1.0 + (a - 1.0) * ka_ref[...])

    cums = _mm_split_lhs(cum_ref[...], ld)
    cum = cums[:R]
    tot = cums[R:]
    inv = jnp.exp(-cum)
    tail = jnp.exp(tot - cum)
    At = -kkn * jnp.exp(cum - ld)
    Rt = r * jnp.exp(cum)
    kka = kkn * a
    Bt = kka * inv
    Kt = kmod * inv
    Bend = kka * tail
    Kend = kmod * tail
    pc = jnp.exp(tot)

    lane = lax.broadcasted_iota(jnp.int32, (C, LANES), 1)
    h0 = lane < RWKV_HEAD_DIM
    split = lambda t: jnp.concatenate([jnp.where(h0, t, 0.0), jnp.where(h0, 0.0, t)], axis=0)
    grow = lax.broadcasted_iota(jnp.int32, (C, 4 * C), 0)
    gcol = lax.broadcasted_iota(jnp.int32, (C, 4 * C), 1) & (C - 1)
    r2 = lax.broadcasted_iota(jnp.int32, (2 * C, 2 * C), 0)
    c2 = lax.broadcasted_iota(jnp.int32, (2 * C, 2 * C), 1)
    eye = (r2 == c2).astype(F32)
    blk_shift = int(math.log2(RWKV_INV_BLOCK))
    same_blk = (r2 >> blk_shift) == (c2 >> blk_shift)
    same_head = (r2 < C) == (c2 < C)
    same_head2 = jnp.concatenate([same_head, same_head], axis=0)
    zeros_c = jnp.zeros((C, LANES), F32)
    zeros_2c = jnp.zeros((2 * C, LANES), F32)

    units = [(ci, p) for ci in range(nchunks) for p in range(npairs)]
    blk = lambda t, u: t[u[0] * C:(u[0] + 1) * C, u[1] * LANES:(u[1] + 1) * LANES]
    each = lambda f, *ls: [f(*xs) for xs in zip(*ls)]

    lhs = [jnp.concatenate([blk(At, u), blk(Rt, u)], axis=0) for u in units]
    rhs = [jnp.concatenate([split(blk(Bt, u)), split(blk(Kt, u))], axis=0) for u in units]
    G = each(_mm_nt, lhs, rhs)
    a_row = [jnp.where(gcol < grow, t[:C], 0.0) for t in G]
    m_row = [jnp.where(gcol <= grow, t[C:], 0.0) for t in G]
    a_bd = [split(t[:, :2 * C]) for t in a_row]

    a_d = [jnp.where(same_blk, t, 0.0) for t in a_bd]
    a_off = each(lambda x, y: x - y, a_bd, a_d)
    dinv = [eye + t for t in a_d]
    pw = a_d
    for _ in range(blk_shift - 1):
        pw = each(_mm, pw, pw)
        dinv = each(lambda x, y: x + _mm(x, y), dinv, pw)
    n1 = each(_mm, dinv, a_off)
    n2 = each(_mm, n1, n1)
    n3 = each(_mm, n1, n2)
    tinv = each(lambda x1, x2, x3, dv: _mm(eye + x1 + x2 + x3, dv), n1, n2, n3, dinv)

    vp = [blk(v, u) for u in units]
    v_st = [split(t) for t in vp]
    rhs0 = each(lambda ar, vs: _mm(ar[:, 2 * C:], vs), a_row, v_st)
    wu = each(lambda t, l, r0: _mm(t, jnp.concatenate([split(l[:C]), split(r0)], axis=1)),
              tinv, lhs, rhs0)
    wu_lp = [t[:C] + t[C:] for t in wu]
    mn = each(lambda w_, v_, u: _mm_tn(
        jnp.concatenate([w_, jnp.concatenate([zeros_c, v_], axis=1)], axis=0),
        jnp.concatenate([blk(Bend, u), blk(Kend, u)], axis=0)), wu_lp, vp, units)
    mn = [jnp.where(same_head2, t, 0.0) for t in mn]
    qy = each(lambda m_, w_, vs: _mm(m_, jnp.concatenate(
        [w_, jnp.concatenate([zeros_2c, vs], axis=1)], axis=0)), m_row, wu, v_st)

    ys = [[None] * npairs for _ in range(nchunks)]
    states = [s_ref[p] for p in range(npairs)]
    for i, (ci, p) in enumerate(units):
        s_old = states[p]
        q = lhs[i][C:] + qy[i][:, :LANES]
        ys[ci][p] = _mm_nt(q, s_old) + qy[i][:, LANES:]
        states[p] = s_old * blk(pc, (ci, p))[0:1, :] + _mm(s_old, mn[i][:LANES]) + mn[i][LANES:]
    for p in range(npairs):
        s_ref[p] = states[p]

    y = jnp.concatenate([jnp.concatenate(t, axis=1) for t in ys], axis=0)
    inv_n = 1.0 / RWKV_HEAD_DIM
    mean = segsum(y) * inv_n
    d = y - mean
    var = segsum(d * d) * inv_n
    yn = d * lax.rsqrt(var + GN_EPS) * lng_ref[...] + lnb_ref[...]
    bonus = segsum(r * kmod * rk_ref[...]) * v
    o_ref[...] = ((yn + bonus) * g).astype(o_ref.dtype)


def _rwkv_cum_matrix(rows):
    t = jnp.arange(rows)
    same = (t[:, None] // RWKV_CHUNK) == (t[None, :] // RWKV_CHUNK)
    return jnp.concatenate([same & (t[None, :] <= t[:, None]), same], axis=0).astype(BF16)


def _rwkv(crw3, mu, wl, w0, a0, k_k, k_a, r_k, ln_g, ln_b, ones_bd, cum_mat):
    b, s, _ = crw3.shape
    R = cum_mat.shape[1]
    W = RWKV_WIDTH
    vec = lambda n: pl.BlockSpec((1, n), lambda i, j: (0, 0))
    return pl.pallas_call(
        _rwkv_kernel,
        grid=(b, s // R),
        in_specs=[pl.BlockSpec((None, R, RWKV_COLS), lambda i, j: (i, j, 0)),
                  vec(RWKV_COLS),
                  pl.BlockSpec((LORA_COLS, 3 * W), lambda i, j: (0, 0)),
                  vec(W), vec(W), vec(W), vec(W), vec(W), vec(W), vec(W),
                  pl.BlockSpec((LANES, LANES), lambda i, j: (0, 0)),
                  pl.BlockSpec((2 * R, R), lambda i, j: (0, 0))],
        out_specs=pl.BlockSpec((None, R, W), lambda i, j: (i, j, 0)),
        out_shape=jax.ShapeDtypeStruct((b, s, W), BF16),
        scratch_shapes=[pltpu.VMEM((1, RWKV_COLS), F32),
                        pltpu.VMEM((W // LANES, LANES, LANES), F32)],
        compiler_params=pltpu.CompilerParams(dimension_semantics=("arbitrary", "arbitrary"),
                                             vmem_limit_bytes=VMEM_LIMIT),
        name="rwkv",
    )(crw3, mu, wl, w0, a0, k_k, k_a, r_k, ln_g, ln_b, ones_bd, cum_mat)


def _s5_expand_kernel(are_ref, aim_ref, kf_ref, of_ref, wa_ref, wo_ref, *, groups, state, chans):
    rows = are_ref.shape[0]
    base = pl.program_id(0) * rows
    r = base + lax.broadcasted_iota(jnp.int32, (rows, LANES), 0)
    lane = lax.broadcasted_iota(jnp.int32, (rows, LANES), 1)
    in_group = (r // chans) % groups
    st_group = (r // state) % groups

    def emit(dst_ref, col0, src, row_group, per_group):
        per_tile = LANES // per_group
        tiles = groups // per_tile
        for m in range(src.shape[1] // LANES):
            vals = src[:, m * LANES:(m + 1) * LANES]
            for tl in range(tiles):
                c = col0 + (m * tiles + tl) * LANES
                keep = row_group == tl * per_tile + lane // per_group
                dst_ref[:, c:c + LANES] = jnp.where(keep, vals, 0.0).astype(dst_ref.dtype)

    half = groups * state
    emit(wa_ref, 0, are_ref[...], in_group, state)
    emit(wa_ref, half, aim_ref[...], in_group, state)
    emit(wa_ref, 2 * half, kf_ref[...], in_group, chans)
    emit(wo_ref, 0, of_ref[...], st_group, chans)


def _s5_expand(a_re, a_im, kf, of, groups, state, chans):
    rows = a_re.shape[0]
    assert kf.shape[0] == rows and of.shape[0] == rows
    tr = _row_tile(rows, 256)
    n_a = 2 * groups * state + kf.shape[1] // LANES * groups * chans
    n_o = of.shape[1] // LANES * groups * chans
    blk = lambda a: pl.BlockSpec((tr, a.shape[1]), lambda i: (i, 0))
    return pl.pallas_call(
        functools.partial(_s5_expand_kernel, groups=groups, state=state, chans=chans),
        grid=(rows // tr,),
        in_specs=[blk(a_re), blk(a_im), blk(kf), blk(of)],
        out_specs=[pl.BlockSpec((tr, n_a), lambda i: (i, 0)), pl.BlockSpec((tr, n_o), lambda i: (i, 0))],
        out_shape=[jax.ShapeDtypeStruct((rows, n_a), BF16), jax.ShapeDtypeStruct((rows, n_o), BF16)],
        compiler_params=pltpu.CompilerParams(dimension_semantics=("arbitrary",),
                                             vmem_limit_bytes=VMEM_LIMIT),
        name="s5_expand",
    )(a_re, a_im, kf, of)


def _s5_mats(lam_re, lam_im, log_dt, b_re, b_im, c_re, c_im, d_skip):
    L = S5_CHUNK
    G, N = lam_re.shape
    ch = b_re.shape[-1]
    dt = jnp.exp(log_dt)[:, None]
    lr, li = lam_re, lam_im
    mag = jnp.exp(lr * dt)
    lb_re, lb_im = mag * jnp.cos(li * dt), mag * jnp.sin(li * dt)
    den = lr * lr + li * li
    nr, ni = lb_re - 1.0, lb_im
    coef_re = (nr * lr + ni * li) / den
    coef_im = (ni * lr - nr * li) / den
    bb_re = coef_re[..., None] * b_re - coef_im[..., None] * b_im
    bb_im = coef_re[..., None] * b_im + coef_im[..., None] * b_re
    prs, pis = [jnp.ones_like(lb_re)], [jnp.zeros_like(lb_im)]
    for _ in range(L):
        pr_, pi_ = prs[-1], pis[-1]
        prs.append(pr_ * lb_re - pi_ * lb_im)
        pis.append(pr_ * lb_im + pi_ * lb_re)
    pr = jnp.stack(prs)
    pi = jnp.stack(pis)
    hp = lax.Precision.HIGHEST

    def lam_bb(qr, qi):
        return (qr[..., None] * bb_re[None] - qi[..., None] * bb_im[None],
                qr[..., None] * bb_im[None] + qi[..., None] * bb_re[None])

    def repeat_cols(a, inner, reps):
        k = a.shape[-1]
        src = jnp.arange(k)[:, None]
        dst = jnp.arange(k * reps)[None, :]
        rep = ((src // inner == dst // (inner * reps)) & (src % inner == dst % inner)).astype(F32)
        return jnp.dot(a, rep, precision=hp)

    in_rows = L * G * ch
    st_rows = 2 * G * N
    wre, wim = lam_bb(pr[:L][::-1], pi[:L][::-1])
    a_in = [repeat_cols(jnp.swapaxes(part, 2, 3).reshape(in_rows, N), N, LANES // N)
            for part in (wre, wim)]
    lre, lim = lam_bb(pr[:L], pi[:L])
    kern = (jnp.einsum('gon,lgni->lgio', c_re, lre, precision=hp)
            - jnp.einsum('gon,lgni->lgio', c_im, lim, precision=hp))
    kern = kern.at[0].add(d_skip[:, :, None] * jnp.eye(ch, dtype=F32)[None])
    lag = jnp.arange(L)[None, :] - jnp.arange(L)[:, None]
    kst = jnp.where((lag >= 0)[:, :, None, None, None], kern[jnp.maximum(lag, 0)], 0.0)
    kf = jnp.transpose(kst, (0, 2, 3, 1, 4)).reshape(in_rows, L * ch)
    kf = repeat_cols(kf, ch, LANES // ch)
    qr, qi = pr[1:L + 1][:, :, None, :], pi[1:L + 1][:, :, None, :]
    o_re = c_re[None] * qr - c_im[None] * qi
    o_im = -c_re[None] * qi - c_im[None] * qr
    of = jnp.transpose(jnp.stack([o_re, o_im]), (0, 2, 4, 1, 3))
    of = repeat_cols(of.reshape(st_rows, L * ch), ch, LANES // ch)
    w_a, w_out_flat = _s5_expand(a_in[0], a_in[1], kf, of, G, N, ch)
    plr = pr[L].reshape(1, G * N)
    pli = pi[L].reshape(1, G * N)
    return w_a, w_out_flat, plr, pli


def _s5_kernel(u_ref, wa_ref, wo_ref, plr_ref, pli_ref, o_ref, wloc_ref, zprev_ref):
    nch, L, w = u_ref.shape
    half = S5_ZW // 2

    r = _mm(u_ref[:, 0, :], wa_ref[0:w, :])
    for j in range(1, L):
        r = r + _mm(u_ref[:, j, :], wa_ref[j * w:(j + 1) * w, :])
    wloc_ref[...] = r[:, :S5_ZW]
    y_lag = r[:, S5_ZW:]

    plr = plr_ref[...]
    pli = pli_ref[...]

    def step(ci, z):
        zprev_ref[pl.ds(ci, 1), :] = z
        zr, zi = z[:, :half], z[:, half:]
        nz = jnp.concatenate([plr * zr - pli * zi, plr * zi + pli * zr], axis=1)
        return nz + wloc_ref[pl.ds(ci, 1), :]

    lax.fori_loop(0, nch, step, jnp.zeros((1, S5_ZW), F32))
    y = y_lag + _mm(zprev_ref[...], wo_ref[...])
    for j in range(L):
        o_ref[:, j, :] = y[:, j * w:(j + 1) * w]


def _s5(u4, w_a, w_o, plr, pli):
    b, nch, L, w = u4.shape
    const = lambda a: pl.BlockSpec(a.shape, lambda i: (0,) * a.ndim, pipeline_mode=pl.Buffered(1))
    blk = pl.BlockSpec((None, nch, L, w), lambda i: (i, 0, 0, 0))
    return pl.pallas_call(
        _s5_kernel,
        grid=(b,),
        in_specs=[blk, const(w_a), const(w_o), const(plr), const(pli)],
        out_specs=blk,
        out_shape=jax.ShapeDtypeStruct(u4.shape, F32),
        scratch_shapes=[pltpu.VMEM((nch, S5_ZW), F32),
                        pltpu.VMEM((nch, S5_ZW), F32)],
        compiler_params=pltpu.CompilerParams(dimension_semantics=("arbitrary",),
                                             vmem_limit_bytes=VMEM_LIMIT),
        name="s5",
    )(u4, w_a, w_o, plr, pli)


def _merge_tile_math(x_ref, ya_ref, yb_ref, gates_ref, gluw_ref, glub_ref, wba_ref, wbb_ref, wout_ref,
                     fng_ref, rw_ref, rb_ref, tri_ref, upper_ref, x1_ref, t_ref, rt_ref, pos_ref,
                     te_ref, fill_ref, nused_ref, cnt_ref, cur_ref):
    d = x_ref.shape[1]
    tm = x_ref.shape[0]
    y_a = jnp.dot(ya_ref[...], wba_ref[...], preferred_element_type=F32)
    ys = yb_ref[...]
    z = 0.5 * ys * (1.0 + jnp.tanh(math.sqrt(2.0 / math.pi) * (ys + 0.044715 * (ys * ys * ys))))
    z = z * _sigmoid(_mm(z, gluw_ref[...]) + glub_ref[...])
    y_b = _mm(z, wbb_ref[...])
    gates = gates_ref[...].astype(F32)
    merged = _sigmoid(gates[:, :d]) * y_a + _sigmoid(gates[:, d:]) * y_b
    x1 = x_ref[...] + _mm(merged, wout_ref[...])
    x1_ref[...] = x1
    t = _rms_norm(x1, fng_ref[...])
    t_hi = t.astype(BF16)
    t_ref[...] = t

    t_lo = (t - t_hi.astype(F32)).astype(BF16)
    hh_hl = jnp.dot(t_hi, rw_ref[...], preferred_element_type=F32)
    lh = jnp.dot(t_lo, rw_ref[:, :ROUTER_LANES], preferred_element_type=F32)
    logits = hh_hl[:, :ROUTER_LANES] + hh_hl[:, ROUTER_LANES:] + lh + rb_ref[...]
    lane = lax.broadcasted_iota(jnp.int32, logits.shape, 1)
    neg = -jnp.inf
    lane_f = lane.astype(F32)
    big = float(1 << 20)
    is_g = (lane >= N_EXPERTS) & (lane < N_EXPERTS + N_GROUPS)
    gl = jnp.where(is_g, logits, neg)
    gmax = jnp.max(gl, axis=-1, keepdims=True)
    g_p = 1.0 / jnp.sum(jnp.exp(gl - gmax), axis=-1, keepdims=True)
    g_idx = jnp.min(jnp.where(gl == gmax, lane_f - N_EXPERTS, big), axis=-1,
                    keepdims=True).astype(jnp.int32)
    el = jnp.where((lane < N_EXPERTS) & ((lane >> int(math.log2(EXPERTS_PER_GROUP))) == g_idx), logits, neg)
    t1 = jnp.max(el, axis=-1, keepdims=True)
    i1 = jnp.min(jnp.where(el == t1, lane_f, big), axis=-1, keepdims=True).astype(jnp.int32)
    el2 = jnp.where(lane == i1, neg, el)
    t2 = jnp.max(el2, axis=-1, keepdims=True)
    i2 = jnp.min(jnp.where(el2 == t2, lane_f, big), axis=-1, keepdims=True).astype(jnp.int32)
    e21 = jnp.exp(t2 - t1)
    w1 = g_p / (1.0 + e21)
    w2 = g_p * e21 / (1.0 + e21)
    rt_ref[...] = jnp.where(lane == 0, w1, jnp.where(lane == 1, w2, 0.0))

    sh = int(math.log2(MOE_TILE))
    oh1 = lane == i1
    oh2 = lane == i2
    ind = jnp.where(oh1 | oh2, 1.0, 0.0)
    lrank = jnp.dot(tri_ref[...], ind.astype(BF16), preferred_element_type=F32).astype(jnp.int32)
    n_new = lrank[tm - 1:tm, :] + ind[tm - 1:tm, :].astype(jnp.int32)
    cnt = cnt_ref[...]
    cur = cur_ref[...]
    nfree = nused_ref[...]
    tiles_before = (cnt + (MOE_TILE - 1)) >> sh
    newf = ((cnt + n_new + (MOE_TILE - 1)) >> sh) - tiles_before
    newf8 = jnp.broadcast_to(newf.astype(BF16), (8, newf.shape[1]))
    pre = jnp.dot(newf8, upper_ref[...], preferred_element_type=F32)[0:1, :].astype(jnp.int32)
    new_tile = nfree + pre
    grank = cnt + lrank
    ptile = jnp.where((grank >> sh) < tiles_before, cur, new_tile)
    posfull = ((ptile << sh) + (grank & (MOE_TILE - 1))).astype(F32)
    pos1 = jnp.sum(jnp.where(oh1, posfull, 0.0), axis=-1, keepdims=True)
    pos2 = jnp.sum(jnp.where(oh2, posfull, 0.0), axis=-1, keepdims=True)
    tr = lax.broadcasted_iota(jnp.int32, (tm, tm), 0)
    tc = lax.broadcasted_iota(jnp.int32, (tm, tm), 1)
    as_row = lambda v: jnp.sum(jnp.where(tr == tc, jnp.broadcast_to(v, (tm, tm)), 0.0),
                               axis=0, keepdims=True)
    sub = lax.broadcasted_iota(jnp.int32, pos_ref.shape, 0)
    pos_ref[...] = jnp.where(sub == 0, as_row(pos1),
                             jnp.where(sub == 1, as_row(pos2), 0.0)).astype(jnp.int32)

    el_r = lax.broadcasted_iota(jnp.int32, (ROUTER_LANES, ROUTER_LANES), 0)
    el_c = lax.broadcasted_iota(jnp.int32, (ROUTER_LANES, ROUTER_LANES), 1)
    as_col = lambda v: jnp.sum(jnp.where(el_r == el_c, jnp.broadcast_to(v, el_r.shape), 0.0),
                               axis=1, keepdims=True)
    tile_col = as_col(jnp.where(newf > 0, new_tile, -1).astype(F32)).astype(jnp.int32)
    tlane = lax.broadcasted_iota(jnp.int32, (ROUTER_LANES, te_ref.shape[1]), 1)
    erow = lax.broadcasted_iota(jnp.int32, (ROUTER_LANES, te_ref.shape[1]), 0).astype(F32)
    te_ref[...] += jnp.sum(jnp.where(tlane == tile_col, erow, 0.0), axis=0,
                           keepdims=True).astype(jnp.int32)
    to_cur = jnp.minimum(n_new, (tiles_before << sh) - cnt)
    cur_col = as_col(cur.astype(F32)).astype(jnp.int32)
    added = (jnp.where(tlane == cur_col, as_col(to_cur.astype(F32)), 0.0)
             + jnp.where(tlane == tile_col, as_col((n_new - to_cur).astype(F32)), 0.0))
    fill_ref[...] += jnp.sum(added, axis=0, keepdims=True).astype(jnp.int32)
    cnt_ref[...] = cnt + n_new
    cur_ref[...] = jnp.where(newf > 0, new_tile, cur)
    nused_ref[...] = nfree + jnp.sum(newf.astype(F32), axis=-1, keepdims=True).astype(jnp.int32)


def _merge_kernel(x_ref, ya_ref, yb_ref, gates_ref, gluw_ref, glub_ref, wba_ref, wbb_ref, wout_ref,
                  fng_ref, rw_ref, rb_ref, tri_ref, upper_ref, x1_ref, rt_ref, pos_ref,
                  te_ref, fill_ref, nused_ref, xs_ref, cnt_ref, cur_ref, tbuf, posv, pos_smem, fill_smem,
                  zblk, rsem, psem, zsem):
    i = pl.program_id(0)
    last = pl.num_programs(0) - 1
    tm = x_ref.shape[0]

    @pl.when(i == 0)
    def _():
        cnt_ref[...] = jnp.zeros_like(cnt_ref)
        cur_ref[...] = jnp.zeros_like(cur_ref)
        te_ref[...] = jnp.zeros_like(te_ref)
        fill_ref[...] = jnp.zeros_like(fill_ref)
        nused_ref[...] = jnp.zeros_like(nused_ref)

    def row_copies(tile, unroll, op):
        slot = tile % 2

        def body(r, carry):
            for k in range(2):
                op(pltpu.make_async_copy(tbuf.at[slot, pl.ds(r, 1), :],
                                         xs_ref.at[pl.ds(pos_smem[slot, k, r], 1), :], rsem.at[slot]))
            return carry
        lax.fori_loop(0, tm, body, 0, unroll=unroll)

    def drain_rows(tile):
        slot = tile % 2
        for _ in range(2):
            pltpu.make_async_copy(tbuf.at[slot], xs_ref.at[pl.ds(0, tm), :], rsem.at[slot]).wait()

    def pos_to_smem(tile):
        return pltpu.make_async_copy(posv, pos_smem.at[pl.ds(tile % 2, 1)], psem)

    @pl.when(i >= 1)
    def _():
        pos_to_smem(i - 1).wait()

    @pl.when(i >= 2)
    def _():
        drain_rows(i - 2)

    def step(dispatch_prev):
        if dispatch_prev:
            row_copies(i - 1, True, lambda c: c.start())
        _merge_tile_math(x_ref, ya_ref, yb_ref, gates_ref, gluw_ref, glub_ref, wba_ref, wbb_ref,
                         wout_ref, fng_ref, rw_ref, rb_ref, tri_ref, upper_ref, x1_ref,
                         tbuf.at[i % 2], rt_ref, pos_ref.at[0], te_ref, fill_ref, nused_ref, cnt_ref,
                         cur_ref)
        posv[...] = pos_ref[...]

    pl.when(i == 0)(lambda: step(False))
    pl.when(i > 0)(lambda: step(True))
    pos_to_smem(i).start()

    @pl.when(i == last)
    def _():
        pos_to_smem(i).wait()
        row_copies(i, DMA_UNROLL, lambda c: c.start())

        @pl.when(i >= 1)
        def _():
            drain_rows(i - 1)
        drain_rows(i)

        fill_copy = pltpu.make_async_copy(fill_ref, fill_smem, psem)
        fill_copy.start()
        zblk[...] = jnp.zeros_like(zblk)
        fill_copy.wait()

        def over_tails(op):
            def tile(j, carry):
                fill = fill_smem[0, j]
                head = (-fill) & (SUBLANES - 1)
                for r in range(SUBLANES - 1):
                    @pl.when(r < head)
                    def _(r=r):
                        op(pltpu.make_async_copy(zblk.at[pl.ds(0, 1), :],
                                                 xs_ref.at[pl.ds(j * MOE_TILE + fill + r, 1), :], zsem))
                start = fill + head
                todo = MOE_TILE - start
                size = MOE_TILE
                while size >= SUBLANES:
                    @pl.when((todo & size) != 0)
                    def _(start=start, size=size):
                        off = pl.multiple_of(j * MOE_TILE + start, SUBLANES)
                        op(pltpu.make_async_copy(zblk.at[pl.ds(0, size), :],
                                                 xs_ref.at[pl.ds(off, size), :], zsem))
                    start = start + (todo & size)
                    size //= 2
                return carry
            lax.fori_loop(0, xs_ref.shape[0] // MOE_TILE, tile, 0)

        over_tails(lambda c: c.start())
        over_tails(lambda c: c.wait())


def _moe_tiles(t):
    return (2 * t) // MOE_TILE + N_EXPERTS


def _merge(x2, ya, yb, gates, gluw, glub, wba, wbb, wout, fng, rw, rb):
    t, d = x2.shape
    tm = MERGE_TILE
    assert t % tm == 0 and tm <= MOE_TILE
    n_tiles = _moe_tiles(t)
    te_lanes = -(-n_tiles // LANES) * LANES
    rr = jnp.arange(tm)
    tri = (rr[None, :] < rr[:, None]).astype(BF16)
    ll = jnp.arange(ROUTER_LANES)
    upper = (ll[:, None] < ll[None, :]).astype(BF16)
    full = lambda a: pl.BlockSpec(a.shape, lambda i: (0,) * a.ndim)
    rowblk = lambda n: pl.BlockSpec((tm, n), lambda i: (i, 0))
    fixed = lambda n: pl.BlockSpec((1, n), lambda i: (0, 0))
    return pl.pallas_call(
        _merge_kernel,
        grid=(t // tm,),
        in_specs=[rowblk(d), rowblk(ya.shape[1]), rowblk(yb.shape[1]), rowblk(gates.shape[1]),
                  full(gluw), full(glub), full(wba), full(wbb), full(wout), full(fng), full(rw),
                  full(rb), full(tri), full(upper)],
        out_specs=[rowblk(d), rowblk(ROUTER_LANES),
                   pl.BlockSpec((1, SUBLANES, tm), lambda i: (i, 0, 0)),
                   fixed(te_lanes), fixed(te_lanes), fixed(ROUTER_LANES),
                   pl.BlockSpec(memory_space=pl.ANY)],
        out_shape=[jax.ShapeDtypeStruct((t, d), F32),
                   jax.ShapeDtypeStruct((t, ROUTER_LANES), F32),
                   jax.ShapeDtypeStruct((t // tm, SUBLANES, tm), jnp.int32),
                   jax.ShapeDtypeStruct((1, te_lanes), jnp.int32),
                   jax.ShapeDtypeStruct((1, te_lanes), jnp.int32),
                   jax.ShapeDtypeStruct((1, ROUTER_LANES), jnp.int32),
                   jax.ShapeDtypeStruct((n_tiles * MOE_TILE, d), F32)],
        scratch_shapes=[pltpu.VMEM((1, ROUTER_LANES), jnp.int32),
                        pltpu.VMEM((1, ROUTER_LANES), jnp.int32),
                        pltpu.VMEM((2, tm, d), F32),
                        pltpu.VMEM((1, SUBLANES, tm), jnp.int32),
                        pltpu.SMEM((2, SUBLANES, tm), jnp.int32),
                        pltpu.SMEM((1, te_lanes), jnp.int32),
                        pltpu.VMEM((MOE_TILE, d), F32),
                        pltpu.SemaphoreType.DMA((2,)),
                        pltpu.SemaphoreType.DMA(()),
                        pltpu.SemaphoreType.DMA(())],
        compiler_params=pltpu.CompilerParams(dimension_semantics=("arbitrary",),
                                             vmem_limit_bytes=VMEM_LIMIT),
        name="merge",
    )(x2, ya, yb, gates, gluw, glub, wba, wbb, wout, fng, rw, rb, tri, upper)


def _experts_kernel(te_ref, nused_ref, xs_ref, wg_ref, wu_ref, wd_ref, y_ref):
    j = pl.program_id(0)

    @pl.when(j < nused_ref[0])
    def _():
        x = xs_ref[...].astype(BF16)
        hg = _mm(x, wg_ref[...])
        hid = hg * _sigmoid(hg) * _mm(x, wu_ref[...])
        y_ref[...] = _mm(hid, wd_ref[...])

    @pl.when(j >= nused_ref[0])
    def _():
        y_ref[...] = jnp.zeros_like(y_ref)


def _experts(tile_expert, n_used, xs, wg, wu, wd):
    n_rows, hw = xs.shape
    ne, d, de = wg.shape
    return pl.pallas_call(
        _experts_kernel,
        grid_spec=pltpu.PrefetchScalarGridSpec(
            num_scalar_prefetch=2,
            grid=(n_rows // MOE_TILE,),
            in_specs=[pl.BlockSpec((MOE_TILE, hw), lambda j, te, nu: (j, 0)),
                      pl.BlockSpec((None, d, de), lambda j, te, nu: (te[j], 0, 0)),
                      pl.BlockSpec((None, d, de), lambda j, te, nu: (te[j], 0, 0)),
                      pl.BlockSpec((None, de, d), lambda j, te, nu: (te[j], 0, 0))],
            out_specs=pl.BlockSpec((MOE_TILE, hw), lambda j, te, nu: (j, 0))),
        out_shape=jax.ShapeDtypeStruct((n_rows, hw), F32),
        compiler_params=pltpu.CompilerParams(dimension_semantics=("arbitrary",),
                                             vmem_limit_bytes=VMEM_LIMIT),
        name="experts",
    )(tile_expert, n_used, xs, wg, wu, wd)


def _ple_kernel(pos1_ref, pos2_ref, x_ref, rt_ref, p_ref, png_ref, wg_ref, wp_ref, fng_ref, y_ref,
                o_ref, ybuf, sem):
    i = pl.program_id(0)
    tm = x_ref.shape[0]

    def gather(tile, unroll):
        slot = tile % 2

        def body(r, carry):
            for k, pos_ref in enumerate((pos1_ref, pos2_ref)):
                pltpu.make_async_copy(y_ref.at[pl.ds(pos_ref[tile * tm + r], 1), :],
                                      ybuf.at[slot, k, pl.ds(r, 1), :], sem.at[slot]).start()
            return carry
        lax.fori_loop(0, tm, body, 0, unroll=unroll)

    @pl.when(i == 0)
    def _():
        gather(i, DMA_UNROLL)

    def step(prefetch_next):
        slot = i % 2
        for k in range(2):
            pltpu.make_async_copy(y_ref.at[pl.ds(0, tm), :], ybuf.at[slot, k], sem.at[slot]).wait()
        if prefetch_next:
            gather(i + 1, True)
        rt = rt_ref[...]
        x2 = x_ref[...] + rt[:, 0:1] * ybuf[slot, 0] + rt[:, 1:2] * ybuf[slot, 1]
        hp = _rms_norm(x2, png_ref[...])
        gate = _sigmoid(_mm(hp, wg_ref[...]))
        x3 = x2 + gate * _mm(p_ref[...], wp_ref[...])
        o_ref[...] = _rms_norm(x3, fng_ref[...])

    last = pl.num_programs(0) - 1
    pl.when(i < last)(lambda: step(True))
    pl.when(i == last)(lambda: step(False))


def _ple(pos1, pos2, x1, rt, p2, png, wg, wp, fng, y_pack, tm):
    t, d = x1.shape
    hw = y_pack.shape[1]
    full = lambda a: pl.BlockSpec(a.shape, lambda i, p1, p2_: (0,) * a.ndim)
    rowblk = lambda n: pl.BlockSpec((tm, n), lambda i, p1, p2_: (i, 0))
    return pl.pallas_call(
        _ple_kernel,
        grid_spec=pltpu.PrefetchScalarGridSpec(
            num_scalar_prefetch=2,
            grid=(t // tm,),
            in_specs=[rowblk(d), rowblk(rt.shape[1]), rowblk(p2.shape[1]),
                      full(png), full(wg), full(wp), full(fng),
                      pl.BlockSpec(memory_space=pl.ANY)],
            out_specs=rowblk(d),
            scratch_shapes=[pltpu.VMEM((2, 2, tm, hw), F32),
                            pltpu.SemaphoreType.DMA((2,))]),
        out_shape=jax.ShapeDtypeStruct((t, d), F32),
        compiler_params=pltpu.CompilerParams(dimension_semantics=("arbitrary",),
                                             vmem_limit_bytes=VMEM_LIMIT),
        name="ple",
    )(pos1, pos2, x1, rt, p2, png, wg, wp, fng, y_pack)


def _row_tile(t, want):
    tm = min(want, t)
    while t % tm:
        tm //= 2
    return tm


def _layer(x, p, mix_norm, w_in, mu_shift, rk_w0, rk_w_up, rk_a0, rk_a_up, rk_g_up,
           rk_k_k, rk_k_a, rk_r_k, rk_ln_g, rk_ln_b, s5_lam_re, s5_lam_im, s5_log_dt,
           s5_b_re, s5_b_im, s5_c_re, s5_c_im, s5_d, s5_glu_w, s5_glu_b,
           w_branch_a, w_branch_b, w_out, ffn_norm, router_group_w, router_group_b,
           router_expert_w, router_expert_b, exp_w_gate, exp_w_up, exp_w_down,
           ple_norm, ple_gate_w, ple_proj):
    b, s, d = x.shape
    t = b * s
    W = RWKV_WIDTH
    row = lambda a: a.reshape(1, -1).astype(F32)
    x2 = x.reshape(t, d)

    crw, us5, gates = _in_proj(x2, row(mix_norm), w_in.astype(BF16), _row_tile(t, 512))

    wl = jnp.zeros((LORA_COLS, 3 * W), F32)
    wl = wl.at[:DECAY_LORA, :W].set(rk_w_up)
    wl = wl.at[DECAY_LORA:DECAY_LORA + AAA_LORA, W:2 * W].set(rk_a_up)
    wl = wl.at[DECAY_LORA + AAA_LORA:, 2 * W:].set(rk_g_up)
    hid = jnp.arange(LANES) // RWKV_HEAD_DIM
    ones_bd = (hid[:, None] == hid[None, :]).astype(BF16)
    ya = _rwkv(crw.reshape(b, s, RWKV_COLS), row(mu_shift), wl.astype(BF16), row(rk_w0), row(rk_a0),
               row(rk_k_k), row(rk_k_a), row(rk_r_k), row(rk_ln_g), row(rk_ln_b), ones_bd,
               _rwkv_cum_matrix(_row_tile(s, RWKV_BLOCK)))

    s5_wa, s5_wo, plr, pli = _s5_mats(s5_lam_re, s5_lam_im, s5_log_dt, s5_b_re, s5_b_im,
                                      s5_c_re, s5_c_im, s5_d)
    yb = _s5(us5.reshape(b, s // S5_CHUNK, S5_CHUNK, S5_WIDTH), s5_wa, s5_wo, plr, pli)

    rw = jnp.zeros((d, ROUTER_LANES), F32)
    rw = rw.at[:, :N_EXPERTS].set(router_expert_w).at[:, N_EXPERTS:N_EXPERTS + N_GROUPS].set(router_group_w)
    rb = jnp.zeros((1, ROUTER_LANES), F32)
    rb = rb.at[0, :N_EXPERTS].set(router_expert_b).at[0, N_EXPERTS:N_EXPERTS + N_GROUPS].set(router_group_b)
    rw_hi = rw.astype(BF16)
    rw = jnp.concatenate([rw_hi, (rw - rw_hi.astype(F32)).astype(BF16)], axis=1)
    x1, rt, pos, tile_expert, _, n_used, xs = _merge(
        x2, ya.reshape(t, W), yb.reshape(t, S5_WIDTH), gates, s5_glu_w.astype(BF16), row(s5_glu_b),
        w_branch_a.astype(BF16), w_branch_b.astype(BF16), w_out.astype(BF16), row(ffn_norm), rw, rb)
    n_tiles = _moe_tiles(t)
    pos1, pos2 = pos[:, 0, :].reshape(t), pos[:, 1, :].reshape(t)
    y_pack = _experts(tile_expert[0, :n_tiles], n_used[0, :1], xs, exp_w_gate, exp_w_up, exp_w_down)
    return (pos1, pos2, x1, rt, p.reshape(t, -1), row(ple_norm), ple_gate_w.astype(BF16),
            ple_proj.astype(BF16), y_pack)


def kernel(x, p, mix_norm, w_in, mu_shift, rk_w0, rk_w_up, rk_a0, rk_a_up, rk_g_up, rk_k_k, rk_k_a,
           rk_r_k, rk_ln_g, rk_ln_b, s5_lam_re, s5_lam_im, s5_log_dt, s5_b_re, s5_b_im, s5_c_re,
           s5_c_im, s5_d, s5_glu_w, s5_glu_b, w_branch_a, w_branch_b, w_out, ffn_norm,
           router_group_w, router_group_b, router_expert_w, router_expert_b, exp_w_gate, exp_w_up,
           exp_w_down, ple_norm, ple_gate_w, ple_proj, final_norm):
    b, s, d = x.shape
    depth = w_in.shape[0]
    assert depth == 1, "the final norm is fused into the last layer's PLE kernel"
    i = 0
    pos1, pos2, x1, rt, p2, png, wpg, wpp, y_pack = _layer(
        x, p[i], mix_norm[i], w_in[i], mu_shift[i], rk_w0[i], rk_w_up[i], rk_a0[i], rk_a_up[i],
        rk_g_up[i], rk_k_k[i], rk_k_a[i], rk_r_k[i], rk_ln_g[i], rk_ln_b[i], s5_lam_re[i],
        s5_lam_im[i], s5_log_dt[i], s5_b_re[i], s5_b_im[i], s5_c_re[i], s5_c_im[i], s5_d[i],
        s5_glu_w[i], s5_glu_b[i], w_branch_a[i], w_branch_b[i], w_out[i], ffn_norm[i],
        router_group_w[i], router_group_b[i], router_expert_w[i], router_expert_b[i],
        exp_w_gate[i], exp_w_up[i], exp_w_down[i], ple_norm[i], ple_gate_w[i], ple_proj[i])
    out = _ple(pos1, pos2, x1, rt, p2, png, wpg, wpp, final_norm.reshape(1, -1).astype(F32), y_pack,
               _row_tile(b * s, MERGE_TILE))
    return out.reshape(b, s, d)
```

```python
import functools
import math

import jax
import jax.numpy as jnp
from jax import lax
from jax.experimental import pallas as pl
from jax.experimental.pallas import tpu as pltpu

F32 = jnp.float32
BF16 = jnp.bfloat16

NORM_EPS = 1e-6
GN_EPS = 64e-5

RWKV_HEADS = 8
RWKV_HEAD_DIM = 64
RWKV_WIDTH = RWKV_HEADS * RWKV_HEAD_DIM
DECAY_LORA = 64
AAA_LORA = 64
GATE_LORA = 128
LORA_COLS = DECAY_LORA + AAA_LORA + GATE_LORA
RWKV_COLS = 3 * RWKV_WIDTH + LORA_COLS
S5_GROUPS = 16
S5_GROUP_CH = 16
S5_WIDTH = S5_GROUPS * S5_GROUP_CH
S5_STATE = 64
S5_ZW = 2 * S5_GROUPS * S5_STATE
N_GROUPS = 4
EXPERTS_PER_GROUP = 8
N_EXPERTS = N_GROUPS * EXPERTS_PER_GROUP

LANES = 128
SUBLANES = 8
RWKV_CHUNK = 64
RWKV_INV_BLOCK = 16
RWKV_BLOCK = 256
S5_CHUNK = 8
ROUTER_LANES = 128
MOE_TILE = 512
MERGE_TILE = 256
DMA_UNROLL = 8
VMEM_LIMIT = 56 * 1024 * 1024


def _mm(a, b):
    return jnp.dot(a.astype(BF16), b.astype(BF16), preferred_element_type=F32)


def _mm_nt(a, b):
    return lax.dot_general(a.astype(BF16), b.astype(BF16), (((1,), (1,)), ((), ())),
                           preferred_element_type=F32)


def _mm_tn(a, b):
    return lax.dot_general(a.astype(BF16), b.astype(BF16), (((0,), (0,)), ((), ())),
                           preferred_element_type=F32)


def _split2(x):
    hi = x.astype(BF16)
    return hi, (x - hi.astype(F32)).astype(BF16)


def _mm_split_lhs(a_bf16, x):
    hi, lo = _split2(x)
    d = lambda t: jnp.dot(a_bf16, t, preferred_element_type=F32)
    return d(hi) + d(lo)


def _mm_split_rhs(x, b_bf16):
    hi, lo = _split2(x)
    d = lambda t: jnp.dot(t, b_bf16, preferred_element_type=F32)
    return d(hi) + d(lo)


def _sigmoid(x):
    return 0.5 * jnp.tanh(0.5 * x) + 0.5


def _rms_norm(x, g):
    ms = jnp.mean(x * x, axis=-1, keepdims=True)
    return x * lax.rsqrt(ms + NORM_EPS) * g


def _in_proj_kernel(x_ref, g_ref, w_ref, crw_ref, us5_ref, gates_ref):
    h = _rms_norm(x_ref[...], g_ref[...]).astype(BF16)
    c0, c1 = RWKV_COLS, RWKV_COLS + S5_WIDTH
    crw_ref[...] = jnp.dot(h, w_ref[:, :c0], preferred_element_type=F32)
    us5_ref[...] = jnp.dot(h, w_ref[:, c0:c1], preferred_element_type=F32)
    gates_ref[...] = jnp.dot(h, w_ref[:, c1:], preferred_element_type=F32).astype(BF16)


def _in_proj(x2, g, w, tm):
    t, d = x2.shape
    n = w.shape[1]
    ng = n - RWKV_COLS - S5_WIDTH
    return pl.pallas_call(
        _in_proj_kernel,
        grid=(t // tm,),
        in_specs=[pl.BlockSpec((tm, d), lambda i: (i, 0)),
                  pl.BlockSpec((1, d), lambda i: (0, 0)),
                  pl.BlockSpec((d, n), lambda i: (0, 0))],
        out_specs=[pl.BlockSpec((tm, RWKV_COLS), lambda i: (i, 0)),
                   pl.BlockSpec((tm, S5_WIDTH), lambda i: (i, 0)),
                   pl.BlockSpec((tm, ng), lambda i: (i, 0))],
        out_shape=[jax.ShapeDtypeStruct((t, RWKV_COLS), F32),
                   jax.ShapeDtypeStruct((t, S5_WIDTH), F32),
                   jax.ShapeDtypeStruct((t, ng), BF16)],
        compiler_params=pltpu.CompilerParams(dimension_semantics=("arbitrary",),
                                             vmem_limit_bytes=VMEM_LIMIT),
        name="in_proj",
    )(x2, g, w)


def _rwkv_kernel(c_ref, mu_ref, wl_ref, w0_ref, a0_ref, kk_ref, ka_ref, rk_ref, lng_ref, lnb_ref,
                 ones_ref, cum_ref, o_ref, carry_ref, s_ref):
    C = RWKV_CHUNK
    W = RWKV_WIDTH
    npairs = W // LANES

    @pl.when(pl.program_id(1) == 0)
    def _():
        carry_ref[...] = jnp.zeros_like(carry_ref)
        s_ref[...] = jnp.zeros_like(s_ref)

    c = c_ref[...]
    R = c.shape[0]
    nchunks = R // C
    row = lax.broadcasted_iota(jnp.int32, (R, 1), 0)
    prev = jnp.where(row == 0, carry_ref[...], pltpu.roll(c, 1, 0))
    carry_ref[...] = c[R - 1:R, :]
    cs = c + (prev - c) * mu_ref[...]

    r = cs[:, 0:W]
    k = cs[:, W:2 * W]
    v = cs[:, 2 * W:3 * W]
    lin = cs[:, 3 * W:]
    llane = lax.broadcasted_iota(jnp.int32, lin.shape, 1)
    lact = jnp.where(llane < DECAY_LORA, jnp.tanh(lin),
                     jnp.where(llane < DECAY_LORA + AAA_LORA, lin, _sigmoid(lin)))
    lo = _mm(lact, wl_ref[...])
    zw = -(w0_ref[...] + lo[:, 0:W])
    softplus = jnp.maximum(zw, 0.0) + jnp.log(1.0 + jnp.exp(-jnp.abs(zw)))
    ld = -jnp.exp(-softplus - 0.5)
    a = _sigmoid(a0_ref[...] + lo[:, W:2 * W])
    g = lo[:, 2 * W:3 * W]

    ones_bd = ones_ref[...]
    segsum = lambda t: jnp.concatenate(
        [_mm_split_rhs(t[:, p * LANES:(p + 1) * LANES], ones_bd) for p in range(npairs)], axis=1)
    kk = k * kk_ref[...]
    kkn = kk / jnp.maximum(jnp.sqrt(segsum(kk * kk)), 1e-12)
    kmod = k * (1.0 + (a - 1.0) * ka_ref[...])

    cums = _mm_split_lhs(cum_ref[...], ld)
    cum = cums[:R]
    tot = cums[R:]
    inv = jnp.exp(-cum)
    tail = jnp.exp(tot - cum)
    At = -kkn * jnp.exp(cum - ld)
    Rt = r * jnp.exp(cum)
    kka = kkn * a
    Bt = kka * inv
    Kt = kmod * inv
    Bend = kka * tail
    Kend = kmod * tail
    pc = jnp.exp(tot)

    lane = lax.broadcasted_iota(jnp.int32, (C, LANES), 1)
    h0 = lane < RWKV_HEAD_DIM
    split = lambda t: jnp.concatenate([jnp.where(h0, t, 0.0), jnp.where(h0, 0.0, t)], axis=0)
    grow = lax.broadcasted_iota(jnp.int32, (C, 4 * C), 0)
    gcol = lax.broadcasted_iota(jnp.int32, (C, 4 * C), 1) & (C - 1)
    r2 = lax.broadcasted_iota(jnp.int32, (2 * C, 2 * C), 0)
    c2 = lax.broadcasted_iota(jnp.int32, (2 * C, 2 * C), 1)
    eye = (r2 == c2).astype(F32)
    blk_shift = int(math.log2(RWKV_INV_BLOCK))
    same_blk = (r2 >> blk_shift) == (c2 >> blk_shift)
    same_head = (r2 < C) == (c2 < C)
    same_head2 = jnp.concatenate([same_head, same_head], axis=0)
    zeros_c = jnp.zeros((C, LANES), F32)
    zeros_2c = jnp.zeros((2 * C, LANES), F32)

    units = [(ci, p) for ci in range(nchunks) for p in range(npairs)]
    blk = lambda t, u: t[u[0] * C:(u[0] + 1) * C, u[1] * LANES:(u[1] + 1) * LANES]
    each = lambda f, *ls: [f(*xs) for xs in zip(*ls)]

    lhs = [jnp.concatenate([blk(At, u), blk(Rt, u)], axis=0) for u in units]
    rhs = [jnp.concatenate([split(blk(Bt, u)), split(blk(Kt, u))], axis=0) for u in units]
    G = each(_mm_nt, lhs, rhs)
    a_row = [jnp.where(gcol < grow, t[:C], 0.0) for t in G]
    m_row = [jnp.where(gcol <= grow, t[C:], 0.0) for t in G]
    a_bd = [split(t[:, :2 * C]) for t in a_row]

    a_d = [jnp.where(same_blk, t, 0.0) for t in a_bd]
    a_off = each(lambda x, y: x - y, a_bd, a_d)
    dinv = [eye + t for t in a_d]
    pw = a_d
    for _ in range(blk_shift - 1):
        pw = each(_mm, pw, pw)
        dinv = each(lambda x, y: x + _mm(x, y), dinv, pw)
    n1 = each(_mm, dinv, a_off)
    n2 = each(_mm, n1, n1)
    n3 = each(_mm, n1, n2)
    tinv = each(lambda x1, x2, x3, dv: _mm(eye + x1 + x2 + x3, dv), n1, n2, n3, dinv)

    vp = [blk(v, u) for u in units]
    v_st = [split(t) for t in vp]
    rhs0 = each(lambda ar, vs: _mm(ar[:, 2 * C:], vs), a_row, v_st)
    wu = each(lambda t, l, r0: _mm(t, jnp.concatenate([split(l[:C]), split(r0)], axis=1)),
              tinv, lhs, rhs0)
    wu_lp = [t[:C] + t[C:] for t in wu]
    mn = each(lambda w_, v_, u: _mm_tn(
        jnp.concatenate([w_, jnp.concatenate([zeros_c, v_], axis=1)], axis=0),
        jnp.concatenate([blk(Bend, u), blk(Kend, u)], axis=0)), wu_lp, vp, units)
    mn = [jnp.where(same_head2, t, 0.0) for t in mn]
    qy = each(lambda m_, w_, vs: _mm(m_, jnp.concatenate(
        [w_, jnp.concatenate([zeros_2c, vs], axis=1)], axis=0)), m_row, wu, v_st)

    ys = [[None] * npairs for _ in range(nchunks)]
    states = [s_ref[p] for p in range(npairs)]
    for i, (ci, p) in enumerate(units):
        s_old = states[p]
        q = lhs[i][C:] + qy[i][:, :LANES]
        ys[ci][p] = _mm_nt(q, s_old) + qy[i][:, LANES:]
        states[p] = s_old * blk(pc, (ci, p))[0:1, :] + _mm(s_old, mn[i][:LANES]) + mn[i][LANES:]
    for p in range(npairs):
        s_ref[p] = states[p]

    y = jnp.concatenate([jnp.concatenate(t, axis=1) for t in ys], axis=0)
    inv_n = 1.0 / RWKV_HEAD_DIM
    mean = segsum(y) * inv_n
    d = y - mean
    var = segsum(d * d) * inv_n
    yn = d * lax.rsqrt(var + GN_EPS) * lng_ref[...] + lnb_ref[...]
    bonus = segsum(r * kmod * rk_ref[...]) * v
    o_ref[...] = ((yn + bonus) * g).astype(o_ref.dtype)


def _rwkv_cum_matrix(rows):
    t = jnp.arange(rows)
    same = (t[:, None] // RWKV_CHUNK) == (t[None, :] // RWKV_CHUNK)
    return jnp.concatenate([same & (t[None, :] <= t[:, None]), same], axis=0).astype(BF16)


def _rwkv(crw3, mu, wl, w0, a0, k_k, k_a, r_k, ln_g, ln_b, ones_bd, cum_mat):
    b, s, _ = crw3.shape
    R = cum_mat.shape[1]
    W = RWKV_WIDTH
    vec = lambda n: pl.BlockSpec((1, n), lambda i, j: (0, 0))
    return pl.pallas_call(
        _rwkv_kernel,
        grid=(b, s // R),
        in_specs=[pl.BlockSpec((None, R, RWKV_COLS), lambda i, j: (i, j, 0)),
                  vec(RWKV_COLS),
                  pl.BlockSpec((LORA_COLS, 3 * W), lambda i, j: (0, 0)),
                  vec(W), vec(W), vec(W), vec(W), vec(W), vec(W), vec(W),
                  pl.BlockSpec((LANES, LANES), lambda i, j: (0, 0)),
                  pl.BlockSpec((2 * R, R), lambda i, j: (0, 0))],
        out_specs=pl.BlockSpec((None, R, W), lambda i, j: (i, j, 0)),
        out_shape=jax.ShapeDtypeStruct((b, s, W), BF16),
        scratch_shapes=[pltpu.VMEM((1, RWKV_COLS), F32),
                        pltpu.VMEM((W // LANES, LANES, LANES), F32)],
        compiler_params=pltpu.CompilerParams(dimension_semantics=("arbitrary", "arbitrary"),
                                             vmem_limit_bytes=VMEM_LIMIT),
        name="rwkv",
    )(crw3, mu, wl, w0, a0, k_k, k_a, r_k, ln_g, ln_b, ones_bd, cum_mat)


def _s5_expand_kernel(are_ref, aim_ref, kf_ref, of_ref, wa_ref, wo_ref, *, groups, state, chans):
    rows = are_ref.shape[0]
    base = pl.program_id(0) * rows
    r = base + lax.broadcasted_iota(jnp.int32, (rows, LANES), 0)
    lane = lax.broadcasted_iota(jnp.int32, (rows, LANES), 1)
    in_group = (r // chans) % groups
    st_group = (r // state) % groups

    def emit(dst_ref, col0, src, row_group, per_group):
        per_tile = LANES // per_group
        tiles = groups // per_tile
        for m in range(src.shape[1] // LANES):
            vals = src[:, m * LANES:(m + 1) * LANES]
            for tl in range(tiles):
                c = col0 + (m * tiles + tl) * LANES
                keep = row_group == tl * per_tile + lane // per_group
                dst_ref[:, c:c + LANES] = jnp.where(keep, vals, 0.0).astype(dst_ref.dtype)

    half = groups * state
    emit(wa_ref, 0, are_ref[...], in_group, state)
    emit(wa_ref, half, aim_ref[...], in_group, state)
    emit(wa_ref, 2 * half, kf_ref[...], in_group, chans)
    emit(wo_ref, 0, of_ref[...], st_group, chans)


def _s5_expand(a_re, a_im, kf, of, groups, state, chans):
    rows = a_re.shape[0]
    assert kf.shape[0] == rows and of.shape[0] == rows
    tr = _row_tile(rows, 256)
    n_a = 2 * groups * state + kf.shape[1] // LANES * groups * chans
    n_o = of.shape[1] // LANES * groups * chans
    blk = lambda a: pl.BlockSpec((tr, a.shape[1]), lambda i: (i, 0))
    return pl.pallas_call(
        functools.partial(_s5_expand_kernel, groups=groups, state=state, chans=chans),
        grid=(rows // tr,),
        in_specs=[blk(a_re), blk(a_im), blk(kf), blk(of)],
        out_specs=[pl.BlockSpec((tr, n_a), lambda i: (i, 0)), pl.BlockSpec((tr, n_o), lambda i: (i, 0))],
        out_shape=[jax.ShapeDtypeStruct((rows, n_a), BF16), jax.ShapeDtypeStruct((rows, n_o), BF16)],
        compiler_params=pltpu.CompilerParams(dimension_semantics=("arbitrary",),
                                             vmem_limit_bytes=VMEM_LIMIT),
        name="s5_expand",
    )(a_re, a_im, kf, of)


def _s5_mats(lam_re, lam_im, log_dt, b_re, b_im, c_re, c_im, d_skip):
    L = S5_CHUNK
    G, N = lam_re.shape
    ch = b_re.shape[-1]
    dt = jnp.exp(log_dt)[:, None]
    lr, li = lam_re, lam_im
    mag = jnp.exp(lr * dt)
    lb_re, lb_im = mag * jnp.cos(li * dt), mag * jnp.sin(li * dt)
    den = lr * lr + li * li
    nr, ni = lb_re - 1.0, lb_im
    coef_re = (nr * lr + ni * li) / den
    coef_im = (ni * lr - nr * li) / den
    bb_re = coef_re[..., None] * b_re - coef_im[..., None] * b_im
    bb_im = coef_re[..., None] * b_im + coef_im[..., None] * b_re
    prs, pis = [jnp.ones_like(lb_re)], [jnp.zeros_like(lb_im)]
    for _ in range(L):
        pr_, pi_ = prs[-1], pis[-1]
        prs.append(pr_ * lb_re - pi_ * lb_im)
        pis.append(pr_ * lb_im + pi_ * lb_re)
    pr = jnp.stack(prs)
    pi = jnp.stack(pis)
    hp = lax.Precision.HIGHEST

    def lam_bb(qr, qi):
        return (qr[..., None] * bb_re[None] - qi[..., None] * bb_im[None],
                qr[..., None] * bb_im[None] + qi[..., None] * bb_re[None])

    def repeat_cols(a, inner, reps):
        k = a.shape[-1]
        src = jnp.arange(k)[:, None]
        dst = jnp.arange(k * reps)[None, :]
        rep = ((src // inner == dst // (inner * reps)) & (src % inner == dst % inner)).astype(F32)
        return jnp.dot(a, rep, precision=hp)

    in_rows = L * G * ch
    st_rows = 2 * G * N
    wre, wim = lam_bb(pr[:L][::-1], pi[:L][::-1])
    a_in = [repeat_cols(jnp.swapaxes(part, 2, 3).reshape(in_rows, N), N, LANES // N)
            for part in (wre, wim)]
    lre, lim = lam_bb(pr[:L], pi[:L])
    kern = (jnp.einsum('gon,lgni->lgio', c_re, lre, precision=hp)
            - jnp.einsum('gon,lgni->lgio', c_im, lim, precision=hp))
    kern = kern.at[0].add(d_skip[:, :, None] * jnp.eye(ch, dtype=F32)[None])
    lag = jnp.arange(L)[None, :] - jnp.arange(L)[:, None]
    kst = jnp.where((lag >= 0)[:, :, None, None, None], kern[jnp.maximum(lag, 0)], 0.0)
    kf = jnp.transpose(kst, (0, 2, 3, 1, 4)).reshape(in_rows, L * ch)
    kf = repeat_cols(kf, ch, LANES // ch)
    qr, qi = pr[1:L + 1][:, :, None, :], pi[1:L + 1][:, :, None, :]
    o_re = c_re[None] * qr - c_im[None] * qi
    o_im = -c_re[None] * qi - c_im[None] * qr
    of = jnp.transpose(jnp.stack([o_re, o_im]), (0, 2, 4, 1, 3))
    of = repeat_cols(of.reshape(st_rows, L * ch), ch, LANES // ch)
    w_a, w_out_flat = _s5_expand(a_in[0], a_in[1], kf, of, G, N, ch)
    plr = pr[L].reshape(1, G * N)
    pli = pi[L].reshape(1, G * N)
    return w_a, w_out_flat, plr, pli


def _s5_kernel(u_ref, wa_ref, wo_ref, plr_ref, pli_ref, o_ref, wloc_ref, zprev_ref):
    nch, L, w = u_ref.shape
    half = S5_ZW // 2

    r = _mm(u_ref[:, 0, :], wa_ref[0:w, :])
    for j in range(1, L):
        r = r + _mm(u_ref[:, j, :], wa_ref[j * w:(j + 1) * w, :])
    wloc_ref[...] = r[:, :S5_ZW]
    y_lag = r[:, S5_ZW:]

    plr = plr_ref[...]
    pli = pli_ref[...]

    def step(ci, z):
        zprev_ref[pl.ds(ci, 1), :] = z
        zr, zi = z[:, :half], z[:, half:]
        nz = jnp.concatenate([plr * zr - pli * zi, plr * zi + pli * zr], axis=1)
        return nz + wloc_ref[pl.ds(ci, 1), :]

    lax.fori_loop(0, nch, step, jnp.zeros((1, S5_ZW), F32))
    y = y_lag + _mm(zprev_ref[...], wo_ref[...])
    for j in range(L):
        o_ref[:, j, :] = y[:, j * w:(j + 1) * w]


def _s5(u4, w_a, w_o, plr, pli):
    b, nch, L, w = u4.shape
    const = lambda a: pl.BlockSpec(a.shape, lambda i: (0,) * a.ndim, pipeline_mode=pl.Buffered(1))
    blk = pl.BlockSpec((None, nch, L, w), lambda i: (i, 0, 0, 0))
    return pl.pallas_call(
        _s5_kernel,
        grid=(b,),
        in_specs=[blk, const(w_a), const(w_o), const(plr), const(pli)],
        out_specs=blk,
        out_shape=jax.ShapeDtypeStruct(u4.shape, F32),
        scratch_shapes=[pltpu.VMEM((nch, S5_ZW), F32),
                        pltpu.VMEM((nch, S5_ZW), F32)],
        compiler_params=pltpu.CompilerParams(dimension_semantics=("arbitrary",),
                                             vmem_limit_bytes=VMEM_LIMIT),
        name="s5",
    )(u4, w_a, w_o, plr, pli)


def _merge_tile_math(x_ref, ya_ref, yb_ref, gates_ref, gluw_ref, glub_ref, wba_ref, wbb_ref, wout_ref,
                     fng_ref, rw_ref, rb_ref, tri_ref, upper_ref, x1_ref, t_ref, rt_ref, pos_ref,
                     te_ref, fill_ref, nused_ref, cnt_ref, cur_ref):
    d = x_ref.shape[1]
    tm = x_ref.shape[0]
    y_a = jnp.dot(ya_ref[...], wba_ref[...], preferred_element_type=F32)
    ys = yb_ref[...]
    z = 0.5 * ys * (1.0 + jnp.tanh(math.sqrt(2.0 / math.pi) * (ys + 0.044715 * (ys * ys * ys))))
    z = z * _sigmoid(_mm(z, gluw_ref[...]) + glub_ref[...])
    y_b = _mm(z, wbb_ref[...])
    gates = gates_ref[...].astype(F32)
    merged = _sigmoid(gates[:, :d]) * y_a + _sigmoid(gates[:, d:]) * y_b
    x1 = x_ref[...] + _mm(merged, wout_ref[...])
    x1_ref[...] = x1
    t = _rms_norm(x1, fng_ref[...])
    t_hi = t.astype(BF16)
    t_ref[...] = t

    t_lo = (t - t_hi.astype(F32)).astype(BF16)
    hh_hl = jnp.dot(t_hi, rw_ref[...], preferred_element_type=F32)
    lh = jnp.dot(t_lo, rw_ref[:, :ROUTER_LANES], preferred_element_type=F32)
    logits = hh_hl[:, :ROUTER_LANES] + hh_hl[:, ROUTER_LANES:] + lh + rb_ref[...]
    lane = lax.broadcasted_iota(jnp.int32, logits.shape, 1)
    neg = -jnp.inf
    lane_f = lane.astype(F32)
    big = float(1 << 20)
    is_g = (lane >= N_EXPERTS) & (lane < N_EXPERTS + N_GROUPS)
    gl = jnp.where(is_g, logits, neg)
    gmax = jnp.max(gl, axis=-1, keepdims=True)
    g_p = 1.0 / jnp.sum(jnp.exp(gl - gmax), axis=-1, keepdims=True)
    g_idx = jnp.min(jnp.where(gl == gmax, lane_f - N_EXPERTS, big), axis=-1,
                    keepdims=True).astype(jnp.int32)
    el = jnp.where((lane < N_EXPERTS) & ((lane >> int(math.log2(EXPERTS_PER_GROUP))) == g_idx), logits, neg)
    t1 = jnp.max(el, axis=-1, keepdims=True)
    i1 = jnp.min(jnp.where(el == t1, lane_f, big), axis=-1, keepdims=True).astype(jnp.int32)
    el2 = jnp.where(lane == i1, neg, el)
    t2 = jnp.max(el2, axis=-1, keepdims=True)
    i2 = jnp.min(jnp.where(el2 == t2, lane_f, big), axis=-1, keepdims=True).astype(jnp.int32)
    e21 = jnp.exp(t2 - t1)
    w1 = g_p / (1.0 + e21)
    w2 = g_p * e21 / (1.0 + e21)
    rt_ref[...] = jnp.where(lane == 0, w1, jnp.where(lane == 1, w2, 0.0))

    sh = int(math.log2(MOE_TILE))
    oh1 = lane == i1
    oh2 = lane == i2
    ind = jnp.where(oh1 | oh2, 1.0, 0.0)
    lrank = jnp.dot(tri_ref[...], ind.astype(BF16), preferred_element_type=F32).astype(jnp.int32)
    n_new = lrank[tm - 1:tm, :] + ind[tm - 1:tm, :].astype(jnp.int32)
    cnt = cnt_ref[...]
    cur = cur_ref[...]
    nfree = nused_ref[...]
    tiles_before = (cnt + (MOE_TILE - 1)) >> sh
    newf = ((cnt + n_new + (MOE_TILE - 1)) >> sh) - tiles_before
    newf8 = jnp.broadcast_to(newf.astype(BF16), (8, newf.shape[1]))
    pre = jnp.dot(newf8, upper_ref[...], preferred_element_type=F32)[0:1, :].astype(jnp.int32)
    new_tile = nfree + pre
    grank = cnt + lrank
    ptile = jnp.where((grank >> sh) < tiles_before, cur, new_tile)
    posfull = ((ptile << sh) + (grank & (MOE_TILE - 1))).astype(F32)
    pos1 = jnp.sum(jnp.where(oh1, posfull, 0.0), axis=-1, keepdims=True)
    pos2 = jnp.sum(jnp.where(oh2, posfull, 0.0), axis=-1, keepdims=True)
    tr = lax.broadcasted_iota(jnp.int32, (tm, tm), 0)
    tc = lax.broadcasted_iota(jnp.int32, (tm, tm), 1)
    as_row = lambda v: jnp.sum(jnp.where(tr == tc, jnp.broadcast_to(v, (tm, tm)), 0.0),
                               axis=0, keepdims=True)
    sub = lax.broadcasted_iota(jnp.int32, pos_ref.shape, 0)
    pos_ref[...] = jnp.where(sub == 0, as_row(pos1),
                             jnp.where(sub == 1, as_row(pos2), 0.0)).astype(jnp.int32)

    el_r = lax.broadcasted_iota(jnp.int32, (ROUTER_LANES, ROUTER_LANES), 0)
    el_c = lax.broadcasted_iota(jnp.int32, (ROUTER_LANES, ROUTER_LANES), 1)
    as_col = lambda v: jnp.sum(jnp.where(el_r == el_c, jnp.broadcast_to(v, el_r.shape), 0.0),
                               axis=1, keepdims=True)
    tile_col = as_col(jnp.where(newf > 0, new_tile, -1).astype(F32)).astype(jnp.int32)
    tlane = lax.broadcasted_iota(jnp.int32, (ROUTER_LANES, te_ref.shape[1]), 1)
    erow = lax.broadcasted_iota(jnp.int32, (ROUTER_LANES, te_ref.shape[1]), 0).astype(F32)
    te_ref[...] += jnp.sum(jnp.where(tlane == tile_col, erow, 0.0), axis=0,
                           keepdims=True).astype(jnp.int32)
    to_cur = jnp.minimum(n_new, (tiles_before << sh) - cnt)
    cur_col = as_col(cur.astype(F32)).astype(jnp.int32)
    added = (jnp.where(tlane == cur_col, as_col(to_cur.astype(F32)), 0.0)
             + jnp.where(tlane == tile_col, as_col((n_new - to_cur).astype(F32)), 0.0))
    fill_ref[...] += jnp.sum(added, axis=0, keepdims=True).astype(jnp.int32)
    cnt_ref[...] = cnt + n_new
    cur_ref[...] = jnp.where(newf > 0, new_tile, cur)
    nused_ref[...] = nfree + jnp.sum(newf.astype(F32), axis=-1, keepdims=True).astype(jnp.int32)


def _merge_kernel(x_ref, ya_ref, yb_ref, gates_ref, gluw_ref, glub_ref, wba_ref, wbb_ref, wout_ref,
                  fng_ref, rw_ref, rb_ref, tri_ref, upper_ref, x1_ref, rt_ref, pos_ref,
                  te_ref, fill_ref, nused_ref, xs_ref, cnt_ref, cur_ref, tbuf_even, tbuf_odd, posv,
                  pos_smem, fill_smem, zblk, rsem, psem, zsem):
    i = pl.program_id(0)
    last = pl.num_programs(0) - 1
    tm = x_ref.shape[0]
    tbufs = (tbuf_even, tbuf_odd)

    @pl.when(i == 0)
    def _():
        cnt_ref[...] = jnp.zeros_like(cnt_ref)
        cur_ref[...] = jnp.zeros_like(cur_ref)
        te_ref[...] = jnp.zeros_like(te_ref)
        fill_ref[...] = jnp.zeros_like(fill_ref)
        nused_ref[...] = jnp.zeros_like(nused_ref)

    def row_copies(par, unroll, op):
        def body(r, carry):
            for k in range(2):
                op(pltpu.make_async_copy(tbufs[par].at[pl.ds(r, 1), :],
                                         xs_ref.at[pl.ds(pos_smem[par, k, r], 1), :], rsem.at[par]))
            return carry
        lax.fori_loop(0, tm, body, 0, unroll=unroll)

    def drain_rows(par):
        for _ in range(2):
            pltpu.make_async_copy(tbufs[par], xs_ref.at[pl.ds(0, tm), :], rsem.at[par]).wait()

    def pos_to_smem(par):
        return pltpu.make_async_copy(posv, pos_smem.at[pl.ds(par, 1)], psem)

    @pl.when(i >= 1)
    def _():
        pos_to_smem(0).wait()

    def step(par, dispatch_prev):
        @pl.when(i >= 2)
        def _():
            drain_rows(par)
        if dispatch_prev:
            row_copies(1 - par, True, lambda c: c.start())
        _merge_tile_math(x_ref, ya_ref, yb_ref, gates_ref, gluw_ref, glub_ref, wba_ref, wbb_ref,
                         wout_ref, fng_ref, rw_ref, rb_ref, tri_ref, upper_ref, x1_ref,
                         tbufs[par], rt_ref, pos_ref.at[0], te_ref, fill_ref, nused_ref, cnt_ref,
                         cur_ref)
        posv[...] = pos_ref[...]
        pos_to_smem(par).start()

    pl.when(i == 0)(lambda: step(0, False))
    for par in range(2):
        pl.when((i > 0) & (i % 2 == par))(functools.partial(step, par, True))

    @pl.when(i == last)
    def _():
        pos_to_smem(0).wait()
        for par in range(2):
            @pl.when(i % 2 == par)
            def _(par=par):
                row_copies(par, DMA_UNROLL, lambda c: c.start())

                @pl.when(i >= 1)
                def _():
                    drain_rows(1 - par)
                drain_rows(par)

        fill_copy = pltpu.make_async_copy(fill_ref, fill_smem, psem)
        fill_copy.start()
        zblk[...] = jnp.zeros_like(zblk)
        fill_copy.wait()

        def over_tails(op):
            def tile(j, carry):
                fill = fill_smem[0, j]
                head = (-fill) & (SUBLANES - 1)
                for r in range(SUBLANES - 1):
                    @pl.when(r < head)
                    def _(r=r):
                        op(pltpu.make_async_copy(zblk.at[pl.ds(0, 1), :],
                                                 xs_ref.at[pl.ds(j * MOE_TILE + fill + r, 1), :], zsem))
                start = fill + head
                todo = MOE_TILE - start
                size = MOE_TILE
                while size >= SUBLANES:
                    @pl.when((todo & size) != 0)
                    def _(start=start, size=size):
                        off = pl.multiple_of(j * MOE_TILE + start, SUBLANES)
                        op(pltpu.make_async_copy(zblk.at[pl.ds(0, size), :],
                                                 xs_ref.at[pl.ds(off, size), :], zsem))
                    start = start + (todo & size)
                    size //= 2
                return carry
            lax.fori_loop(0, xs_ref.shape[0] // MOE_TILE, tile, 0)

        over_tails(lambda c: c.start())
        over_tails(lambda c: c.wait())


def _moe_tiles(t):
    return (2 * t) // MOE_TILE + N_EXPERTS


def _merge(x2, ya, yb, gates, gluw, glub, wba, wbb, wout, fng, rw, rb):
    t, d = x2.shape
    tm = MERGE_TILE
    assert t % tm == 0 and tm <= MOE_TILE
    n_tiles = _moe_tiles(t)
    te_lanes = -(-n_tiles // LANES) * LANES
    rr = jnp.arange(tm)
    tri = (rr[None, :] < rr[:, None]).astype(BF16)
    ll = jnp.arange(ROUTER_LANES)
    upper = (ll[:, None] < ll[None, :]).astype(BF16)
    full = lambda a: pl.BlockSpec(a.shape, lambda i: (0,) * a.ndim)
    rowblk = lambda n: pl.BlockSpec((tm, n), lambda i: (i, 0))
    fixed = lambda n: pl.BlockSpec((1, n), lambda i: (0, 0))
    return pl.pallas_call(
        _merge_kernel,
        grid=(t // tm,),
        in_specs=[rowblk(d), rowblk(ya.shape[1]), rowblk(yb.shape[1]), rowblk(gates.shape[1]),
                  full(gluw), full(glub), full(wba), full(wbb), full(wout), full(fng), full(rw),
                  full(rb), full(tri), full(upper)],
        out_specs=[rowblk(d), rowblk(ROUTER_LANES),
                   pl.BlockSpec((1, SUBLANES, tm), lambda i: (i, 0, 0)),
                   fixed(te_lanes), fixed(te_lanes), fixed(ROUTER_LANES),
                   pl.BlockSpec(memory_space=pl.ANY)],
        out_shape=[jax.ShapeDtypeStruct((t, d), F32),
                   jax.ShapeDtypeStruct((t, ROUTER_LANES), F32),
                   jax.ShapeDtypeStruct((t // tm, SUBLANES, tm), jnp.int32),
                   jax.ShapeDtypeStruct((1, te_lanes), jnp.int32),
                   jax.ShapeDtypeStruct((1, te_lanes), jnp.int32),
                   jax.ShapeDtypeStruct((1, ROUTER_LANES), jnp.int32),
                   jax.ShapeDtypeStruct((n_tiles * MOE_TILE, d), F32)],
        scratch_shapes=[pltpu.VMEM((1, ROUTER_LANES), jnp.int32),
                        pltpu.VMEM((1, ROUTER_LANES), jnp.int32),
                        pltpu.VMEM((tm, d), F32),
                        pltpu.VMEM((tm, d), F32),
                        pltpu.VMEM((1, SUBLANES, tm), jnp.int32),
                        pltpu.SMEM((2, SUBLANES, tm), jnp.int32),
                        pltpu.SMEM((1, te_lanes), jnp.int32),
                        pltpu.VMEM((MOE_TILE, d), F32),
                        pltpu.SemaphoreType.DMA((2,)),
                        pltpu.SemaphoreType.DMA(()),
                        pltpu.SemaphoreType.DMA(())],
        compiler_params=pltpu.CompilerParams(dimension_semantics=("arbitrary",),
                                             vmem_limit_bytes=VMEM_LIMIT),
        name="merge",
    )(x2, ya, yb, gates, gluw, glub, wba, wbb, wout, fng, rw, rb, tri, upper)


def _experts_kernel(te_ref, nused_ref, xs_ref, wg_ref, wu_ref, wd_ref, y_ref):
    j = pl.program_id(0)

    @pl.when(j < nused_ref[0])
    def _():
        x = xs_ref[...].astype(BF16)
        hg = _mm(x, wg_ref[...])
        hid = hg * _sigmoid(hg) * _mm(x, wu_ref[...])
        y_ref[...] = _mm(hid, wd_ref[...])

    @pl.when(j >= nused_ref[0])
    def _():
        y_ref[...] = jnp.zeros_like(y_ref)


def _experts(tile_expert, n_used, xs, wg, wu, wd):
    n_rows, hw = xs.shape
    ne, d, de = wg.shape
    return pl.pallas_call(
        _experts_kernel,
        grid_spec=pltpu.PrefetchScalarGridSpec(
            num_scalar_prefetch=2,
            grid=(n_rows // MOE_TILE,),
            in_specs=[pl.BlockSpec((MOE_TILE, hw), lambda j, te, nu: (j, 0)),
                      pl.BlockSpec((None, d, de), lambda j, te, nu: (te[j], 0, 0)),
                      pl.BlockSpec((None, d, de), lambda j, te, nu: (te[j], 0, 0)),
                      pl.BlockSpec((None, de, d), lambda j, te, nu: (te[j], 0, 0))],
            out_specs=pl.BlockSpec((MOE_TILE, hw), lambda j, te, nu: (j, 0))),
        out_shape=jax.ShapeDtypeStruct((n_rows, hw), F32),
        compiler_params=pltpu.CompilerParams(dimension_semantics=("arbitrary",),
                                             vmem_limit_bytes=VMEM_LIMIT),
        name="experts",
    )(tile_expert, n_used, xs, wg, wu, wd)


def _ple_kernel(pos1_ref, pos2_ref, x_ref, rt_ref, p_ref, png_ref, wg_ref, wp_ref, fng_ref, y_ref,
                o_ref, ybuf_even, ybuf_odd, sem):
    i = pl.program_id(0)
    tm = x_ref.shape[0]
    ybufs = (ybuf_even, ybuf_odd)

    def gather(tile, par, unroll):
        def body(r, carry):
            for k, pos_ref in enumerate((pos1_ref, pos2_ref)):
                pltpu.make_async_copy(y_ref.at[pl.ds(pos_ref[tile * tm + r], 1), :],
                                      ybufs[par].at[k, pl.ds(r, 1), :], sem.at[par]).start()
            return carry
        lax.fori_loop(0, tm, body, 0, unroll=unroll)

    @pl.when(i == 0)
    def _():
        gather(i, 0, DMA_UNROLL)

    def step(par, prefetch_next):
        ybuf = ybufs[par]
        for k in range(2):
            pltpu.make_async_copy(y_ref.at[pl.ds(0, tm), :], ybuf.at[k], sem.at[par]).wait()
        if prefetch_next:
            gather(i + 1, 1 - par, True)
        rt = rt_ref[...]
        x2 = x_ref[...] + rt[:, 0:1] * ybuf[0] + rt[:, 1:2] * ybuf[1]
        hp = _rms_norm(x2, png_ref[...])
        gate = _sigmoid(_mm(hp, wg_ref[...]))
        x3 = x2 + gate * _mm(p_ref[...], wp_ref[...])
        o_ref[...] = _rms_norm(x3, fng_ref[...])

    last = pl.num_programs(0) - 1
    for par in range(2):
        pl.when((i % 2 == par) & (i < last))(functools.partial(step, par, True))
        pl.when((i % 2 == par) & (i == last))(functools.partial(step, par, False))


def _ple(pos1, pos2, x1, rt, p2, png, wg, wp, fng, y_pack, tm):
    t, d = x1.shape
    hw = y_pack.shape[1]
    full = lambda a: pl.BlockSpec(a.shape, lambda i, p1, p2_: (0,) * a.ndim)
    rowblk = lambda n: pl.BlockSpec((tm, n), lambda i, p1, p2_: (i, 0))
    return pl.pallas_call(
        _ple_kernel,
        grid_spec=pltpu.PrefetchScalarGridSpec(
            num_scalar_prefetch=2,
            grid=(t // tm,),
            in_specs=[rowblk(d), rowblk(rt.shape[1]), rowblk(p2.shape[1]),
                      full(png), full(wg), full(wp), full(fng),
                      pl.BlockSpec(memory_space=pl.ANY)],
            out_specs=rowblk(d),
            scratch_shapes=[pltpu.VMEM((2, tm, hw), F32),
                            pltpu.VMEM((2, tm, hw), F32),
                            pltpu.SemaphoreType.DMA((2,))]),
        out_shape=jax.ShapeDtypeStruct((t, d), F32),
        compiler_params=pltpu.CompilerParams(dimension_semantics=("arbitrary",),
                                             vmem_limit_bytes=VMEM_LIMIT),
        name="ple",
    )(pos1, pos2, x1, rt, p2, png, wg, wp, fng, y_pack)


def _row_tile(t, want):
    tm = min(want, t)
    while t % tm:
        tm //= 2
    return tm


def _layer(x, p, mix_norm, w_in, mu_shift, rk_w0, rk_w_up, rk_a0, rk_a_up, rk_g_up,
           rk_k_k, rk_k_a, rk_r_k, rk_ln_g, rk_ln_b, s5_lam_re, s5_lam_im, s5_log_dt,
           s5_b_re, s5_b_im, s5_c_re, s5_c_im, s5_d, s5_glu_w, s5_glu_b,
           w_branch_a, w_branch_b, w_out, ffn_norm, router_group_w, router_group_b,
           router_expert_w, router_expert_b, exp_w_gate, exp_w_up, exp_w_down,
           ple_norm, ple_gate_w, ple_proj):
    b, s, d = x.shape
    t = b * s
    W = RWKV_WIDTH
    row = lambda a: a.reshape(1, -1).astype(F32)
    x2 = x.reshape(t, d)

    crw, us5, gates = _in_proj(x2, row(mix_norm), w_in.astype(BF16), _row_tile(t, 512))

    wl = jnp.zeros((LORA_COLS, 3 * W), F32)
    wl = wl.at[:DECAY_LORA, :W].set(rk_w_up)
    wl = wl.at[DECAY_LORA:DECAY_LORA + AAA_LORA, W:2 * W].set(rk_a_up)
    wl = wl.at[DECAY_LORA + AAA_LORA:, 2 * W:].set(rk_g_up)
    hid = jnp.arange(LANES) // RWKV_HEAD_DIM
    ones_bd = (hid[:, None] == hid[None, :]).astype(BF16)
    ya = _rwkv(crw.reshape(b, s, RWKV_COLS), row(mu_shift), wl.astype(BF16), row(rk_w0), row(rk_a0),
               row(rk_k_k), row(rk_k_a), row(rk_r_k), row(rk_ln_g), row(rk_ln_b), ones_bd,
               _rwkv_cum_matrix(_row_tile(s, RWKV_BLOCK)))

    s5_wa, s5_wo, plr, pli = _s5_mats(s5_lam_re, s5_lam_im, s5_log_dt, s5_b_re, s5_b_im,
                                      s5_c_re, s5_c_im, s5_d)
    yb = _s5(us5.reshape(b, s // S5_CHUNK, S5_CHUNK, S5_WIDTH), s5_wa, s5_wo, plr, pli)

    rw = jnp.zeros((d, ROUTER_LANES), F32)
    rw = rw.at[:, :N_EXPERTS].set(router_expert_w).at[:, N_EXPERTS:N_EXPERTS + N_GROUPS].set(router_group_w)
    rb = jnp.zeros((1, ROUTER_LANES), F32)
    rb = rb.at[0, :N_EXPERTS].set(router_expert_b).at[0, N_EXPERTS:N_EXPERTS + N_GROUPS].set(router_group_b)
    rw_hi = rw.astype(BF16)
    rw = jnp.concatenate([rw_hi, (rw - rw_hi.astype(F32)).astype(BF16)], axis=1)
    x1, rt, pos, tile_expert, _, n_used, xs = _merge(
        x2, ya.reshape(t, W), yb.reshape(t, S5_WIDTH), gates, s5_glu_w.astype(BF16), row(s5_glu_b),
        w_branch_a.astype(BF16), w_branch_b.astype(BF16), w_out.astype(BF16), row(ffn_norm), rw, rb)
    n_tiles = _moe_tiles(t)
    pos1, pos2 = pos[:, 0, :].reshape(t), pos[:, 1, :].reshape(t)
    y_pack = _experts(tile_expert[0, :n_tiles], n_used[0, :1], xs, exp_w_gate, exp_w_up, exp_w_down)
    return (pos1, pos2, x1, rt, p.reshape(t, -1), row(ple_norm), ple_gate_w.astype(BF16),
            ple_proj.astype(BF16), y_pack)


def kernel(x, p, mix_norm, w_in, mu_shift, rk_w0, rk_w_up, rk_a0, rk_a_up, rk_g_up, rk_k_k, rk_k_a,
           rk_r_k, rk_ln_g, rk_ln_b, s5_lam_re, s5_lam_im, s5_log_dt, s5_b_re, s5_b_im, s5_c_re,
           s5_c_im, s5_d, s5_glu_w, s5_glu_b, w_branch_a, w_branch_b, w_out, ffn_norm,
           router_group_w, router_group_b, router_expert_w, router_expert_b, exp_w_gate, exp_w_up,
           exp_w_down, ple_norm, ple_gate_w, ple_proj, final_norm):
    b, s, d = x.shape
    depth = w_in.shape[0]
    assert depth == 1, "the final norm is fused into the last layer's PLE kernel"
    i = 0
    pos1, pos2, x1, rt, p2, png, wpg, wpp, y_pack = _layer(
        x, p[i], mix_norm[i], w_in[i], mu_shift[i], rk_w0[i], rk_w_up[i], rk_a0[i], rk_a_up[i],
        rk_g_up[i], rk_k_k[i], rk_k_a[i], rk_r_k[i], rk_ln_g[i], rk_ln_b[i], s5_lam_re[i],
        s5_lam_im[i], s5_log_dt[i], s5_b_re[i], s5_b_im[i], s5_c_re[i], s5_c_im[i], s5_d[i],
        s5_glu_w[i], s5_glu_b[i], w_branch_a[i], w_branch_b[i], w_out[i], ffn_norm[i],
        router_group_w[i], router_group_b[i], router_expert_w[i], router_expert_b[i],
        exp_w_gate[i], exp_w_up[i], exp_w_down[i], ple_norm[i], ple_gate_w[i], ple_proj[i])
    out = _ple(pos1, pos2, x1, rt, p2, png, wpg, wpp, final_norm.reshape(1, -1).astype(F32), y_pack,
               _row_tile(b * s, MERGE_TILE))
    return out.reshape(b, s, d)
```

```python
import functools
import math

import jax
import jax.numpy as jnp
from jax import lax
from jax.experimental import pallas as pl
from jax.experimental.pallas import tpu as pltpu

F32 = jnp.float32
BF16 = jnp.bfloat16

NORM_EPS = 1e-6
GN_EPS = 64e-5

RWKV_HEADS = 8
RWKV_HEAD_DIM = 64
RWKV_WIDTH = RWKV_HEADS * RWKV_HEAD_DIM
DECAY_LORA = 64
AAA_LORA = 64
GATE_LORA = 128
LORA_COLS = DECAY_LORA + AAA_LORA + GATE_LORA
RWKV_COLS = 3 * RWKV_WIDTH + LORA_COLS
S5_GROUPS = 16
S5_GROUP_CH = 16
S5_WIDTH = S5_GROUPS * S5_GROUP_CH
S5_STATE = 64
S5_ZW = 2 * S5_GROUPS * S5_STATE
N_GROUPS = 4
EXPERTS_PER_GROUP = 8
N_EXPERTS = N_GROUPS * EXPERTS_PER_GROUP

LANES = 128
SUBLANES = 8
RWKV_CHUNK = 64
RWKV_INV_BLOCK = 16
RWKV_BLOCK = 256
S5_CHUNK = 8
ROUTER_LANES = 128
MOE_TILE = 512
MERGE_TILE = 512
DMA_UNROLL = 8
VMEM_LIMIT = 56 * 1024 * 1024


def _mm(a, b):
    return jnp.dot(a.astype(BF16), b.astype(BF16), preferred_element_type=F32)


def _mm_nt(a, b):
    return lax.dot_general(a.astype(BF16), b.astype(BF16), (((1,), (1,)), ((), ())),
                           preferred_element_type=F32)


def _mm_tn(a, b):
    return lax.dot_general(a.astype(BF16), b.astype(BF16), (((0,), (0,)), ((), ())),
                           preferred_element_type=F32)


def _split2(x):
    hi = x.astype(BF16)
    return hi, (x - hi.astype(F32)).astype(BF16)


def _mm_split_lhs(a_bf16, x):
    hi, lo = _split2(x)
    d = lambda t: jnp.dot(a_bf16, t, preferred_element_type=F32)
    return d(hi) + d(lo)


def _mm_split_rhs(x, b_bf16):
    hi, lo = _split2(x)
    d = lambda t: jnp.dot(t, b_bf16, preferred_element_type=F32)
    return d(hi) + d(lo)


def _sigmoid(x):
    return 0.5 * jnp.tanh(0.5 * x) + 0.5


def _rms_norm(x, g):
    ms = jnp.mean(x * x, axis=-1, keepdims=True)
    return x * lax.rsqrt(ms + NORM_EPS) * g


def _in_proj_kernel(x_ref, g_ref, w_ref, crw_ref, us5_ref, gates_ref):
    h = _rms_norm(x_ref[...], g_ref[...]).astype(BF16)
    c0, c1 = RWKV_COLS, RWKV_COLS + S5_WIDTH
    crw_ref[...] = jnp.dot(h, w_ref[:, :c0], preferred_element_type=F32)
    us5_ref[...] = jnp.dot(h, w_ref[:, c0:c1], preferred_element_type=F32)
    gates_ref[...] = jnp.dot(h, w_ref[:, c1:], preferred_element_type=F32).astype(BF16)


def _in_proj(x2, g, w, tm):
    t, d = x2.shape
    n = w.shape[1]
    ng = n - RWKV_COLS - S5_WIDTH
    return pl.pallas_call(
        _in_proj_kernel,
        grid=(t // tm,),
        in_specs=[pl.BlockSpec((tm, d), lambda i: (i, 0)),
                  pl.BlockSpec((1, d), lambda i: (0, 0)),
                  pl.BlockSpec((d, n), lambda i: (0, 0))],
        out_specs=[pl.BlockSpec((tm, RWKV_COLS), lambda i: (i, 0)),
                   pl.BlockSpec((tm, S5_WIDTH), lambda i: (i, 0)),
                   pl.BlockSpec((tm, ng), lambda i: (i, 0))],
        out_shape=[jax.ShapeDtypeStruct((t, RWKV_COLS), F32),
                   jax.ShapeDtypeStruct((t, S5_WIDTH), F32),
                   jax.ShapeDtypeStruct((t, ng), BF16)],
        compiler_params=pltpu.CompilerParams(dimension_semantics=("arbitrary",),
                                             vmem_limit_bytes=VMEM_LIMIT),
        name="in_proj",
    )(x2, g, w)


def _rwkv_kernel(c_ref, mu_ref, wl_ref, w0_ref, a0_ref, kk_ref, ka_ref, rk_ref, lng_ref, lnb_ref,
                 ones_ref, cum_ref, o_ref, carry_ref, s_ref):
    C = RWKV_CHUNK
    W = RWKV_WIDTH
    npairs = W // LANES

    @pl.when(pl.program_id(1) == 0)
    def _():
        carry_ref[...] = jnp.zeros_like(carry_ref)
        s_ref[...] = jnp.zeros_like(s_ref)

    c = c_ref[...]
    R = c.shape[0]
    nchunks = R // C
    row = lax.broadcasted_iota(jnp.int32, (R, 1), 0)
    prev = jnp.where(row == 0, carry_ref[...], pltpu.roll(c, 1, 0))
    carry_ref[...] = c[R - 1:R, :]
    cs = c + (prev - c) * mu_ref[...]

    r = cs[:, 0:W]
    k = cs[:, W:2 * W]
    v = cs[:, 2 * W:3 * W]
    lin = cs[:, 3 * W:]
    llane = lax.broadcasted_iota(jnp.int32, lin.shape, 1)
    lact = jnp.where(llane < DECAY_LORA, jnp.tanh(lin),
                     jnp.where(llane < DECAY_LORA + AAA_LORA, lin, _sigmoid(lin)))
    lo = _mm(lact, wl_ref[...])
    zw = -(w0_ref[...] + lo[:, 0:W])
    softplus = jnp.maximum(zw, 0.0) + jnp.log(1.0 + jnp.exp(-jnp.abs(zw)))
    ld = -jnp.exp(-softplus - 0.5)
    a = _sigmoid(a0_ref[...] + lo[:, W:2 * W])
    g = lo[:, 2 * W:3 * W]

    ones_bd = ones_ref[...]
    segsum = lambda t: jnp.concatenate(
        [_mm_split_rhs(t[:, p * LANES:(p + 1) * LANES], ones_bd) for p in range(npairs)], axis=1)
    kk = k * kk_ref[...]
    kkn = kk / jnp.maximum(jnp.sqrt(segsum(kk * kk)), 1e-12)
    kmod = k * (1.0 + (a - 1.0) * ka_ref[...])

    cums = _mm_split_lhs(cum_ref[...], ld)
    cum = cums[:R]
    tot = cums[R:]
    inv = jnp.exp(-cum)
    tail = jnp.exp(tot - cum)
    At = -kkn * jnp.exp(cum - ld)
    Rt = r * jnp.exp(cum)
    kka = kkn * a
    Bt = kka * inv
    Kt = kmod * inv
    Bend = kka * tail
    Kend = kmod * tail
    pc = jnp.exp(tot)

    lane = lax.broadcasted_iota(jnp.int32, (C, LANES), 1)
    h0 = lane < RWKV_HEAD_DIM
    split = lambda t: jnp.concatenate([jnp.where(h0, t, 0.0), jnp.where(h0, 0.0, t)], axis=0)
    grow = lax.broadcasted_iota(jnp.int32, (C, 4 * C), 0)
    gcol = lax.broadcasted_iota(jnp.int32, (C, 4 * C), 1) & (C - 1)
    r2 = lax.broadcasted_iota(jnp.int32, (2 * C, 2 * C), 0)
    c2 = lax.broadcasted_iota(jnp.int32, (2 * C, 2 * C), 1)
    eye = (r2 == c2).astype(F32)
    blk_shift = int(math.log2(RWKV_INV_BLOCK))
    same_blk = (r2 >> blk_shift) == (c2 >> blk_shift)
    same_head = (r2 < C) == (c2 < C)
    same_head2 = jnp.concatenate([same_head, same_head], axis=0)
    zeros_c = jnp.zeros((C, LANES), F32)
    zeros_2c = jnp.zeros((2 * C, LANES), F32)

    units = [(ci, p) for ci in range(nchunks) for p in range(npairs)]
    blk = lambda t, u: t[u[0] * C:(u[0] + 1) * C, u[1] * LANES:(u[1] + 1) * LANES]
    each = lambda f, *ls: [f(*xs) for xs in zip(*ls)]

    lhs = [jnp.concatenate([blk(At, u), blk(Rt, u)], axis=0) for u in units]
    rhs = [jnp.concatenate([split(blk(Bt, u)), split(blk(Kt, u))], axis=0) for u in units]
    G = each(_mm_nt, lhs, rhs)
    a_row = [jnp.where(gcol < grow, t[:C], 0.0) for t in G]
    m_row = [jnp.where(gcol <= grow, t[C:], 0.0) for t in G]
    a_bd = [split(t[:, :2 * C]) for t in a_row]

    a_d = [jnp.where(same_blk, t, 0.0) for t in a_bd]
    a_off = each(lambda x, y: x - y, a_bd, a_d)
    dinv = [eye + t for t in a_d]
    pw = a_d
    for _ in range(blk_shift - 1):
        pw = each(_mm, pw, pw)
        dinv = each(lambda x, y: x + _mm(x, y), dinv, pw)
    n1 = each(_mm, dinv, a_off)
    n2 = each(_mm, n1, n1)
    n3 = each(_mm, n1, n2)
    tinv = each(lambda x1, x2, x3, dv: _mm(eye + x1 + x2 + x3, dv), n1, n2, n3, dinv)

    vp = [blk(v, u) for u in units]
    v_st = [split(t) for t in vp]
    rhs0 = each(lambda ar, vs: _mm(ar[:, 2 * C:], vs), a_row, v_st)
    wu = each(lambda t, l, r0: _mm(t, jnp.concatenate([split(l[:C]), split(r0)], axis=1)),
              tinv, lhs, rhs0)
    wu_lp = [t[:C] + t[C:] for t in wu]
    mn = each(lambda w_, v_, u: _mm_tn(
        jnp.concatenate([w_, jnp.concatenate([zeros_c, v_], axis=1)], axis=0),
        jnp.concatenate([blk(Bend, u), blk(Kend, u)], axis=0)), wu_lp, vp, units)
    mn = [jnp.where(same_head2, t, 0.0) for t in mn]
    qy = each(lambda m_, w_, vs: _mm(m_, jnp.concatenate(
        [w_, jnp.concatenate([zeros_2c, vs], axis=1)], axis=0)), m_row, wu, v_st)

    ys = [[None] * npairs for _ in range(nchunks)]
    states = [s_ref[p] for p in range(npairs)]
    for i, (ci, p) in enumerate(units):
        s_old = states[p]
        q = lhs[i][C:] + qy[i][:, :LANES]
        ys[ci][p] = _mm_nt(q, s_old) + qy[i][:, LANES:]
        states[p] = s_old * blk(pc, (ci, p))[0:1, :] + _mm(s_old, mn[i][:LANES]) + mn[i][LANES:]
    for p in range(npairs):
        s_ref[p] = states[p]

    y = jnp.concatenate([jnp.concatenate(t, axis=1) for t in ys], axis=0)
    inv_n = 1.0 / RWKV_HEAD_DIM
    mean = segsum(y) * inv_n
    d = y - mean
    var = segsum(d * d) * inv_n
    yn = d * lax.rsqrt(var + GN_EPS) * lng_ref[...] + lnb_ref[...]
    bonus = segsum(r * kmod * rk_ref[...]) * v
    o_ref[...] = ((yn + bonus) * g).astype(o_ref.dtype)


def _rwkv_cum_matrix(rows):
    t = jnp.arange(rows)
    same = (t[:, None] // RWKV_CHUNK) == (t[None, :] // RWKV_CHUNK)
    return jnp.concatenate([same & (t[None, :] <= t[:, None]), same], axis=0).astype(BF16)


def _rwkv(crw3, mu, wl, w0, a0, k_k, k_a, r_k, ln_g, ln_b, ones_bd, cum_mat):
    b, s, _ = crw3.shape
    R = cum_mat.shape[1]
    W = RWKV_WIDTH
    vec = lambda n: pl.BlockSpec((1, n), lambda i, j: (0, 0))
    return pl.pallas_call(
        _rwkv_kernel,
        grid=(b, s // R),
        in_specs=[pl.BlockSpec((None, R, RWKV_COLS), lambda i, j: (i, j, 0)),
                  vec(RWKV_COLS),
                  pl.BlockSpec((LORA_COLS, 3 * W), lambda i, j: (0, 0)),
                  vec(W), vec(W), vec(W), vec(W), vec(W), vec(W), vec(W),
                  pl.BlockSpec((LANES, LANES), lambda i, j: (0, 0)),
                  pl.BlockSpec((2 * R, R), lambda i, j: (0, 0))],
        out_specs=pl.BlockSpec((None, R, W), lambda i, j: (i, j, 0)),
        out_shape=jax.ShapeDtypeStruct((b, s, W), BF16),
        scratch_shapes=[pltpu.VMEM((1, RWKV_COLS), F32),
                        pltpu.VMEM((W // LANES, LANES, LANES), F32)],
        compiler_params=pltpu.CompilerParams(dimension_semantics=("arbitrary", "arbitrary"),
                                             vmem_limit_bytes=VMEM_LIMIT),
        name="rwkv",
    )(crw3, mu, wl, w0, a0, k_k, k_a, r_k, ln_g, ln_b, ones_bd, cum_mat)


def _s5_expand_kernel(are_ref, aim_ref, kf_ref, of_ref, wa_ref, wo_ref, *, groups, state, chans):
    rows = are_ref.shape[0]
    base = pl.program_id(0) * rows
    r = base + lax.broadcasted_iota(jnp.int32, (rows, LANES), 0)
    lane = lax.broadcasted_iota(jnp.int32, (rows, LANES), 1)
    in_group = (r // chans) % groups
    st_group = (r // state) % groups

    def emit(dst_ref, col0, src, row_group, per_group):
        per_tile = LANES // per_group
        tiles = groups // per_tile
        for m in range(src.shape[1] // LANES):
            vals = src[:, m * LANES:(m + 1) * LANES]
            for tl in range(tiles):
                c = col0 + (m * tiles + tl) * LANES
                keep = row_group == tl * per_tile + lane // per_group
                dst_ref[:, c:c + LANES] = jnp.where(keep, vals, 0.0).astype(dst_ref.dtype)

    half = groups * state
    emit(wa_ref, 0, are_ref[...], in_group, state)
    emit(wa_ref, half, aim_ref[...], in_group, state)
    emit(wa_ref, 2 * half, kf_ref[...], in_group, chans)
    emit(wo_ref, 0, of_ref[...], st_group, chans)


def _s5_expand(a_re, a_im, kf, of, groups, state, chans):
    rows = a_re.shape[0]
    assert kf.shape[0] == rows and of.shape[0] == rows
    tr = _row_tile(rows, 256)
    n_a = 2 * groups * state + kf.shape[1] // LANES * groups * chans
    n_o = of.shape[1] // LANES * groups * chans
    blk = lambda a: pl.BlockSpec((tr, a.shape[1]), lambda i: (i, 0))
    return pl.pallas_call(
        functools.partial(_s5_expand_kernel, groups=groups, state=state, chans=chans),
        grid=(rows // tr,),
        in_specs=[blk(a_re), blk(a_im), blk(kf), blk(of)],
        out_specs=[pl.BlockSpec((tr, n_a), lambda i: (i, 0)), pl.BlockSpec((tr, n_o), lambda i: (i, 0))],
        out_shape=[jax.ShapeDtypeStruct((rows, n_a), BF16), jax.ShapeDtypeStruct((rows, n_o), BF16)],
        compiler_params=pltpu.CompilerParams(dimension_semantics=("arbitrary",),
                                             vmem_limit_bytes=VMEM_LIMIT),
        name="s5_expand",
    )(a_re, a_im, kf, of)


def _s5_mats(lam_re, lam_im, log_dt, b_re, b_im, c_re, c_im, d_skip):
    L = S5_CHUNK
    G, N = lam_re.shape
    ch = b_re.shape[-1]
    dt = jnp.exp(log_dt)[:, None]
    lr, li = lam_re, lam_im
    mag = jnp.exp(lr * dt)
    lb_re, lb_im = mag * jnp.cos(li * dt), mag * jnp.sin(li * dt)
    den = lr * lr + li * li
    nr, ni = lb_re - 1.0, lb_im
    coef_re = (nr * lr + ni * li) / den
    coef_im = (ni * lr - nr * li) / den
    bb_re = coef_re[..., None] * b_re - coef_im[..., None] * b_im
    bb_im = coef_re[..., None] * b_im + coef_im[..., None] * b_re
    prs, pis = [jnp.ones_like(lb_re)], [jnp.zeros_like(lb_im)]
    for _ in range(L):
        pr_, pi_ = prs[-1], pis[-1]
        prs.append(pr_ * lb_re - pi_ * lb_im)
        pis.append(pr_ * lb_im + pi_ * lb_re)
    pr = jnp.stack(prs)
    pi = jnp.stack(pis)
    hp = lax.Precision.HIGHEST

    def lam_bb(qr, qi):
        return (qr[..., None] * bb_re[None] - qi[..., None] * bb_im[None],
                qr[..., None] * bb_im[None] + qi[..., None] * bb_re[None])

    def repeat_cols(a, inner, reps):
        k = a.shape[-1]
        src = jnp.arange(k)[:, None]
        dst = jnp.arange(k * reps)[None, :]
        rep = ((src // inner == dst // (inner * reps)) & (src % inner == dst % inner)).astype(F32)
        return jnp.dot(a, rep, precision=hp)

    in_rows = L * G * ch
    st_rows = 2 * G * N
    wre, wim = lam_bb(pr[:L][::-1], pi[:L][::-1])
    a_in = [repeat_cols(jnp.swapaxes(part, 2, 3).reshape(in_rows, N), N, LANES // N)
            for part in (wre, wim)]
    lre, lim = lam_bb(pr[:L], pi[:L])
    kern = (jnp.einsum('gon,lgni->lgio', c_re, lre, precision=hp)
            - jnp.einsum('gon,lgni->lgio', c_im, lim, precision=hp))
    kern = kern.at[0].add(d_skip[:, :, None] * jnp.eye(ch, dtype=F32)[None])
    lag = jnp.arange(L)[None, :] - jnp.arange(L)[:, None]
    kst = jnp.where((lag >= 0)[:, :, None, None, None], kern[jnp.maximum(lag, 0)], 0.0)
    kf = jnp.transpose(kst, (0, 2, 3, 1, 4)).reshape(in_rows, L * ch)
    kf = repeat_cols(kf, ch, LANES // ch)
    qr, qi = pr[1:L + 1][:, :, None, :], pi[1:L + 1][:, :, None, :]
    o_re = c_re[None] * qr - c_im[None] * qi
    o_im = -c_re[None] * qi - c_im[None] * qr
    of = jnp.transpose(jnp.stack([o_re, o_im]), (0, 2, 4, 1, 3))
    of = repeat_cols(of.reshape(st_rows, L * ch), ch, LANES // ch)
    w_a, w_out_flat = _s5_expand(a_in[0], a_in[1], kf, of, G, N, ch)
    plr = pr[L].reshape(1, G * N)
    pli = pi[L].reshape(1, G * N)
    return w_a, w_out_flat, plr, pli


def _s5_kernel(u_ref, wa_ref, wo_ref, plr_ref, pli_ref, o_ref, wloc_ref, zprev_ref):
    nch, L, w = u_ref.shape
    half = S5_ZW // 2

    r = _mm(u_ref[:, 0, :], wa_ref[0:w, :])
    for j in range(1, L):
        r = r + _mm(u_ref[:, j, :], wa_ref[j * w:(j + 1) * w, :])
    wloc_ref[...] = r[:, :S5_ZW]
    y_lag = r[:, S5_ZW:]

    plr = plr_ref[...]
    pli = pli_ref[...]

    def step(ci, z):
        zprev_ref[pl.ds(ci, 1), :] = z
        zr, zi = z[:, :half], z[:, half:]
        nz = jnp.concatenate([plr * zr - pli * zi, plr * zi + pli * zr], axis=1)
        return nz + wloc_ref[pl.ds(ci, 1), :]

    lax.fori_loop(0, nch, step, jnp.zeros((1, S5_ZW), F32))
    y = y_lag + _mm(zprev_ref[...], wo_ref[...])
    for j in range(L):
        o_ref[:, j, :] = y[:, j * w:(j + 1) * w]


def _s5(u4, w_a, w_o, plr, pli):
    b, nch, L, w = u4.shape
    const = lambda a: pl.BlockSpec(a.shape, lambda i: (0,) * a.ndim, pipeline_mode=pl.Buffered(1))
    blk = pl.BlockSpec((None, nch, L, w), lambda i: (i, 0, 0, 0))
    return pl.pallas_call(
        _s5_kernel,
        grid=(b,),
        in_specs=[blk, const(w_a), const(w_o), const(plr), const(pli)],
        out_specs=blk,
        out_shape=jax.ShapeDtypeStruct(u4.shape, F32),
        scratch_shapes=[pltpu.VMEM((nch, S5_ZW), F32),
                        pltpu.VMEM((nch, S5_ZW), F32)],
        compiler_params=pltpu.CompilerParams(dimension_semantics=("arbitrary",),
                                             vmem_limit_bytes=VMEM_LIMIT),
        name="s5",
    )(u4, w_a, w_o, plr, pli)


def _merge_tile_math(x_ref, ya_ref, yb_ref, gates_ref, gluw_ref, glub_ref, wba_ref, wbb_ref, wout_ref,
                     fng_ref, rw_ref, rb_ref, tri_ref, upper_ref, x1_ref, t_ref, rt_ref, pos_ref,
                     te_ref, fill_ref, nused_ref, cnt_ref, cur_ref):
    d = x_ref.shape[1]
    tm = x_ref.shape[0]
    y_a = jnp.dot(ya_ref[...], wba_ref[...], preferred_element_type=F32)
    ys = yb_ref[...]
    z = 0.5 * ys * (1.0 + jnp.tanh(math.sqrt(2.0 / math.pi) * (ys + 0.044715 * (ys * ys * ys))))
    z = z * _sigmoid(_mm(z, gluw_ref[...]) + glub_ref[...])
    y_b = _mm(z, wbb_ref[...])
    gates = gates_ref[...].astype(F32)
    merged = _sigmoid(gates[:, :d]) * y_a + _sigmoid(gates[:, d:]) * y_b
    x1 = x_ref[...] + _mm(merged, wout_ref[...])
    x1_ref[...] = x1
    t = _rms_norm(x1, fng_ref[...])
    t_hi = t.astype(BF16)
    t_ref[...] = t

    t_lo = (t - t_hi.astype(F32)).astype(BF16)
    hh_hl = jnp.dot(t_hi, rw_ref[...], preferred_element_type=F32)
    lh = jnp.dot(t_lo, rw_ref[:, :ROUTER_LANES], preferred_element_type=F32)
    logits = hh_hl[:, :ROUTER_LANES] + hh_hl[:, ROUTER_LANES:] + lh + rb_ref[...]
    lane = lax.broadcasted_iota(jnp.int32, logits.shape, 1)
    neg = -jnp.inf
    lane_f = lane.astype(F32)
    big = float(1 << 20)
    is_g = (lane >= N_EXPERTS) & (lane < N_EXPERTS + N_GROUPS)
    gl = jnp.where(is_g, logits, neg)
    gmax = jnp.max(gl, axis=-1, keepdims=True)
    g_p = 1.0 / jnp.sum(jnp.exp(gl - gmax), axis=-1, keepdims=True)
    g_idx = jnp.min(jnp.where(gl == gmax, lane_f - N_EXPERTS, big), axis=-1,
                    keepdims=True).astype(jnp.int32)
    el = jnp.where((lane < N_EXPERTS) & ((lane >> int(math.log2(EXPERTS_PER_GROUP))) == g_idx), logits, neg)
    t1 = jnp.max(el, axis=-1, keepdims=True)
    i1 = jnp.min(jnp.where(el == t1, lane_f, big), axis=-1, keepdims=True).astype(jnp.int32)
    el2 = jnp.where(lane == i1, neg, el)
    t2 = jnp.max(el2, axis=-1, keepdims=True)
    i2 = jnp.min(jnp.where(el2 == t2, lane_f, big), axis=-1, keepdims=True).astype(jnp.int32)
    e21 = jnp.exp(t2 - t1)
    w1 = g_p / (1.0 + e21)
    w2 = g_p * e21 / (1.0 + e21)
    rt_ref[...] = jnp.where(lane == 0, w1, jnp.where(lane == 1, w2, 0.0))

    sh = int(math.log2(MOE_TILE))
    oh1 = lane == i1
    oh2 = lane == i2
    ind = jnp.where(oh1 | oh2, 1.0, 0.0)
    lrank = jnp.dot(tri_ref[...], ind.astype(BF16), preferred_element_type=F32).astype(jnp.int32)
    n_new = lrank[tm - 1:tm, :] + ind[tm - 1:tm, :].astype(jnp.int32)
    cnt = cnt_ref[...]
    cur = cur_ref[...]
    nfree = nused_ref[...]
    tiles_before = (cnt + (MOE_TILE - 1)) >> sh
    newf = ((cnt + n_new + (MOE_TILE - 1)) >> sh) - tiles_before
    newf8 = jnp.broadcast_to(newf.astype(BF16), (8, newf.shape[1]))
    pre = jnp.dot(newf8, upper_ref[...], preferred_element_type=F32)[0:1, :].astype(jnp.int32)
    new_tile = nfree + pre
    grank = cnt + lrank
    ptile = jnp.where((grank >> sh) < tiles_before, cur, new_tile)
    posfull = ((ptile << sh) + (grank & (MOE_TILE - 1))).astype(F32)
    pos1 = jnp.sum(jnp.where(oh1, posfull, 0.0), axis=-1, keepdims=True)
    pos2 = jnp.sum(jnp.where(oh2, posfull, 0.0), axis=-1, keepdims=True)
    tr = lax.broadcasted_iota(jnp.int32, (tm, tm), 0)
    tc = lax.broadcasted_iota(jnp.int32, (tm, tm), 1)
    as_row = lambda v: jnp.sum(jnp.where(tr == tc, jnp.broadcast_to(v, (tm, tm)), 0.0),
                               axis=0, keepdims=True)
    sub = lax.broadcasted_iota(jnp.int32, pos_ref.shape, 0)
    pos_ref[...] = jnp.where(sub == 0, as_row(pos1),
                             jnp.where(sub == 1, as_row(pos2), 0.0)).astype(jnp.int32)

    el_r = lax.broadcasted_iota(jnp.int32, (ROUTER_LANES, ROUTER_LANES), 0)
    el_c = lax.broadcasted_iota(jnp.int32, (ROUTER_LANES, ROUTER_LANES), 1)
    as_col = lambda v: jnp.sum(jnp.where(el_r == el_c, jnp.broadcast_to(v, el_r.shape), 0.0),
                               axis=1, keepdims=True)
    tile_col = as_col(jnp.where(newf > 0, new_tile, -1).astype(F32)).astype(jnp.int32)
    tlane = lax.broadcasted_iota(jnp.int32, (ROUTER_LANES, te_ref.shape[1]), 1)
    erow = lax.broadcasted_iota(jnp.int32, (ROUTER_LANES, te_ref.shape[1]), 0).astype(F32)
    te_ref[...] += jnp.sum(jnp.where(tlane == tile_col, erow, 0.0), axis=0,
                           keepdims=True).astype(jnp.int32)
    to_cur = jnp.minimum(n_new, (tiles_before << sh) - cnt)
    cur_col = as_col(cur.astype(F32)).astype(jnp.int32)
    added = (jnp.where(tlane == cur_col, as_col(to_cur.astype(F32)), 0.0)
             + jnp.where(tlane == tile_col, as_col((n_new - to_cur).astype(F32)), 0.0))
    fill_ref[...] += jnp.sum(added, axis=0, keepdims=True).astype(jnp.int32)
    cnt_ref[...] = cnt + n_new
    cur_ref[...] = jnp.where(newf > 0, new_tile, cur)
    nused_ref[...] = nfree + jnp.sum(newf.astype(F32), axis=-1, keepdims=True).astype(jnp.int32)


def _merge_kernel(x_ref, ya_ref, yb_ref, gates_ref, gluw_ref, glub_ref, wba_ref, wbb_ref, wout_ref,
                  fng_ref, rw_ref, rb_ref, tri_ref, upper_ref, x1_ref, rt_ref, pos_ref,
                  te_ref, fill_ref, nused_ref, xs_ref, cnt_ref, cur_ref, tbuf_even, tbuf_odd, posv,
                  pos_smem, fill_smem, zblk, rsem, psem, zsem):
    i = pl.program_id(0)
    last = pl.num_programs(0) - 1
    tm = x_ref.shape[0]
    tbufs = (tbuf_even, tbuf_odd)

    @pl.when(i == 0)
    def _():
        cnt_ref[...] = jnp.zeros_like(cnt_ref)
        cur_ref[...] = jnp.zeros_like(cur_ref)
        te_ref[...] = jnp.zeros_like(te_ref)
        fill_ref[...] = jnp.zeros_like(fill_ref)
        nused_ref[...] = jnp.zeros_like(nused_ref)

    def row_copies(par, unroll, op):
        def body(r, carry):
            for k in range(2):
                op(pltpu.make_async_copy(tbufs[par].at[pl.ds(r, 1), :],
                                         xs_ref.at[pl.ds(pos_smem[par, k, r], 1), :], rsem.at[par]))
            return carry
        lax.fori_loop(0, tm, body, 0, unroll=unroll)

    def drain_rows(par):
        for _ in range(2):
            pltpu.make_async_copy(tbufs[par], xs_ref.at[pl.ds(0, tm), :], rsem.at[par]).wait()

    def pos_to_smem(par):
        return pltpu.make_async_copy(posv, pos_smem.at[pl.ds(par, 1)], psem)

    @pl.when(i >= 1)
    def _():
        pos_to_smem(0).wait()

    def step(par, dispatch_prev):
        @pl.when(i >= 2)
        def _():
            drain_rows(par)
        if dispatch_prev:
            row_copies(1 - par, True, lambda c: c.start())
        _merge_tile_math(x_ref, ya_ref, yb_ref, gates_ref, gluw_ref, glub_ref, wba_ref, wbb_ref,
                         wout_ref, fng_ref, rw_ref, rb_ref, tri_ref, upper_ref, x1_ref,
                         tbufs[par], rt_ref, pos_ref.at[0], te_ref, fill_ref, nused_ref, cnt_ref,
                         cur_ref)
        posv[...] = pos_ref[...]
        pos_to_smem(par).start()

    pl.when(i == 0)(lambda: step(0, False))
    for par in range(2):
        pl.when((i > 0) & (i % 2 == par))(functools.partial(step, par, True))

    @pl.when(i == last)
    def _():
        pos_to_smem(0).wait()
        for par in range(2):
            @pl.when(i % 2 == par)
            def _(par=par):
                row_copies(par, DMA_UNROLL, lambda c: c.start())

                @pl.when(i >= 1)
                def _():
                    drain_rows(1 - par)
                drain_rows(par)

        fill_copy = pltpu.make_async_copy(fill_ref, fill_smem, psem)
        fill_copy.start()
        zblk[...] = jnp.zeros_like(zblk)
        fill_copy.wait()

        def over_tails(op):
            def tile(j, carry):
                fill = fill_smem[0, j]
                head = (-fill) & (SUBLANES - 1)
                for r in range(SUBLANES - 1):
                    @pl.when(r < head)
                    def _(r=r):
                        op(pltpu.make_async_copy(zblk.at[pl.ds(0, 1), :],
                                                 xs_ref.at[pl.ds(j * MOE_TILE + fill + r, 1), :], zsem))
                start = fill + head
                todo = MOE_TILE - start
                size = MOE_TILE
                while size >= SUBLANES:
                    @pl.when((todo & size) != 0)
                    def _(start=start, size=size):
                        off = pl.multiple_of(j * MOE_TILE + start, SUBLANES)
                        op(pltpu.make_async_copy(zblk.at[pl.ds(0, size), :],
                                                 xs_ref.at[pl.ds(off, size), :], zsem))
                    start = start + (todo & size)
                    size //= 2
                return carry
            lax.fori_loop(0, xs_ref.shape[0] // MOE_TILE, tile, 0)

        over_tails(lambda c: c.start())
        over_tails(lambda c: c.wait())


def _moe_tiles(t):
    return (2 * t) // MOE_TILE + N_EXPERTS


def _merge(x2, ya, yb, gates, gluw, glub, wba, wbb, wout, fng, rw, rb):
    t, d = x2.shape
    tm = MERGE_TILE
    assert t % tm == 0 and tm <= MOE_TILE
    n_tiles = _moe_tiles(t)
    te_lanes = -(-n_tiles // LANES) * LANES
    rr = jnp.arange(tm)
    tri = (rr[None, :] < rr[:, None]).astype(BF16)
    ll = jnp.arange(ROUTER_LANES)
    upper = (ll[:, None] < ll[None, :]).astype(BF16)
    full = lambda a: pl.BlockSpec(a.shape, lambda i: (0,) * a.ndim)
    rowblk = lambda n: pl.BlockSpec((tm, n), lambda i: (i, 0))
    fixed = lambda n: pl.BlockSpec((1, n), lambda i: (0, 0))
    return pl.pallas_call(
        _merge_kernel,
        grid=(t // tm,),
        in_specs=[rowblk(d), rowblk(ya.shape[1]), rowblk(yb.shape[1]), rowblk(gates.shape[1]),
                  full(gluw), full(glub), full(wba), full(wbb), full(wout), full(fng), full(rw),
                  full(rb), full(tri), full(upper)],
        out_specs=[rowblk(d), rowblk(ROUTER_LANES),
                   pl.BlockSpec((1, SUBLANES, tm), lambda i: (i, 0, 0)),
                   fixed(te_lanes), fixed(te_lanes), fixed(ROUTER_LANES),
                   pl.BlockSpec(memory_space=pl.ANY)],
        out_shape=[jax.ShapeDtypeStruct((t, d), F32),
                   jax.ShapeDtypeStruct((t, ROUTER_LANES), F32),
                   jax.ShapeDtypeStruct((t // tm, SUBLANES, tm), jnp.int32),
                   jax.ShapeDtypeStruct((1, te_lanes), jnp.int32),
                   jax.ShapeDtypeStruct((1, te_lanes), jnp.int32),
                   jax.ShapeDtypeStruct((1, ROUTER_LANES), jnp.int32),
                   jax.ShapeDtypeStruct((n_tiles * MOE_TILE, d), F32)],
        scratch_shapes=[pltpu.VMEM((1, ROUTER_LANES), jnp.int32),
                        pltpu.VMEM((1, ROUTER_LANES), jnp.int32),
                        pltpu.VMEM((tm, d), F32),
                        pltpu.VMEM((tm, d), F32),
                        pltpu.VMEM((1, SUBLANES, tm), jnp.int32),
                        pltpu.SMEM((2, SUBLANES, tm), jnp.int32),
                        pltpu.SMEM((1, te_lanes), jnp.int32),
                        pltpu.VMEM((MOE_TILE, d), F32),
                        pltpu.SemaphoreType.DMA((2,)),
                        pltpu.SemaphoreType.DMA(()),
                        pltpu.SemaphoreType.DMA(())],
        compiler_params=pltpu.CompilerParams(dimension_semantics=("arbitrary",),
                                             vmem_limit_bytes=VMEM_LIMIT),
        name="merge",
    )(x2, ya, yb, gates, gluw, glub, wba, wbb, wout, fng, rw, rb, tri, upper)


def _experts_kernel(te_ref, nused_ref, xs_ref, wg_ref, wu_ref, wd_ref, y_ref):
    j = pl.program_id(0)

    @pl.when(j < nused_ref[0])
    def _():
        x = xs_ref[...].astype(BF16)
        hg = _mm(x, wg_ref[...])
        hid = hg * _sigmoid(hg) * _mm(x, wu_ref[...])
        y_ref[...] = _mm(hid, wd_ref[...])

    @pl.when(j >= nused_ref[0])
    def _():
        y_ref[...] = jnp.zeros_like(y_ref)


def _experts(tile_expert, n_used, xs, wg, wu, wd):
    n_rows, hw = xs.shape
    ne, d, de = wg.shape
    return pl.pallas_call(
        _experts_kernel,
        grid_spec=pltpu.PrefetchScalarGridSpec(
            num_scalar_prefetch=2,
            grid=(n_rows // MOE_TILE,),
            in_specs=[pl.BlockSpec((MOE_TILE, hw), lambda j, te, nu: (j, 0)),
                      pl.BlockSpec((None, d, de), lambda j, te, nu: (te[j], 0, 0)),
                      pl.BlockSpec((None, d, de), lambda j, te, nu: (te[j], 0, 0)),
                      pl.BlockSpec((None, de, d), lambda j, te, nu: (te[j], 0, 0))],
            out_specs=pl.BlockSpec((MOE_TILE, hw), lambda j, te, nu: (j, 0))),
        out_shape=jax.ShapeDtypeStruct((n_rows, hw), F32),
        compiler_params=pltpu.CompilerParams(dimension_semantics=("arbitrary",),
                                             vmem_limit_bytes=VMEM_LIMIT),
        name="experts",
    )(tile_expert, n_used, xs, wg, wu, wd)


def _ple_kernel(pos1_ref, pos2_ref, x_ref, rt_ref, p_ref, png_ref, wg_ref, wp_ref, fng_ref, y_ref,
                o_ref, ybuf_even, ybuf_odd, sem):
    i = pl.program_id(0)
    tm = x_ref.shape[0]
    ybufs = (ybuf_even, ybuf_odd)

    def gather(tile, par, unroll):
        def body(r, carry):
            for k, pos_ref in enumerate((pos1_ref, pos2_ref)):
                pltpu.make_async_copy(y_ref.at[pl.ds(pos_ref[tile * tm + r], 1), :],
                                      ybufs[par].at[k, pl.ds(r, 1), :], sem.at[par]).start()
            return carry
        lax.fori_loop(0, tm, body, 0, unroll=unroll)

    @pl.when(i == 0)
    def _():
        gather(i, 0, DMA_UNROLL)

    def step(par, prefetch_next):
        ybuf = ybufs[par]
        for k in range(2):
            pltpu.make_async_copy(y_ref.at[pl.ds(0, tm), :], ybuf.at[k], sem.at[par]).wait()
        if prefetch_next:
            gather(i + 1, 1 - par, True)
        rt = rt_ref[...]
        x2 = x_ref[...] + rt[:, 0:1] * ybuf[0] + rt[:, 1:2] * ybuf[1]
        hp = _rms_norm(x2, png_ref[...])
        gate = _sigmoid(_mm(hp, wg_ref[...]))
        x3 = x2 + gate * _mm(p_ref[...], wp_ref[...])
        o_ref[...] = _rms_norm(x3, fng_ref[...])

    last = pl.num_programs(0) - 1
    for par in range(2):
        pl.when((i % 2 == par) & (i < last))(functools.partial(step, par, True))
        pl.when((i % 2 == par) & (i == last))(functools.partial(step, par, False))


def _ple(pos1, pos2, x1, rt, p2, png, wg, wp, fng, y_pack, tm):
    t, d = x1.shape
    hw = y_pack.shape[1]
    full = lambda a: pl.BlockSpec(a.shape, lambda i, p1, p2_: (0,) * a.ndim)
    rowblk = lambda n: pl.BlockSpec((tm, n), lambda i, p1, p2_: (i, 0))
    return pl.pallas_call(
        _ple_kernel,
        grid_spec=pltpu.PrefetchScalarGridSpec(
            num_scalar_prefetch=2,
            grid=(t // tm,),
            in_specs=[rowblk(d), rowblk(rt.shape[1]), rowblk(p2.shape[1]),
                      full(png), full(wg), full(wp), full(fng),
                      pl.BlockSpec(memory_space=pl.ANY)],
            out_specs=rowblk(d),
            scratch_shapes=[pltpu.VMEM((2, tm, hw), F32),
                            pltpu.VMEM((2, tm, hw), F32),
                            pltpu.SemaphoreType.DMA((2,))]),
        out_shape=jax.ShapeDtypeStruct((t, d), F32),
        compiler_params=pltpu.CompilerParams(dimension_semantics=("arbitrary",),
                                             vmem_limit_bytes=VMEM_LIMIT),
        name="ple",
    )(pos1, pos2, x1, rt, p2, png, wg, wp, fng, y_pack)


def _row_tile(t, want):
    tm = min(want, t)
    while t % tm:
        tm //= 2
    return tm


def _layer(x, p, mix_norm, w_in, mu_shift, rk_w0, rk_w_up, rk_a0, rk_a_up, rk_g_up,
           rk_k_k, rk_k_a, rk_r_k, rk_ln_g, rk_ln_b, s5_lam_re, s5_lam_im, s5_log_dt,
           s5_b_re, s5_b_im, s5_c_re, s5_c_im, s5_d, s5_glu_w, s5_glu_b,
           w_branch_a, w_branch_b, w_out, ffn_norm, router_group_w, router_group_b,
           router_expert_w, router_expert_b, exp_w_gate, exp_w_up, exp_w_down,
           ple_norm, ple_gate_w, ple_proj):
    b, s, d = x.shape
    t = b * s
    W = RWKV_WIDTH
    row = lambda a: a.reshape(1, -1).astype(F32)
    x2 = x.reshape(t, d)

    crw, us5, gates = _in_proj(x2, row(mix_norm), w_in.astype(BF16), _row_tile(t, 512))

    wl = jnp.zeros((LORA_COLS, 3 * W), F32)
    wl = wl.at[:DECAY_LORA, :W].set(rk_w_up)
    wl = wl.at[DECAY_LORA:DECAY_LORA + AAA_LORA, W:2 * W].set(rk_a_up)
    wl = wl.at[DECAY_LORA + AAA_LORA:, 2 * W:].set(rk_g_up)
    hid = jnp.arange(LANES) // RWKV_HEAD_DIM
    ones_bd = (hid[:, None] == hid[None, :]).astype(BF16)
    ya = _rwkv(crw.reshape(b, s, RWKV_COLS), row(mu_shift), wl.astype(BF16), row(rk_w0), row(rk_a0),
               row(rk_k_k), row(rk_k_a), row(rk_r_k), row(rk_ln_g), row(rk_ln_b), ones_bd,
               _rwkv_cum_matrix(_row_tile(s, RWKV_BLOCK)))

    s5_wa, s5_wo, plr, pli = _s5_mats(s5_lam_re, s5_lam_im, s5_log_dt, s5_b_re, s5_b_im,
                                      s5_c_re, s5_c_im, s5_d)
    yb = _s5(us5.reshape(b, s // S5_CHUNK, S5_CHUNK, S5_WIDTH), s5_wa, s5_wo, plr, pli)

    rw = jnp.zeros((d, ROUTER_LANES), F32)
    rw = rw.at[:, :N_EXPERTS].set(router_expert_w).at[:, N_EXPERTS:N_EXPERTS + N_GROUPS].set(router_group_w)
    rb = jnp.zeros((1, ROUTER_LANES), F32)
    rb = rb.at[0, :N_EXPERTS].set(router_expert_b).at[0, N_EXPERTS:N_EXPERTS + N_GROUPS].set(router_group_b)
    rw_hi = rw.astype(BF16)
    rw = jnp.concatenate([rw_hi, (rw - rw_hi.astype(F32)).astype(BF16)], axis=1)
    x1, rt, pos, tile_expert, _, n_used, xs = _merge(
        x2, ya.reshape(t, W), yb.reshape(t, S5_WIDTH), gates, s5_glu_w.astype(BF16), row(s5_glu_b),
        w_branch_a.astype(BF16), w_branch_b.astype(BF16), w_out.astype(BF16), row(ffn_norm), rw, rb)
    n_tiles = _moe_tiles(t)
    pos1, pos2 = pos[:, 0, :].reshape(t), pos[:, 1, :].reshape(t)
    y_pack = _experts(tile_expert[0, :n_tiles], n_used[0, :1], xs, exp_w_gate, exp_w_up, exp_w_down)
    return (pos1, pos2, x1, rt, p.reshape(t, -1), row(ple_norm), ple_gate_w.astype(BF16),
            ple_proj.astype(BF16), y_pack)


def kernel(x, p, mix_norm, w_in, mu_shift, rk_w0, rk_w_up, rk_a0, rk_a_up, rk_g_up, rk_k_k, rk_k_a,
           rk_r_k, rk_ln_g, rk_ln_b, s5_lam_re, s5_lam_im, s5_log_dt, s5_b_re, s5_b_im, s5_c_re,
           s5_c_im, s5_d, s5_glu_w, s5_glu_b, w_branch_a, w_branch_b, w_out, ffn_norm,
           router_group_w, router_group_b, router_expert_w, router_expert_b, exp_w_gate, exp_w_up,
           exp_w_down, ple_norm, ple_gate_w, ple_proj, final_norm):
    b, s, d = x.shape
    depth = w_in.shape[0]
    assert depth == 1, "the final norm is fused into the last layer's PLE kernel"
    i = 0
    pos1, pos2, x1, rt, p2, png, wpg, wpp, y_pack = _layer(
        x, p[i], mix_norm[i], w_in[i], mu_shift[i], rk_w0[i], rk_w_up[i], rk_a0[i], rk_a_up[i],
        rk_g_up[i], rk_k_k[i], rk_k_a[i], rk_r_k[i], rk_ln_g[i], rk_ln_b[i], s5_lam_re[i],
        s5_lam_im[i], s5_log_dt[i], s5_b_re[i], s5_b_im[i], s5_c_re[i], s5_c_im[i], s5_d[i],
        s5_glu_w[i], s5_glu_b[i], w_branch_a[i], w_branch_b[i], w_out[i], ffn_norm[i],
        router_group_w[i], router_group_b[i], router_expert_w[i], router_expert_b[i],
        exp_w_gate[i], exp_w_up[i], exp_w_down[i], ple_norm[i], ple_gate_w[i], ple_proj[i])
    out = _ple(pos1, pos2, x1, rt, p2, png, wpg, wpp, final_norm.reshape(1, -1).astype(F32), y_pack,
               _row_tile(b * s, MERGE_TILE))
    return out.reshape(b, s, d)
```

```python
import functools
import math

import jax
import jax.numpy as jnp
from jax import lax
from jax.experimental import pallas as pl
from jax.experimental.pallas import tpu as pltpu

F32 = jnp.float32
BF16 = jnp.bfloat16

NORM_EPS = 1e-6
GN_EPS = 64e-5

RWKV_HEADS = 8
RWKV_HEAD_DIM = 64
RWKV_WIDTH = RWKV_HEADS * RWKV_HEAD_DIM
DECAY_LORA = 64
AAA_LORA = 64
GATE_LORA = 128
LORA_COLS = DECAY_LORA + AAA_LORA + GATE_LORA
RWKV_COLS = 3 * RWKV_WIDTH + LORA_COLS
S5_GROUPS = 16
S5_GROUP_CH = 16
S5_WIDTH = S5_GROUPS * S5_GROUP_CH
S5_STATE = 64
S5_ZW = 2 * S5_GROUPS * S5_STATE
N_GROUPS = 4
EXPERTS_PER_GROUP = 8
N_EXPERTS = N_GROUPS * EXPERTS_PER_GROUP

LANES = 128
SUBLANES = 8
RWKV_CHUNK = 64
RWKV_INV_BLOCK = 16
RWKV_BLOCK = 256
S5_CHUNK = 8
ROUTER_LANES = 128
MOE_TILE = 512
MERGE_TILE = 512
DMA_UNROLL = 8
ROW_DMA_PRIORITY = 1
VMEM_LIMIT = 56 * 1024 * 1024


def _mm(a, b):
    return jnp.dot(a.astype(BF16), b.astype(BF16), preferred_element_type=F32)


def _mm_nt(a, b):
    return lax.dot_general(a.astype(BF16), b.astype(BF16), (((1,), (1,)), ((), ())),
                           preferred_element_type=F32)


def _mm_tn(a, b):
    return lax.dot_general(a.astype(BF16), b.astype(BF16), (((0,), (0,)), ((), ())),
                           preferred_element_type=F32)


def _split2(x):
    hi = x.astype(BF16)
    return hi, (x - hi.astype(F32)).astype(BF16)


def _mm_split_lhs(a_bf16, x):
    hi, lo = _split2(x)
    d = lambda t: jnp.dot(a_bf16, t, preferred_element_type=F32)
    return d(hi) + d(lo)


def _sigmoid(x):
    return 0.5 * jnp.tanh(0.5 * x) + 0.5


def _rms_norm(x, g):
    ms = jnp.mean(x * x, axis=-1, keepdims=True)
    return x * lax.rsqrt(ms + NORM_EPS) * g


def _in_proj_kernel(x_ref, g_ref, w_ref, crw_ref, us5_ref, gates_ref):
    h = _rms_norm(x_ref[...], g_ref[...]).astype(BF16)
    c0, c1 = RWKV_COLS, RWKV_COLS + S5_WIDTH
    crw_ref[...] = jnp.dot(h, w_ref[:, :c0], preferred_element_type=F32)
    us5_ref[...] = jnp.dot(h, w_ref[:, c0:c1], preferred_element_type=F32)
    gates_ref[...] = jnp.dot(h, w_ref[:, c1:], preferred_element_type=F32).astype(BF16)


def _in_proj(x2, g, w, tm):
    t, d = x2.shape
    n = w.shape[1]
    ng = n - RWKV_COLS - S5_WIDTH
    return pl.pallas_call(
        _in_proj_kernel,
        grid=(t // tm,),
        in_specs=[pl.BlockSpec((tm, d), lambda i: (i, 0)),
                  pl.BlockSpec((1, d), lambda i: (0, 0)),
                  pl.BlockSpec((d, n), lambda i: (0, 0))],
        out_specs=[pl.BlockSpec((tm, RWKV_COLS), lambda i: (i, 0)),
                   pl.BlockSpec((tm, S5_WIDTH), lambda i: (i, 0)),
                   pl.BlockSpec((tm, ng), lambda i: (i, 0))],
        out_shape=[jax.ShapeDtypeStruct((t, RWKV_COLS), F32),
                   jax.ShapeDtypeStruct((t, S5_WIDTH), F32),
                   jax.ShapeDtypeStruct((t, ng), BF16)],
        compiler_params=pltpu.CompilerParams(dimension_semantics=("arbitrary",),
                                             vmem_limit_bytes=VMEM_LIMIT),
        name="in_proj",
    )(x2, g, w)


def _rwkv_kernel(c_ref, mu_ref, wl_ref, w0_ref, a0_ref, kk_ref, ka_ref, rk_ref, lng_ref, lnb_ref,
                 ones_ref, cum_ref, o_ref, carry_ref, s_ref):
    C = RWKV_CHUNK
    W = RWKV_WIDTH
    npairs = W // LANES

    @pl.when(pl.program_id(1) == 0)
    def _():
        carry_ref[...] = jnp.zeros_like(carry_ref)
        s_ref[...] = jnp.zeros_like(s_ref)

    c = c_ref[...]
    R = c.shape[0]
    nchunks = R // C
    row = lax.broadcasted_iota(jnp.int32, (R, 1), 0)
    prev = jnp.where(row == 0, carry_ref[...], pltpu.roll(c, 1, 0))
    carry_ref[...] = c[R - 1:R, :]
    cs = c + (prev - c) * mu_ref[...]

    r = cs[:, 0:W]
    k = cs[:, W:2 * W]
    v = cs[:, 2 * W:3 * W]
    lin = cs[:, 3 * W:]
    llane = lax.broadcasted_iota(jnp.int32, lin.shape, 1)
    lact = jnp.where(llane < DECAY_LORA, jnp.tanh(lin),
                     jnp.where(llane < DECAY_LORA + AAA_LORA, lin, _sigmoid(lin)))
    lo = _mm(lact, wl_ref[...])
    zw = -(w0_ref[...] + lo[:, 0:W])
    softplus = jnp.maximum(zw, 0.0) + jnp.log(1.0 + jnp.exp(-jnp.abs(zw)))
    ld = -jnp.exp(-softplus - 0.5)
    a = _sigmoid(a0_ref[...] + lo[:, W:2 * W])
    g = lo[:, 2 * W:3 * W]

    ones_bd = ones_ref[...]
    segsum = lambda t: jnp.concatenate(
        [_mm(t[:, p * LANES:(p + 1) * LANES], ones_bd) for p in range(npairs)], axis=1)
    kk = k * kk_ref[...]
    kkn = kk / jnp.maximum(jnp.sqrt(segsum(kk * kk)), 1e-12)
    kmod = k * (1.0 + (a - 1.0) * ka_ref[...])

    cums = _mm_split_lhs(cum_ref[...], ld)
    cum = cums[:R]
    tot = cums[R:]
    inv = jnp.exp(-cum)
    tail = jnp.exp(tot - cum)
    At = -kkn * jnp.exp(cum - ld)
    Rt = r * jnp.exp(cum)
    kka = kkn * a
    Bt = kka * inv
    Kt = kmod * inv
    Bend = kka * tail
    Kend = kmod * tail
    pc = jnp.exp(tot)

    lane = lax.broadcasted_iota(jnp.int32, (C, LANES), 1)
    h0 = lane < RWKV_HEAD_DIM
    split = lambda t: jnp.concatenate([jnp.where(h0, t, 0.0), jnp.where(h0, 0.0, t)], axis=0)
    grow = lax.broadcasted_iota(jnp.int32, (C, 4 * C), 0)
    gcol = lax.broadcasted_iota(jnp.int32, (C, 4 * C), 1) & (C - 1)
    r2 = lax.broadcasted_iota(jnp.int32, (2 * C, 2 * C), 0)
    c2 = lax.broadcasted_iota(jnp.int32, (2 * C, 2 * C), 1)
    eye = (r2 == c2).astype(F32)
    blk_shift = int(math.log2(RWKV_INV_BLOCK))
    same_blk = (r2 >> blk_shift) == (c2 >> blk_shift)
    same_head = (r2 < C) == (c2 < C)
    same_head2 = jnp.concatenate([same_head, same_head], axis=0)
    zeros_c = jnp.zeros((C, LANES), F32)
    zeros_2c = jnp.zeros((2 * C, LANES), F32)

    units = [(ci, p) for ci in range(nchunks) for p in range(npairs)]
    blk = lambda t, u: t[u[0] * C:(u[0] + 1) * C, u[1] * LANES:(u[1] + 1) * LANES]
    each = lambda f, *ls: [f(*xs) for xs in zip(*ls)]

    lhs = [jnp.concatenate([blk(At, u), blk(Rt, u)], axis=0) for u in units]
    rhs = [jnp.concatenate([split(blk(Bt, u)), split(blk(Kt, u))], axis=0) for u in units]
    G = each(_mm_nt, lhs, rhs)
    a_row = [jnp.where(gcol < grow, t[:C], 0.0) for t in G]
    m_row = [jnp.where(gcol <= grow, t[C:], 0.0) for t in G]
    a_bd = [split(t[:, :2 * C]) for t in a_row]

    a_d = [jnp.where(same_blk, t, 0.0) for t in a_bd]
    a_off = each(lambda x, y: x - y, a_bd, a_d)
    dinv = [eye + t for t in a_d]
    pw = a_d
    for _ in range(blk_shift - 1):
        pw = each(_mm, pw, pw)
        dinv = each(lambda x, y: x + _mm(x, y), dinv, pw)
    n1 = each(_mm, dinv, a_off)
    n2 = each(_mm, n1, n1)
    n3 = each(_mm, n1, n2)
    tinv = each(lambda x1, x2, x3, dv: _mm(eye + x1 + x2 + x3, dv), n1, n2, n3, dinv)

    vp = [blk(v, u) for u in units]
    v_st = [split(t) for t in vp]
    rhs0 = each(lambda ar, vs: _mm(ar[:, 2 * C:], vs), a_row, v_st)
    wu = each(lambda t, l, r0: _mm(t, jnp.concatenate([split(l[:C]), split(r0)], axis=1)),
              tinv, lhs, rhs0)
    wu_lp = [t[:C] + t[C:] for t in wu]
    mn = each(lambda w_, v_, u: _mm_tn(
        jnp.concatenate([w_, jnp.concatenate([zeros_c, v_], axis=1)], axis=0),
        jnp.concatenate([blk(Bend, u), blk(Kend, u)], axis=0)), wu_lp, vp, units)
    mn = [jnp.where(same_head2, t, 0.0) for t in mn]
    qy = each(lambda m_, w_, vs: _mm(m_, jnp.concatenate(
        [w_, jnp.concatenate([zeros_2c, vs], axis=1)], axis=0)), m_row, wu, v_st)

    ys = [[None] * npairs for _ in range(nchunks)]
    states = [s_ref[p] for p in range(npairs)]
    for i, (ci, p) in enumerate(units):
        s_old = states[p]
        q = lhs[i][C:] + qy[i][:, :LANES]
        ys[ci][p] = _mm_nt(q, s_old) + qy[i][:, LANES:]
        states[p] = s_old * blk(pc, (ci, p))[0:1, :] + _mm(s_old, mn[i][:LANES]) + mn[i][LANES:]
    for p in range(npairs):
        s_ref[p] = states[p]

    y = jnp.concatenate([jnp.concatenate(t, axis=1) for t in ys], axis=0)
    inv_n = 1.0 / RWKV_HEAD_DIM
    mean = segsum(y) * inv_n
    d = y - mean
    var = segsum(d * d) * inv_n
    yn = d * lax.rsqrt(var + GN_EPS) * lng_ref[...] + lnb_ref[...]
    bonus = segsum(r * kmod * rk_ref[...]) * v
    o_ref[...] = ((yn + bonus) * g).astype(o_ref.dtype)


def _rwkv_cum_matrix(rows):
    t = jnp.arange(rows)
    same = (t[:, None] // RWKV_CHUNK) == (t[None, :] // RWKV_CHUNK)
    return jnp.concatenate([same & (t[None, :] <= t[:, None]), same], axis=0).astype(BF16)


def _rwkv(crw3, mu, wl, w0, a0, k_k, k_a, r_k, ln_g, ln_b, ones_bd, cum_mat):
    b, s, _ = crw3.shape
    R = cum_mat.shape[1]
    W = RWKV_WIDTH
    vec = lambda n: pl.BlockSpec((1, n), lambda i, j: (0, 0))
    return pl.pallas_call(
        _rwkv_kernel,
        grid=(b, s // R),
        in_specs=[pl.BlockSpec((None, R, RWKV_COLS), lambda i, j: (i, j, 0)),
                  vec(RWKV_COLS),
                  pl.BlockSpec((LORA_COLS, 3 * W), lambda i, j: (0, 0)),
                  vec(W), vec(W), vec(W), vec(W), vec(W), vec(W), vec(W),
                  pl.BlockSpec((LANES, LANES), lambda i, j: (0, 0)),
                  pl.BlockSpec((2 * R, R), lambda i, j: (0, 0))],
        out_specs=pl.BlockSpec((None, R, W), lambda i, j: (i, j, 0)),
        out_shape=jax.ShapeDtypeStruct((b, s, W), BF16),
        scratch_shapes=[pltpu.VMEM((1, RWKV_COLS), F32),
                        pltpu.VMEM((W // LANES, LANES, LANES), F32)],
        compiler_params=pltpu.CompilerParams(dimension_semantics=("arbitrary", "arbitrary"),
                                             vmem_limit_bytes=VMEM_LIMIT),
        name="rwkv",
    )(crw3, mu, wl, w0, a0, k_k, k_a, r_k, ln_g, ln_b, ones_bd, cum_mat)


def _s5_expand_kernel(are_ref, aim_ref, kf_ref, of_ref, wa_ref, wo_ref, *, groups, state, chans):
    rows = are_ref.shape[0]
    base = pl.program_id(0) * rows
    r = base + lax.broadcasted_iota(jnp.int32, (rows, LANES), 0)
    lane = lax.broadcasted_iota(jnp.int32, (rows, LANES), 1)
    in_group = (r // chans) % groups
    st_group = (r // state) % groups

    def emit(dst_ref, col0, src, row_group, per_group):
        per_tile = LANES // per_group
        tiles = groups // per_tile
        for m in range(src.shape[1] // LANES):
            vals = src[:, m * LANES:(m + 1) * LANES]
            for tl in range(tiles):
                c = col0 + (m * tiles + tl) * LANES
                keep = row_group == tl * per_tile + lane // per_group
                dst_ref[:, c:c + LANES] = jnp.where(keep, vals, 0.0).astype(dst_ref.dtype)

    half = groups * state
    emit(wa_ref, 0, are_ref[...], in_group, state)
    emit(wa_ref, half, aim_ref[...], in_group, state)
    emit(wa_ref, 2 * half, kf_ref[...], in_group, chans)
    emit(wo_ref, 0, of_ref[...], st_group, chans)


def _s5_expand(a_re, a_im, kf, of, groups, state, chans):
    rows = a_re.shape[0]
    assert kf.shape[0] == rows and of.shape[0] == rows
    tr = _row_tile(rows, 256)
    n_a = 2 * groups * state + kf.shape[1] // LANES * groups * chans
    n_o = of.shape[1] // LANES * groups * chans
    blk = lambda a: pl.BlockSpec((tr, a.shape[1]), lambda i: (i, 0))
    return pl.pallas_call(
        functools.partial(_s5_expand_kernel, groups=groups, state=state, chans=chans),
        grid=(rows // tr,),
        in_specs=[blk(a_re), blk(a_im), blk(kf), blk(of)],
        out_specs=[pl.BlockSpec((tr, n_a), lambda i: (i, 0)), pl.BlockSpec((tr, n_o), lambda i: (i, 0))],
        out_shape=[jax.ShapeDtypeStruct((rows, n_a), BF16), jax.ShapeDtypeStruct((rows, n_o), BF16)],
        compiler_params=pltpu.CompilerParams(dimension_semantics=("arbitrary",),
                                             vmem_limit_bytes=VMEM_LIMIT),
        name="s5_expand",
    )(a_re, a_im, kf, of)


def _s5_mats(lam_re, lam_im, log_dt, b_re, b_im, c_re, c_im, d_skip):
    L = S5_CHUNK
    G, N = lam_re.shape
    ch = b_re.shape[-1]
    dt = jnp.exp(log_dt)[:, None]
    lr, li = lam_re, lam_im
    mag = jnp.exp(lr * dt)
    lb_re, lb_im = mag * jnp.cos(li * dt), mag * jnp.sin(li * dt)
    den = lr * lr + li * li
    nr, ni = lb_re - 1.0, lb_im
    coef_re = (nr * lr + ni * li) / den
    coef_im = (ni * lr - nr * li) / den
    bb_re = coef_re[..., None] * b_re - coef_im[..., None] * b_im
    bb_im = coef_re[..., None] * b_im + coef_im[..., None] * b_re
    prs, pis = [jnp.ones_like(lb_re)], [jnp.zeros_like(lb_im)]
    for _ in range(L):
        pr_, pi_ = prs[-1], pis[-1]
        prs.append(pr_ * lb_re - pi_ * lb_im)
        pis.append(pr_ * lb_im + pi_ * lb_re)
    pr = jnp.stack(prs)
    pi = jnp.stack(pis)
    hp = lax.Precision.HIGHEST

    def lam_bb(qr, qi):
        return (qr[..., None] * bb_re[None] - qi[..., None] * bb_im[None],
                qr[..., None] * bb_im[None] + qi[..., None] * bb_re[None])

    def repeat_cols(a, inner, reps):
        k = a.shape[-1]
        src = jnp.arange(k)[:, None]
        dst = jnp.arange(k * reps)[None, :]
        rep = ((src // inner == dst // (inner * reps)) & (src % inner == dst % inner)).astype(F32)
        return jnp.dot(a, rep, precision=hp)

    in_rows = L * G * ch
    st_rows = 2 * G * N
    wre, wim = lam_bb(pr[:L][::-1], pi[:L][::-1])
    a_in = [repeat_cols(jnp.swapaxes(part, 2, 3).reshape(in_rows, N), N, LANES // N)
            for part in (wre, wim)]
    lre, lim = lam_bb(pr[:L], pi[:L])
    kern = (jnp.einsum('gon,lgni->lgio', c_re, lre, precision=hp)
            - jnp.einsum('gon,lgni->lgio', c_im, lim, precision=hp))
    kern = kern.at[0].add(d_skip[:, :, None] * jnp.eye(ch, dtype=F32)[None])
    lag = jnp.arange(L)[None, :] - jnp.arange(L)[:, None]
    kst = jnp.where((lag >= 0)[:, :, None, None, None], kern[jnp.maximum(lag, 0)], 0.0)
    kf = jnp.transpose(kst, (0, 2, 3, 1, 4)).reshape(in_rows, L * ch)
    kf = repeat_cols(kf, ch, LANES // ch)
    qr, qi = pr[1:L + 1][:, :, None, :], pi[1:L + 1][:, :, None, :]
    o_re = c_re[None] * qr - c_im[None] * qi
    o_im = -c_re[None] * qi - c_im[None] * qr
    of = jnp.transpose(jnp.stack([o_re, o_im]), (0, 2, 4, 1, 3))
    of = repeat_cols(of.reshape(st_rows, L * ch), ch, LANES // ch)
    w_a, w_out_flat = _s5_expand(a_in[0], a_in[1], kf, of, G, N, ch)
    plr = pr[L].reshape(1, G * N)
    pli = pi[L].reshape(1, G * N)
    return w_a, w_out_flat, plr, pli


def _s5_kernel(u_ref, wa_ref, wo_ref, plr_ref, pli_ref, o_ref, wloc_ref, zprev_ref):
    nch, L, w = u_ref.shape
    half = S5_ZW // 2

    r = _mm(u_ref[:, 0, :], wa_ref[0:w, :])
    for j in range(1, L):
        r = r + _mm(u_ref[:, j, :], wa_ref[j * w:(j + 1) * w, :])
    wloc_ref[...] = r[:, :S5_ZW]
    y_lag = r[:, S5_ZW:]

    plr = plr_ref[...]
    pli = pli_ref[...]

    def step(ci, z):
        zprev_ref[pl.ds(ci, 1), :] = z
        zr, zi = z[:, :half], z[:, half:]
        nz = jnp.concatenate([plr * zr - pli * zi, plr * zi + pli * zr], axis=1)
        return nz + wloc_ref[pl.ds(ci, 1), :]

    lax.fori_loop(0, nch, step, jnp.zeros((1, S5_ZW), F32))
    y = y_lag + _mm(zprev_ref[...], wo_ref[...])
    for j in range(L):
        o_ref[:, j, :] = y[:, j * w:(j + 1) * w]


def _s5(u4, w_a, w_o, plr, pli):
    b, nch, L, w = u4.shape
    const = lambda a: pl.BlockSpec(a.shape, lambda i: (0,) * a.ndim, pipeline_mode=pl.Buffered(1))
    blk = pl.BlockSpec((None, nch, L, w), lambda i: (i, 0, 0, 0))
    return pl.pallas_call(
        _s5_kernel,
        grid=(b,),
        in_specs=[blk, const(w_a), const(w_o), const(plr), const(pli)],
        out_specs=blk,
        out_shape=jax.ShapeDtypeStruct(u4.shape, F32),
        scratch_shapes=[pltpu.VMEM((nch, S5_ZW), F32),
                        pltpu.VMEM((nch, S5_ZW), F32)],
        compiler_params=pltpu.CompilerParams(dimension_semantics=("arbitrary",),
                                             vmem_limit_bytes=VMEM_LIMIT),
        name="s5",
    )(u4, w_a, w_o, plr, pli)


def _merge_tile_math(x_ref, ya_ref, yb_ref, gates_ref, gluw_ref, glub_ref, wba_ref, wbb_ref, wout_ref,
                     fng_ref, rw_ref, rb_ref, tri_ref, upper_ref, x1_ref, t_ref, rt_ref, pos_ref,
                     te_ref, fill_ref, nused_ref, cnt_ref, cur_ref):
    d = x_ref.shape[1]
    tm = x_ref.shape[0]
    y_a = jnp.dot(ya_ref[...], wba_ref[...], preferred_element_type=F32)
    ys = yb_ref[...]
    z = 0.5 * ys * (1.0 + jnp.tanh(math.sqrt(2.0 / math.pi) * (ys + 0.044715 * (ys * ys * ys))))
    z = z * _sigmoid(_mm(z, gluw_ref[...]) + glub_ref[...])
    y_b = _mm(z, wbb_ref[...])
    gates = gates_ref[...].astype(F32)
    merged = _sigmoid(gates[:, :d]) * y_a + _sigmoid(gates[:, d:]) * y_b
    x1 = x_ref[...] + _mm(merged, wout_ref[...])
    x1_ref[...] = x1
    t = _rms_norm(x1, fng_ref[...])
    t_hi = t.astype(BF16)
    t_ref[...] = t

    t_lo = (t - t_hi.astype(F32)).astype(BF16)
    hh_hl = jnp.dot(t_hi, rw_ref[...], preferred_element_type=F32)
    lh = jnp.dot(t_lo, rw_ref[:, :ROUTER_LANES], preferred_element_type=F32)
    logits = hh_hl[:, :ROUTER_LANES] + hh_hl[:, ROUTER_LANES:] + lh + rb_ref[...]
    lane = lax.broadcasted_iota(jnp.int32, logits.shape, 1)
    neg = -jnp.inf
    lane_f = lane.astype(F32)
    big = float(1 << 20)
    is_g = (lane >= N_EXPERTS) & (lane < N_EXPERTS + N_GROUPS)
    gl = jnp.where(is_g, logits, neg)
    gmax = jnp.max(gl, axis=-1, keepdims=True)
    g_p = 1.0 / jnp.sum(jnp.exp(gl - gmax), axis=-1, keepdims=True)
    g_idx = jnp.min(jnp.where(gl == gmax, lane_f - N_EXPERTS, big), axis=-1,
                    keepdims=True).astype(jnp.int32)
    el = jnp.where((lane < N_EXPERTS) & ((lane >> int(math.log2(EXPERTS_PER_GROUP))) == g_idx), logits, neg)
    t1 = jnp.max(el, axis=-1, keepdims=True)
    i1 = jnp.min(jnp.where(el == t1, lane_f, big), axis=-1, keepdims=True).astype(jnp.int32)
    el2 = jnp.where(lane == i1, neg, el)
    t2 = jnp.max(el2, axis=-1, keepdims=True)
    i2 = jnp.min(jnp.where(el2 == t2, lane_f, big), axis=-1, keepdims=True).astype(jnp.int32)
    e21 = jnp.exp(t2 - t1)
    w1 = g_p / (1.0 + e21)
    w2 = g_p * e21 / (1.0 + e21)
    rt_ref[...] = jnp.where(lane == 0, w1, jnp.where(lane == 1, w2, 0.0))

    sh = int(math.log2(MOE_TILE))
    oh1 = lane == i1
    oh2 = lane == i2
    ind = jnp.where(oh1 | oh2, 1.0, 0.0)
    lrank = jnp.dot(tri_ref[...], ind.astype(BF16), preferred_element_type=F32).astype(jnp.int32)
    n_new = lrank[tm - 1:tm, :] + ind[tm - 1:tm, :].astype(jnp.int32)
    cnt = cnt_ref[...]
    cur = cur_ref[...]
    nfree = nused_ref[...]
    tiles_before = (cnt + (MOE_TILE - 1)) >> sh
    newf = ((cnt + n_new + (MOE_TILE - 1)) >> sh) - tiles_before
    newf8 = jnp.broadcast_to(newf.astype(BF16), (8, newf.shape[1]))
    pre = jnp.dot(newf8, upper_ref[...], preferred_element_type=F32)[0:1, :].astype(jnp.int32)
    new_tile = nfree + pre
    grank = cnt + lrank
    ptile = jnp.where((grank >> sh) < tiles_before, cur, new_tile)
    posfull = ((ptile << sh) + (grank & (MOE_TILE - 1))).astype(F32)
    pos1 = jnp.sum(jnp.where(oh1, posfull, 0.0), axis=-1, keepdims=True)
    pos2 = jnp.sum(jnp.where(oh2, posfull, 0.0), axis=-1, keepdims=True)
    tr = lax.broadcasted_iota(jnp.int32, (tm, tm), 0)
    tc = lax.broadcasted_iota(jnp.int32, (tm, tm), 1)
    as_row = lambda v: jnp.sum(jnp.where(tr == tc, jnp.broadcast_to(v, (tm, tm)), 0.0),
                               axis=0, keepdims=True)
    sub = lax.broadcasted_iota(jnp.int32, pos_ref.shape, 0)
    pos_ref[...] = jnp.where(sub == 0, as_row(pos1),
                             jnp.where(sub == 1, as_row(pos2), 0.0)).astype(jnp.int32)

    el_r = lax.broadcasted_iota(jnp.int32, (ROUTER_LANES, ROUTER_LANES), 0)
    el_c = lax.broadcasted_iota(jnp.int32, (ROUTER_LANES, ROUTER_LANES), 1)
    as_col = lambda v: jnp.sum(jnp.where(el_r == el_c, jnp.broadcast_to(v, el_r.shape), 0.0),
                               axis=1, keepdims=True)
    tile_col = as_col(jnp.where(newf > 0, new_tile, -1).astype(F32)).astype(jnp.int32)
    tlane = lax.broadcasted_iota(jnp.int32, (ROUTER_LANES, te_ref.shape[1]), 1)
    erow = lax.broadcasted_iota(jnp.int32, (ROUTER_LANES, te_ref.shape[1]), 0).astype(F32)
    te_ref[...] += jnp.sum(jnp.where(tlane == tile_col, erow, 0.0), axis=0,
                           keepdims=True).astype(jnp.int32)
    to_cur = jnp.minimum(n_new, (tiles_before << sh) - cnt)
    cur_col = as_col(cur.astype(F32)).astype(jnp.int32)
    added = (jnp.where(tlane == cur_col, as_col(to_cur.astype(F32)), 0.0)
             + jnp.where(tlane == tile_col, as_col((n_new - to_cur).astype(F32)), 0.0))
    fill_ref[...] += jnp.sum(added, axis=0, keepdims=True).astype(jnp.int32)
    cnt_ref[...] = cnt + n_new
    cur_ref[...] = jnp.where(newf > 0, new_tile, cur)
    nused_ref[...] = nfree + jnp.sum(newf.astype(F32), axis=-1, keepdims=True).astype(jnp.int32)


def _merge_kernel(x_ref, ya_ref, yb_ref, gates_ref, gluw_ref, glub_ref, wba_ref, wbb_ref, wout_ref,
                  fng_ref, rw_ref, rb_ref, tri_ref, upper_ref, x1_ref, rt_ref, pos_ref,
                  te_ref, fill_ref, nused_ref, xs_ref, cnt_ref, cur_ref, tbuf_even, tbuf_odd, posv,
                  pos_smem, fill_smem, zblk, rsem, psem, zsem):
    i = pl.program_id(0)
    last = pl.num_programs(0) - 1
    tm = x_ref.shape[0]
    tbufs = (tbuf_even, tbuf_odd)

    @pl.when(i == 0)
    def _():
        cnt_ref[...] = jnp.zeros_like(cnt_ref)
        cur_ref[...] = jnp.zeros_like(cur_ref)
        te_ref[...] = jnp.zeros_like(te_ref)
        fill_ref[...] = jnp.zeros_like(fill_ref)
        nused_ref[...] = jnp.zeros_like(nused_ref)

    def row_copies(par, unroll):
        def body(r, carry):
            for k in range(2):
                pltpu.async_copy(tbufs[par].at[pl.ds(r, 1), :],
                                 xs_ref.at[pl.ds(pos_smem[par, k, r], 1), :], rsem.at[par],
                                 priority=ROW_DMA_PRIORITY)
            return carry
        lax.fori_loop(0, tm, body, 0, unroll=unroll)

    def drain_rows(par):
        for _ in range(2):
            pltpu.make_async_copy(tbufs[par], xs_ref.at[pl.ds(0, tm), :], rsem.at[par]).wait()

    def pos_to_smem(par):
        return pltpu.make_async_copy(posv, pos_smem.at[pl.ds(par, 1)], psem)

    @pl.when(i >= 1)
    def _():
        pos_to_smem(0).wait()

    def step(par, dispatch_prev):
        @pl.when(i >= 2)
        def _():
            drain_rows(par)
        if dispatch_prev:
            row_copies(1 - par, True)
        _merge_tile_math(x_ref, ya_ref, yb_ref, gates_ref, gluw_ref, glub_ref, wba_ref, wbb_ref,
                         wout_ref, fng_ref, rw_ref, rb_ref, tri_ref, upper_ref, x1_ref,
                         tbufs[par], rt_ref, pos_ref.at[0], te_ref, fill_ref, nused_ref, cnt_ref,
                         cur_ref)
        posv[...] = pos_ref[...]
        pos_to_smem(par).start()

    pl.when(i == 0)(lambda: step(0, False))
    for par in range(2):
        pl.when((i > 0) & (i % 2 == par))(functools.partial(step, par, True))

    @pl.when(i == last)
    def _():
        pos_to_smem(0).wait()
        for par in range(2):
            @pl.when(i % 2 == par)
            def _(par=par):
                row_copies(par, DMA_UNROLL)

                @pl.when(i >= 1)
                def _():
                    drain_rows(1 - par)
                drain_rows(par)

        fill_copy = pltpu.make_async_copy(fill_ref, fill_smem, psem)
        fill_copy.start()
        zblk[...] = jnp.zeros_like(zblk)
        fill_copy.wait()

        def over_tails(op):
            def tile(j, carry):
                fill = fill_smem[0, j]
                head = (-fill) & (SUBLANES - 1)
                for r in range(SUBLANES - 1):
                    @pl.when(r < head)
                    def _(r=r):
                        op(pltpu.make_async_copy(zblk.at[pl.ds(0, 1), :],
                                                 xs_ref.at[pl.ds(j * MOE_TILE + fill + r, 1), :], zsem))
                start = fill + head
                todo = MOE_TILE - start
                size = MOE_TILE
                while size >= SUBLANES:
                    @pl.when((todo & size) != 0)
                    def _(start=start, size=size):
                        off = pl.multiple_of(j * MOE_TILE + start, SUBLANES)
                        op(pltpu.make_async_copy(zblk.at[pl.ds(0, size), :],
                                                 xs_ref.at[pl.ds(off, size), :], zsem))
                    start = start + (todo & size)
                    size //= 2
                return carry
            lax.fori_loop(0, xs_ref.shape[0] // MOE_TILE, tile, 0)

        over_tails(lambda c: c.start())
        over_tails(lambda c: c.wait())


def _moe_tiles(t):
    return (2 * t) // MOE_TILE + N_EXPERTS


def _merge(x2, ya, yb, gates, gluw, glub, wba, wbb, wout, fng, rw, rb):
    t, d = x2.shape
    tm = MERGE_TILE
    assert t % tm == 0 and tm <= MOE_TILE
    n_tiles = _moe_tiles(t)
    te_lanes = -(-n_tiles // LANES) * LANES
    rr = jnp.arange(tm)
    tri = (rr[None, :] < rr[:, None]).astype(BF16)
    ll = jnp.arange(ROUTER_LANES)
    upper = (ll[:, None] < ll[None, :]).astype(BF16)
    full = lambda a: pl.BlockSpec(a.shape, lambda i: (0,) * a.ndim)
    rowblk = lambda n: pl.BlockSpec((tm, n), lambda i: (i, 0))
    fixed = lambda n: pl.BlockSpec((1, n), lambda i: (0, 0))
    return pl.pallas_call(
        _merge_kernel,
        grid=(t // tm,),
        in_specs=[rowblk(d), rowblk(ya.shape[1]), rowblk(yb.shape[1]), rowblk(gates.shape[1]),
                  full(gluw), full(glub), full(wba), full(wbb), full(wout), full(fng), full(rw),
                  full(rb), full(tri), full(upper)],
        out_specs=[rowblk(d), rowblk(ROUTER_LANES),
                   pl.BlockSpec((1, SUBLANES, tm), lambda i: (i, 0, 0)),
                   fixed(te_lanes), fixed(te_lanes), fixed(ROUTER_LANES),
                   pl.BlockSpec(memory_space=pl.ANY)],
        out_shape=[jax.ShapeDtypeStruct((t, d), F32),
                   jax.ShapeDtypeStruct((t, ROUTER_LANES), F32),
                   jax.ShapeDtypeStruct((t // tm, SUBLANES, tm), jnp.int32),
                   jax.ShapeDtypeStruct((1, te_lanes), jnp.int32),
                   jax.ShapeDtypeStruct((1, te_lanes), jnp.int32),
                   jax.ShapeDtypeStruct((1, ROUTER_LANES), jnp.int32),
                   jax.ShapeDtypeStruct((n_tiles * MOE_TILE, d), F32)],
        scratch_shapes=[pltpu.VMEM((1, ROUTER_LANES), jnp.int32),
                        pltpu.VMEM((1, ROUTER_LANES), jnp.int32),
                        pltpu.VMEM((tm, d), F32),
                        pltpu.VMEM((tm, d), F32),
                        pltpu.VMEM((1, SUBLANES, tm), jnp.int32),
                        pltpu.SMEM((2, SUBLANES, tm), jnp.int32),
                        pltpu.SMEM((1, te_lanes), jnp.int32),
                        pltpu.VMEM((MOE_TILE, d), F32),
                        pltpu.SemaphoreType.DMA((2,)),
                        pltpu.SemaphoreType.DMA(()),
                        pltpu.SemaphoreType.DMA(())],
        compiler_params=pltpu.CompilerParams(dimension_semantics=("arbitrary",),
                                             vmem_limit_bytes=VMEM_LIMIT),
        name="merge",
    )(x2, ya, yb, gates, gluw, glub, wba, wbb, wout, fng, rw, rb, tri, upper)


def _experts_kernel(te_ref, nused_ref, xs_ref, wg_ref, wu_ref, wd_ref, y_ref):
    j = pl.program_id(0)

    @pl.when(j < nused_ref[0])
    def _():
        x = xs_ref[...].astype(BF16)
        hg = _mm(x, wg_ref[...])
        hid = hg * _sigmoid(hg) * _mm(x, wu_ref[...])
        y_ref[...] = _mm(hid, wd_ref[...])

    @pl.when(j >= nused_ref[0])
    def _():
        y_ref[...] = jnp.zeros_like(y_ref)


def _experts(tile_expert, n_used, xs, wg, wu, wd):
    n_rows, hw = xs.shape
    ne, d, de = wg.shape
    return pl.pallas_call(
        _experts_kernel,
        grid_spec=pltpu.PrefetchScalarGridSpec(
            num_scalar_prefetch=2,
            grid=(n_rows // MOE_TILE,),
            in_specs=[pl.BlockSpec((MOE_TILE, hw), lambda j, te, nu: (j, 0)),
                      pl.BlockSpec((None, d, de), lambda j, te, nu: (te[j], 0, 0)),
                      pl.BlockSpec((None, d, de), lambda j, te, nu: (te[j], 0, 0)),
                      pl.BlockSpec((None, de, d), lambda j, te, nu: (te[j], 0, 0))],
            out_specs=pl.BlockSpec((MOE_TILE, hw), lambda j, te, nu: (j, 0))),
        out_shape=jax.ShapeDtypeStruct((n_rows, hw), F32),
        compiler_params=pltpu.CompilerParams(dimension_semantics=("arbitrary",),
                                             vmem_limit_bytes=VMEM_LIMIT),
        name="experts",
    )(tile_expert, n_used, xs, wg, wu, wd)


def _ple_kernel(pos1_ref, pos2_ref, x_ref, rt_ref, p_ref, png_ref, wg_ref, wp_ref, fng_ref, y_ref,
                o_ref, ybuf_even, ybuf_odd, sem):
    i = pl.program_id(0)
    tm = x_ref.shape[0]
    ybufs = (ybuf_even, ybuf_odd)

    def gather(tile, par, unroll):
        def body(r, carry):
            for k, pos_ref in enumerate((pos1_ref, pos2_ref)):
                pltpu.async_copy(y_ref.at[pl.ds(pos_ref[tile * tm + r], 1), :],
                                 ybufs[par].at[k, pl.ds(r, 1), :], sem.at[par],
                                 priority=ROW_DMA_PRIORITY)
            return carry
        lax.fori_loop(0, tm, body, 0, unroll=unroll)

    @pl.when(i == 0)
    def _():
        gather(i, 0, DMA_UNROLL)

    def step(par, prefetch_next):
        ybuf = ybufs[par]
        for k in range(2):
            pltpu.make_async_copy(y_ref.at[pl.ds(0, tm), :], ybuf.at[k], sem.at[par]).wait()
        if prefetch_next:
            gather(i + 1, 1 - par, True)
        rt = rt_ref[...]
        x2 = x_ref[...] + rt[:, 0:1] * ybuf[0] + rt[:, 1:2] * ybuf[1]
        hp = _rms_norm(x2, png_ref[...])
        gate = _sigmoid(_mm(hp, wg_ref[...]))
        x3 = x2 + gate * _mm(p_ref[...], wp_ref[...])
        o_ref[...] = _rms_norm(x3, fng_ref[...])

    last = pl.num_programs(0) - 1
    for par in range(2):
        pl.when((i % 2 == par) & (i < last))(functools.partial(step, par, True))
        pl.when((i % 2 == par) & (i == last))(functools.partial(step, par, False))


def _ple(pos1, pos2, x1, rt, p2, png, wg, wp, fng, y_pack, tm):
    t, d = x1.shape
    hw = y_pack.shape[1]
    full = lambda a: pl.BlockSpec(a.shape, lambda i, p1, p2_: (0,) * a.ndim)
    rowblk = lambda n: pl.BlockSpec((tm, n), lambda i, p1, p2_: (i, 0))
    return pl.pallas_call(
        _ple_kernel,
        grid_spec=pltpu.PrefetchScalarGridSpec(
            num_scalar_prefetch=2,
            grid=(t // tm,),
            in_specs=[rowblk(d), rowblk(rt.shape[1]), rowblk(p2.shape[1]),
                      full(png), full(wg), full(wp), full(fng),
                      pl.BlockSpec(memory_space=pl.ANY)],
            out_specs=rowblk(d),
            scratch_shapes=[pltpu.VMEM((2, tm, hw), F32),
                            pltpu.VMEM((2, tm, hw), F32),
                            pltpu.SemaphoreType.DMA((2,))]),
        out_shape=jax.ShapeDtypeStruct((t, d), F32),
        compiler_params=pltpu.CompilerParams(dimension_semantics=("arbitrary",),
                                             vmem_limit_bytes=VMEM_LIMIT),
        name="ple",
    )(pos1, pos2, x1, rt, p2, png, wg, wp, fng, y_pack)


def _row_tile(t, want):
    tm = min(want, t)
    while t % tm:
        tm //= 2
    return tm


def _layer(x, p, mix_norm, w_in, mu_shift, rk_w0, rk_w_up, rk_a0, rk_a_up, rk_g_up,
           rk_k_k, rk_k_a, rk_r_k, rk_ln_g, rk_ln_b, s5_lam_re, s5_lam_im, s5_log_dt,
           s5_b_re, s5_b_im, s5_c_re, s5_c_im, s5_d, s5_glu_w, s5_glu_b,
           w_branch_a, w_branch_b, w_out, ffn_norm, router_group_w, router_group_b,
           router_expert_w, router_expert_b, exp_w_gate, exp_w_up, exp_w_down,
           ple_norm, ple_gate_w, ple_proj):
    b, s, d = x.shape
    t = b * s
    W = RWKV_WIDTH
    row = lambda a: a.reshape(1, -1).astype(F32)
    x2 = x.reshape(t, d)

    crw, us5, gates = _in_proj(x2, row(mix_norm), w_in.astype(BF16), _row_tile(t, 512))

    wl = jnp.zeros((LORA_COLS, 3 * W), F32)
    wl = wl.at[:DECAY_LORA, :W].set(rk_w_up)
    wl = wl.at[DECAY_LORA:DECAY_LORA + AAA_LORA, W:2 * W].set(rk_a_up)
    wl = wl.at[DECAY_LORA + AAA_LORA:, 2 * W:].set(rk_g_up)
    hid = jnp.arange(LANES) // RWKV_HEAD_DIM
    ones_bd = (hid[:, None] == hid[None, :]).astype(BF16)
    ya = _rwkv(crw.reshape(b, s, RWKV_COLS), row(mu_shift), wl.astype(BF16), row(rk_w0), row(rk_a0),
               row(rk_k_k), row(rk_k_a), row(rk_r_k), row(rk_ln_g), row(rk_ln_b), ones_bd,
               _rwkv_cum_matrix(_row_tile(s, RWKV_BLOCK)))

    s5_wa, s5_wo, plr, pli = _s5_mats(s5_lam_re, s5_lam_im, s5_log_dt, s5_b_re, s5_b_im,
                                      s5_c_re, s5_c_im, s5_d)
    yb = _s5(us5.reshape(b, s // S5_CHUNK, S5_CHUNK, S5_WIDTH), s5_wa, s5_wo, plr, pli)

    rw = jnp.zeros((d, ROUTER_LANES), F32)
    rw = rw.at[:, :N_EXPERTS].set(router_expert_w).at[:, N_EXPERTS:N_EXPERTS + N_GROUPS].set(router_group_w)
    rb = jnp.zeros((1, ROUTER_LANES), F32)
    rb = rb.at[0, :N_EXPERTS].set(router_expert_b).at[0, N_EXPERTS:N_EXPERTS + N_GROUPS].set(router_group_b)
    rw_hi = rw.astype(BF16)
    rw = jnp.concatenate([rw_hi, (rw - rw_hi.astype(F32)).astype(BF16)], axis=1)
    x1, rt, pos, tile_expert, _, n_used, xs = _merge(
        x2, ya.reshape(t, W), yb.reshape(t, S5_WIDTH), gates, s5_glu_w.astype(BF16), row(s5_glu_b),
        w_branch_a.astype(BF16), w_branch_b.astype(BF16), w_out.astype(BF16), row(ffn_norm), rw, rb)
    n_tiles = _moe_tiles(t)
    pos1, pos2 = pos[:, 0, :].reshape(t), pos[:, 1, :].reshape(t)
    y_pack = _experts(tile_expert[0, :n_tiles], n_used[0, :1], xs, exp_w_gate, exp_w_up, exp_w_down)
    return (pos1, pos2, x1, rt, p.reshape(t, -1), row(ple_norm), ple_gate_w.astype(BF16),
            ple_proj.astype(BF16), y_pack)


def kernel(x, p, mix_norm, w_in, mu_shift, rk_w0, rk_w_up, rk_a0, rk_a_up, rk_g_up, rk_k_k, rk_k_a,
           rk_r_k, rk_ln_g, rk_ln_b, s5_lam_re, s5_lam_im, s5_log_dt, s5_b_re, s5_b_im, s5_c_re,
           s5_c_im, s5_d, s5_glu_w, s5_glu_b, w_branch_a, w_branch_b, w_out, ffn_norm,
           router_group_w, router_group_b, router_expert_w, router_expert_b, exp_w_gate, exp_w_up,
           exp_w_down, ple_norm, ple_gate_w, ple_proj, final_norm):
    b, s, d = x.shape
    depth = w_in.shape[0]
    assert depth == 1, "the final norm is fused into the last layer's PLE kernel"
    i = 0
    pos1, pos2, x1, rt, p2, png, wpg, wpp, y_pack = _layer(
        x, p[i], mix_norm[i], w_in[i], mu_shift[i], rk_w0[i], rk_w_up[i], rk_a0[i], rk_a_up[i],
        rk_g_up[i], rk_k_k[i], rk_k_a[i], rk_r_k[i], rk_ln_g[i], rk_ln_b[i], s5_lam_re[i],
        s5_lam_im[i], s5_log_dt[i], s5_b_re[i], s5_b_im[i], s5_c_re[i], s5_c_im[i], s5_d[i],
        s5_glu_w[i], s5_glu_b[i], w_branch_a[i], w_branch_b[i], w_out[i], ffn_norm[i],
        router_group_w[i], router_group_b[i], router_expert_w[i], router_expert_b[i],
        exp_w_gate[i], exp_w_up[i], exp_w_down[i], ple_norm[i], ple_gate_w[i], ple_proj[i])
    out = _ple(pos1, pos2, x1, rt, p2, png, wpg, wpp, final_norm.reshape(1, -1).astype(F32), y_pack,
               _row_tile(b * s, MERGE_TILE))
    return out.reshape(b, s, d)
```

```python
import functools
import math

import jax
import jax.numpy as jnp
from jax import lax
from jax.experimental import pallas as pl
from jax.experimental.pallas import tpu as pltpu

F32 = jnp.float32
BF16 = jnp.bfloat16

NORM_EPS = 1e-6
GN_EPS = 64e-5

RWKV_HEADS = 8
RWKV_HEAD_DIM = 64
RWKV_WIDTH = RWKV_HEADS * RWKV_HEAD_DIM
DECAY_LORA = 64
AAA_LORA = 64
GATE_LORA = 128
LORA_COLS = DECAY_LORA + AAA_LORA + GATE_LORA
RWKV_COLS = 3 * RWKV_WIDTH + LORA_COLS
S5_GROUPS = 16
S5_GROUP_CH = 16
S5_WIDTH = S5_GROUPS * S5_GROUP_CH
S5_STATE = 64
S5_ZW = 2 * S5_GROUPS * S5_STATE
N_GROUPS = 4
EXPERTS_PER_GROUP = 8
N_EXPERTS = N_GROUPS * EXPERTS_PER_GROUP

LANES = 128
SUBLANES = 8
RWKV_CHUNK = 64
RWKV_INV_BLOCK = 16
RWKV_BLOCK = 256
S5_CHUNK = 8
ROUTER_LANES = 128
MOE_TILE = 512
MERGE_TILE = 512
DMA_UNROLL = 8
VMEM_LIMIT = 56 * 1024 * 1024


def _mm(a, b):
    return jnp.dot(a.astype(BF16), b.astype(BF16), preferred_element_type=F32)


def _mm_nt(a, b):
    return lax.dot_general(a.astype(BF16), b.astype(BF16), (((1,), (1,)), ((), ())),
                           preferred_element_type=F32)


def _mm_tn(a, b):
    return lax.dot_general(a.astype(BF16), b.astype(BF16), (((0,), (0,)), ((), ())),
                           preferred_element_type=F32)


def _split2(x):
    hi = x.astype(BF16)
    return hi, (x - hi.astype(F32)).astype(BF16)


def _mm_split_lhs(a_bf16, x):
    hi, lo = _split2(x)
    d = lambda t: jnp.dot(a_bf16, t, preferred_element_type=F32)
    return d(hi) + d(lo)


def _sigmoid(x):
    return 0.5 * jnp.tanh(0.5 * x) + 0.5


def _rms_norm(x, g):
    ms = jnp.mean(x * x, axis=-1, keepdims=True)
    return x * lax.rsqrt(ms + NORM_EPS) * g


def _in_proj_kernel(x_ref, g_ref, w_ref, crw_ref, us5_ref, gates_ref):
    h = _rms_norm(x_ref[...], g_ref[...]).astype(BF16)
    c0, c1 = RWKV_COLS, RWKV_COLS + S5_WIDTH
    crw_ref[...] = jnp.dot(h, w_ref[:, :c0], preferred_element_type=F32)
    us5_ref[...] = jnp.dot(h, w_ref[:, c0:c1], preferred_element_type=F32)
    gates_ref[...] = jnp.dot(h, w_ref[:, c1:], preferred_element_type=F32).astype(BF16)


def _in_proj(x2, g, w, tm):
    t, d = x2.shape
    n = w.shape[1]
    ng = n - RWKV_COLS - S5_WIDTH
    return pl.pallas_call(
        _in_proj_kernel,
        grid=(t // tm,),
        in_specs=[pl.BlockSpec((tm, d), lambda i: (i, 0)),
                  pl.BlockSpec((1, d), lambda i: (0, 0)),
                  pl.BlockSpec((d, n), lambda i: (0, 0))],
        out_specs=[pl.BlockSpec((tm, RWKV_COLS), lambda i: (i, 0)),
                   pl.BlockSpec((tm, S5_WIDTH), lambda i: (i, 0)),
                   pl.BlockSpec((tm, ng), lambda i: (i, 0))],
        out_shape=[jax.ShapeDtypeStruct((t, RWKV_COLS), F32),
                   jax.ShapeDtypeStruct((t, S5_WIDTH), F32),
                   jax.ShapeDtypeStruct((t, ng), BF16)],
        compiler_params=pltpu.CompilerParams(dimension_semantics=("arbitrary",),
                                             vmem_limit_bytes=VMEM_LIMIT),
        name="in_proj",
    )(x2, g, w)


def _rwkv_kernel(c_ref, mu_ref, wl_ref, w0_ref, a0_ref, kk_ref, ka_ref, rk_ref, lng_ref, lnb_ref,
                 ones_ref, cum_ref, o_ref, carry_ref, s_ref):
    C = RWKV_CHUNK
    W = RWKV_WIDTH
    npairs = W // LANES

    @pl.when(pl.program_id(1) == 0)
    def _():
        carry_ref[...] = jnp.zeros_like(carry_ref)
        s_ref[...] = jnp.zeros_like(s_ref)

    c = c_ref[...]
    R = c.shape[0]
    nchunks = R // C
    row = lax.broadcasted_iota(jnp.int32, (R, 1), 0)
    prev = jnp.where(row == 0, carry_ref[...], pltpu.roll(c, 1, 0))
    carry_ref[...] = c[R - 1:R, :]
    cs = c + (prev - c) * mu_ref[...]

    r = cs[:, 0:W]
    k = cs[:, W:2 * W]
    v = cs[:, 2 * W:3 * W]
    lin = cs[:, 3 * W:]
    llane = lax.broadcasted_iota(jnp.int32, lin.shape, 1)
    lact = jnp.where(llane < DECAY_LORA, jnp.tanh(lin),
                     jnp.where(llane < DECAY_LORA + AAA_LORA, lin, _sigmoid(lin)))
    lo = _mm(lact, wl_ref[...])
    zw = -(w0_ref[...] + lo[:, 0:W])
    softplus = jnp.maximum(zw, 0.0) + jnp.log(1.0 + jnp.exp(-jnp.abs(zw)))
    ld = -jnp.exp(-softplus - 0.5)
    a = _sigmoid(a0_ref[...] + lo[:, W:2 * W])
    g = lo[:, 2 * W:3 * W]

    ones_bd = ones_ref[...]
    segsum = lambda t: jnp.concatenate(
        [_mm(t[:, p * LANES:(p + 1) * LANES], ones_bd) for p in range(npairs)], axis=1)
    kk = k * kk_ref[...]
    kkn = kk / jnp.maximum(jnp.sqrt(segsum(kk * kk)), 1e-12)
    kmod = k * (1.0 + (a - 1.0) * ka_ref[...])

    cums = _mm_split_lhs(cum_ref[...], ld)
    cum = cums[:R]
    tot = cums[R:]
    inv = jnp.exp(-cum)
    tail = jnp.exp(tot - cum)
    At = -kkn * jnp.exp(cum - ld)
    Rt = r * jnp.exp(cum)
    kka = kkn * a
    Bt = kka * inv
    Kt = kmod * inv
    Bend = kka * tail
    Kend = kmod * tail
    pc = jnp.exp(tot)

    lane = lax.broadcasted_iota(jnp.int32, (C, LANES), 1)
    h0 = lane < RWKV_HEAD_DIM
    split = lambda t: jnp.concatenate([jnp.where(h0, t, 0.0), jnp.where(h0, 0.0, t)], axis=0)
    grow = lax.broadcasted_iota(jnp.int32, (C, 4 * C), 0)
    gcol = lax.broadcasted_iota(jnp.int32, (C, 4 * C), 1) & (C - 1)
    r2 = lax.broadcasted_iota(jnp.int32, (2 * C, 2 * C), 0)
    c2 = lax.broadcasted_iota(jnp.int32, (2 * C, 2 * C), 1)
    eye = (r2 == c2).astype(F32)
    blk_shift = int(math.log2(RWKV_INV_BLOCK))
    same_blk = (r2 >> blk_shift) == (c2 >> blk_shift)
    same_head = (r2 < C) == (c2 < C)
    same_head2 = jnp.concatenate([same_head, same_head], axis=0)
    zeros_c = jnp.zeros((C, LANES), F32)
    zeros_2c = jnp.zeros((2 * C, LANES), F32)

    units = [(ci, p) for ci in range(nchunks) for p in range(npairs)]
    blk = lambda t, u: t[u[0] * C:(u[0] + 1) * C, u[1] * LANES:(u[1] + 1) * LANES]
    each = lambda f, *ls: [f(*xs) for xs in zip(*ls)]

    lhs = [jnp.concatenate([blk(At, u), blk(Rt, u)], axis=0) for u in units]
    rhs = [jnp.concatenate([split(blk(Bt, u)), split(blk(Kt, u))], axis=0) for u in units]
    G = each(_mm_nt, lhs, rhs)
    a_row = [jnp.where(gcol < grow, t[:C], 0.0) for t in G]
    m_row = [jnp.where(gcol <= grow, t[C:], 0.0) for t in G]
    a_bd = [split(t[:, :2 * C]) for t in a_row]

    a_d = [jnp.where(same_blk, t, 0.0) for t in a_bd]
    a_off = each(lambda x, y: x - y, a_bd, a_d)
    dinv = [eye + t for t in a_d]
    pw = a_d
    for _ in range(blk_shift - 1):
        pw = each(_mm, pw, pw)
        dinv = each(lambda x, y: x + _mm(x, y), dinv, pw)
    n1 = each(_mm, dinv, a_off)
    n2 = each(_mm, n1, n1)
    n3 = each(_mm, n1, n2)
    tinv = each(lambda x1, x2, x3, dv: _mm(eye + x1 + x2 + x3, dv), n1, n2, n3, dinv)

    vp = [blk(v, u) for u in units]
    v_st = [split(t) for t in vp]
    rhs0 = each(lambda ar, vs: _mm(ar[:, 2 * C:], vs), a_row, v_st)
    wu = each(lambda t, l, r0: _mm(t, jnp.concatenate([split(l[:C]), split(r0)], axis=1)),
              tinv, lhs, rhs0)
    wu_lp = [t[:C] + t[C:] for t in wu]
    mn = each(lambda w_, v_, u: _mm_tn(
        jnp.concatenate([w_, jnp.concatenate([zeros_c, v_], axis=1)], axis=0),
        jnp.concatenate([blk(Bend, u), blk(Kend, u)], axis=0)), wu_lp, vp, units)
    mn = [jnp.where(same_head2, t, 0.0) for t in mn]
    qy = each(lambda m_, w_, vs: _mm(m_, jnp.concatenate(
        [w_, jnp.concatenate([zeros_2c, vs], axis=1)], axis=0)), m_row, wu, v_st)

    ys = [[None] * npairs for _ in range(nchunks)]
    states = [s_ref[p] for p in range(npairs)]
    for i, (ci, p) in enumerate(units):
        s_old = states[p]
        q = lhs[i][C:] + qy[i][:, :LANES]
        ys[ci][p] = _mm_nt(q, s_old) + qy[i][:, LANES:]
        states[p] = s_old * blk(pc, (ci, p))[0:1, :] + _mm(s_old, mn[i][:LANES]) + mn[i][LANES:]
    for p in range(npairs):
        s_ref[p] = states[p]

    y = jnp.concatenate([jnp.concatenate(t, axis=1) for t in ys], axis=0)
    inv_n = 1.0 / RWKV_HEAD_DIM
    mean = segsum(y) * inv_n
    d = y - mean
    var = segsum(d * d) * inv_n
    yn = d * lax.rsqrt(var + GN_EPS) * lng_ref[...] + lnb_ref[...]
    bonus = segsum(r * kmod * rk_ref[...]) * v
    o_ref[...] = ((yn + bonus) * g).astype(o_ref.dtype)


def _rwkv_cum_matrix(rows):
    t = jnp.arange(rows)
    same = (t[:, None] // RWKV_CHUNK) == (t[None, :] // RWKV_CHUNK)
    return jnp.concatenate([same & (t[None, :] <= t[:, None]), same], axis=0).astype(BF16)


def _rwkv(crw3, mu, wl, w0, a0, k_k, k_a, r_k, ln_g, ln_b, ones_bd, cum_mat):
    b, s, _ = crw3.shape
    R = cum_mat.shape[1]
    W = RWKV_WIDTH
    vec = lambda n: pl.BlockSpec((1, n), lambda i, j: (0, 0))
    return pl.pallas_call(
        _rwkv_kernel,
        grid=(b, s // R),
        in_specs=[pl.BlockSpec((None, R, RWKV_COLS), lambda i, j: (i, j, 0)),
                  vec(RWKV_COLS),
                  pl.BlockSpec((LORA_COLS, 3 * W), lambda i, j: (0, 0)),
                  vec(W), vec(W), vec(W), vec(W), vec(W), vec(W), vec(W),
                  pl.BlockSpec((LANES, LANES), lambda i, j: (0, 0)),
                  pl.BlockSpec((2 * R, R), lambda i, j: (0, 0))],
        out_specs=pl.BlockSpec((None, R, W), lambda i, j: (i, j, 0)),
        out_shape=jax.ShapeDtypeStruct((b, s, W), BF16),
        scratch_shapes=[pltpu.VMEM((1, RWKV_COLS), F32),
                        pltpu.VMEM((W // LANES, LANES, LANES), F32)],
        compiler_params=pltpu.CompilerParams(dimension_semantics=("arbitrary", "arbitrary"),
                                             vmem_limit_bytes=VMEM_LIMIT),
        name="rwkv",
    )(crw3, mu, wl, w0, a0, k_k, k_a, r_k, ln_g, ln_b, ones_bd, cum_mat)


def _s5_expand_kernel(are_ref, aim_ref, kf_ref, of_ref, wa_ref, wo_ref, *, groups, state, chans):
    rows = are_ref.shape[0]
    base = pl.program_id(0) * rows
    r = base + lax.broadcasted_iota(jnp.int32, (rows, LANES), 0)
    lane = lax.broadcasted_iota(jnp.int32, (rows, LANES), 1)
    in_group = (r // chans) % groups
    st_group = (r // state) % groups

    def emit(dst_ref, col0, src, row_group, per_group):
        per_tile = LANES // per_group
        tiles = groups // per_tile
        for m in range(src.shape[1] // LANES):
            vals = src[:, m * LANES:(m + 1) * LANES]
            for tl in range(tiles):
                c = col0 + (m * tiles + tl) * LANES
                keep = row_group == tl * per_tile + lane // per_group
                dst_ref[:, c:c + LANES] = jnp.where(keep, vals, 0.0).astype(dst_ref.dtype)

    half = groups * state
    emit(wa_ref, 0, are_ref[...], in_group, state)
    emit(wa_ref, half, aim_ref[...], in_group, state)
    emit(wa_ref, 2 * half, kf_ref[...], in_group, chans)
    emit(wo_ref, 0, of_ref[...], st_group, chans)


def _s5_expand(a_re, a_im, kf, of, groups, state, chans):
    rows = a_re.shape[0]
    assert kf.shape[0] == rows and of.shape[0] == rows
    tr = _row_tile(rows, 256)
    n_a = 2 * groups * state + kf.shape[1] // LANES * groups * chans
    n_o = of.shape[1] // LANES * groups * chans
    blk = lambda a: pl.BlockSpec((tr, a.shape[1]), lambda i: (i, 0))
    return pl.pallas_call(
        functools.partial(_s5_expand_kernel, groups=groups, state=state, chans=chans),
        grid=(rows // tr,),
        in_specs=[blk(a_re), blk(a_im), blk(kf), blk(of)],
        out_specs=[pl.BlockSpec((tr, n_a), lambda i: (i, 0)), pl.BlockSpec((tr, n_o), lambda i: (i, 0))],
        out_shape=[jax.ShapeDtypeStruct((rows, n_a), BF16), jax.ShapeDtypeStruct((rows, n_o), BF16)],
        compiler_params=pltpu.CompilerParams(dimension_semantics=("arbitrary",),
                                             vmem_limit_bytes=VMEM_LIMIT),
        name="s5_expand",
    )(a_re, a_im, kf, of)


def _s5_mats(lam_re, lam_im, log_dt, b_re, b_im, c_re, c_im, d_skip):
    L = S5_CHUNK
    G, N = lam_re.shape
    ch = b_re.shape[-1]
    dt = jnp.exp(log_dt)[:, None]
    lr, li = lam_re, lam_im
    mag = jnp.exp(lr * dt)
    lb_re, lb_im = mag * jnp.cos(li * dt), mag * jnp.sin(li * dt)
    den = lr * lr + li * li
    nr, ni = lb_re - 1.0, lb_im
    coef_re = (nr * lr + ni * li) / den
    coef_im = (ni * lr - nr * li) / den
    bb_re = coef_re[..., None] * b_re - coef_im[..., None] * b_im
    bb_im = coef_re[..., None] * b_im + coef_im[..., None] * b_re
    prs, pis = [jnp.ones_like(lb_re)], [jnp.zeros_like(lb_im)]
    for _ in range(L):
        pr_, pi_ = prs[-1], pis[-1]
        prs.append(pr_ * lb_re - pi_ * lb_im)
        pis.append(pr_ * lb_im + pi_ * lb_re)
    pr = jnp.stack(prs)
    pi = jnp.stack(pis)
    hp = lax.Precision.HIGHEST

    def lam_bb(qr, qi):
        return (qr[..., None] * bb_re[None] - qi[..., None] * bb_im[None],
                qr[..., None] * bb_im[None] + qi[..., None] * bb_re[None])

    def repeat_cols(a, inner, reps):
        k = a.shape[-1]
        src = jnp.arange(k)[:, None]
        dst = jnp.arange(k * reps)[None, :]
        rep = ((src // inner == dst // (inner * reps)) & (src % inner == dst % inner)).astype(F32)
        return jnp.dot(a, rep, precision=hp)

    in_rows = L * G * ch
    st_rows = 2 * G * N
    wre, wim = lam_bb(pr[:L][::-1], pi[:L][::-1])
    a_in = [repeat_cols(jnp.swapaxes(part, 2, 3).reshape(in_rows, N), N, LANES // N)
            for part in (wre, wim)]
    lre, lim = lam_bb(pr[:L], pi[:L])
    kern = (jnp.einsum('gon,lgni->lgio', c_re, lre, precision=hp)
            - jnp.einsum('gon,lgni->lgio', c_im, lim, precision=hp))
    kern = kern.at[0].add(d_skip[:, :, None] * jnp.eye(ch, dtype=F32)[None])
    lag = jnp.arange(L)[None, :] - jnp.arange(L)[:, None]
    kst = jnp.where((lag >= 0)[:, :, None, None, None], kern[jnp.maximum(lag, 0)], 0.0)
    kf = jnp.transpose(kst, (0, 2, 3, 1, 4)).reshape(in_rows, L * ch)
    kf = repeat_cols(kf, ch, LANES // ch)
    qr, qi = pr[1:L + 1][:, :, None, :], pi[1:L + 1][:, :, None, :]
    o_re = c_re[None] * qr - c_im[None] * qi
    o_im = -c_re[None] * qi - c_im[None] * qr
    of = jnp.transpose(jnp.stack([o_re, o_im]), (0, 2, 4, 1, 3))
    of = repeat_cols(of.reshape(st_rows, L * ch), ch, LANES // ch)
    w_a, w_out_flat = _s5_expand(a_in[0], a_in[1], kf, of, G, N, ch)
    plr = pr[L].reshape(1, G * N)
    pli = pi[L].reshape(1, G * N)
    return w_a, w_out_flat, plr, pli


def _s5_kernel(u_ref, wa_ref, wo_ref, plr_ref, pli_ref, o_ref, wloc_ref, zprev_ref):
    nch, L, w = u_ref.shape
    half = S5_ZW // 2

    r = _mm(u_ref[:, 0, :], wa_ref[0:w, :])
    for j in range(1, L):
        r = r + _mm(u_ref[:, j, :], wa_ref[j * w:(j + 1) * w, :])
    wloc_ref[...] = r[:, :S5_ZW]
    y_lag = r[:, S5_ZW:]

    plr = plr_ref[...]
    pli = pli_ref[...]

    def step(ci, z):
        zprev_ref[pl.ds(ci, 1), :] = z
        zr, zi = z[:, :half], z[:, half:]
        nz = jnp.concatenate([plr * zr - pli * zi, plr * zi + pli * zr], axis=1)
        return nz + wloc_ref[pl.ds(ci, 1), :]

    lax.fori_loop(0, nch, step, jnp.zeros((1, S5_ZW), F32))
    y = y_lag + _mm(zprev_ref[...], wo_ref[...])
    for j in range(L):
        o_ref[:, j, :] = y[:, j * w:(j + 1) * w]


def _s5(u4, w_a, w_o, plr, pli):
    b, nch, L, w = u4.shape
    const = lambda a: pl.BlockSpec(a.shape, lambda i: (0,) * a.ndim, pipeline_mode=pl.Buffered(1))
    blk = pl.BlockSpec((None, nch, L, w), lambda i: (i, 0, 0, 0))
    return pl.pallas_call(
        _s5_kernel,
        grid=(b,),
        in_specs=[blk, const(w_a), const(w_o), const(plr), const(pli)],
        out_specs=blk,
        out_shape=jax.ShapeDtypeStruct(u4.shape, F32),
        scratch_shapes=[pltpu.VMEM((nch, S5_ZW), F32),
                        pltpu.VMEM((nch, S5_ZW), F32)],
        compiler_params=pltpu.CompilerParams(dimension_semantics=("arbitrary",),
                                             vmem_limit_bytes=VMEM_LIMIT),
        name="s5",
    )(u4, w_a, w_o, plr, pli)


def _merge_tile_math(x_ref, ya_ref, yb_ref, gates_ref, gluw_ref, glub_ref, wba_ref, wbb_ref, wout_ref,
                     fng_ref, rw_ref, rb_ref, tri_ref, upper_ref, x1_ref, t_ref, rt_ref, pos_ref,
                     te_ref, fill_ref, nused_ref, cnt_ref, cur_ref):
    d = x_ref.shape[1]
    tm = x_ref.shape[0]
    y_a = jnp.dot(ya_ref[...], wba_ref[...], preferred_element_type=F32)
    ys = yb_ref[...]
    z = 0.5 * ys * (1.0 + jnp.tanh(math.sqrt(2.0 / math.pi) * (ys + 0.044715 * (ys * ys * ys))))
    z = z * _sigmoid(_mm(z, gluw_ref[...]) + glub_ref[...])
    y_b = _mm(z, wbb_ref[...])
    gates = gates_ref[...].astype(F32)
    merged = _sigmoid(gates[:, :d]) * y_a + _sigmoid(gates[:, d:]) * y_b
    x1 = x_ref[...] + _mm(merged, wout_ref[...])
    x1_ref[...] = x1
    t = _rms_norm(x1, fng_ref[...])
    t_hi = t.astype(BF16)
    t_ref[...] = t

    t_lo = (t - t_hi.astype(F32)).astype(BF16)
    hh_hl = jnp.dot(t_hi, rw_ref[...], preferred_element_type=F32)
    lh = jnp.dot(t_lo, rw_ref[:, :ROUTER_LANES], preferred_element_type=F32)
    logits = hh_hl[:, :ROUTER_LANES] + hh_hl[:, ROUTER_LANES:] + lh + rb_ref[...]
    lane = lax.broadcasted_iota(jnp.int32, logits.shape, 1)
    neg = -jnp.inf
    lane_f = lane.astype(F32)
    big = float(1 << 20)
    is_g = (lane >= N_EXPERTS) & (lane < N_EXPERTS + N_GROUPS)
    gl = jnp.where(is_g, logits, neg)
    gmax = jnp.max(gl, axis=-1, keepdims=True)
    g_p = 1.0 / jnp.sum(jnp.exp(gl - gmax), axis=-1, keepdims=True)
    g_idx = jnp.min(jnp.where(gl == gmax, lane_f - N_EXPERTS, big), axis=-1,
                    keepdims=True).astype(jnp.int32)
    el = jnp.where((lane < N_EXPERTS) & ((lane >> int(math.log2(EXPERTS_PER_GROUP))) == g_idx), logits, neg)
    t1 = jnp.max(el, axis=-1, keepdims=True)
    i1 = jnp.min(jnp.where(el == t1, lane_f, big), axis=-1, keepdims=True).astype(jnp.int32)
    el2 = jnp.where(lane == i1, neg, el)
    t2 = jnp.max(el2, axis=-1, keepdims=True)
    i2 = jnp.min(jnp.where(el2 == t2, lane_f, big), axis=-1, keepdims=True).astype(jnp.int32)
    e21 = jnp.exp(t2 - t1)
    w1 = g_p / (1.0 + e21)
    w2 = g_p * e21 / (1.0 + e21)
    rt_ref[...] = jnp.where(lane == 0, w1, jnp.where(lane == 1, w2, 0.0))

    sh = int(math.log2(MOE_TILE))
    oh1 = lane == i1
    oh2 = lane == i2
    ind = jnp.where(oh1 | oh2, 1.0, 0.0)
    lrank = jnp.dot(tri_ref[...], ind.astype(BF16), preferred_element_type=F32).astype(jnp.int32)
    n_new = lrank[tm - 1:tm, :] + ind[tm - 1:tm, :].astype(jnp.int32)
    cnt = cnt_ref[...]
    cur = cur_ref[...]
    nfree = nused_ref[...]
    tiles_before = (cnt + (MOE_TILE - 1)) >> sh
    newf = ((cnt + n_new + (MOE_TILE - 1)) >> sh) - tiles_before
    newf8 = jnp.broadcast_to(newf.astype(BF16), (8, newf.shape[1]))
    pre = jnp.dot(newf8, upper_ref[...], preferred_element_type=F32)[0:1, :].astype(jnp.int32)
    new_tile = nfree + pre
    grank = cnt + lrank
    ptile = jnp.where((grank >> sh) < tiles_before, cur, new_tile)
    posfull = ((ptile << sh) + (grank & (MOE_TILE - 1))).astype(F32)
    pos1 = jnp.sum(jnp.where(oh1, posfull, 0.0), axis=-1, keepdims=True)
    pos2 = jnp.sum(jnp.where(oh2, posfull, 0.0), axis=-1, keepdims=True)
    tr = lax.broadcasted_iota(jnp.int32, (tm, tm), 0)
    tc = lax.broadcasted_iota(jnp.int32, (tm, tm), 1)
    as_row = lambda v: jnp.sum(jnp.where(tr == tc, jnp.broadcast_to(v, (tm, tm)), 0.0),
                               axis=0, keepdims=True)
    sub = lax.broadcasted_iota(jnp.int32, pos_ref.shape, 0)
    pos_ref[...] = jnp.where(sub == 0, as_row(pos1),
                             jnp.where(sub == 1, as_row(pos2), 0.0)).astype(jnp.int32)

    el_r = lax.broadcasted_iota(jnp.int32, (ROUTER_LANES, ROUTER_LANES), 0)
    el_c = lax.broadcasted_iota(jnp.int32, (ROUTER_LANES, ROUTER_LANES), 1)
    as_col = lambda v: jnp.sum(jnp.where(el_r == el_c, jnp.broadcast_to(v, el_r.shape), 0.0),
                               axis=1, keepdims=True)
    tile_col = as_col(jnp.where(newf > 0, new_tile, -1).astype(F32)).astype(jnp.int32)
    tlane = lax.broadcasted_iota(jnp.int32, (ROUTER_LANES, te_ref.shape[1]), 1)
    erow = lax.broadcasted_iota(jnp.int32, (ROUTER_LANES, te_ref.shape[1]), 0).astype(F32)
    te_ref[...] += jnp.sum(jnp.where(tlane == tile_col, erow, 0.0), axis=0,
                           keepdims=True).astype(jnp.int32)
    to_cur = jnp.minimum(n_new, (tiles_before << sh) - cnt)
    cur_col = as_col(cur.astype(F32)).astype(jnp.int32)
    added = (jnp.where(tlane == cur_col, as_col(to_cur.astype(F32)), 0.0)
             + jnp.where(tlane == tile_col, as_col((n_new - to_cur).astype(F32)), 0.0))
    fill_ref[...] += jnp.sum(added, axis=0, keepdims=True).astype(jnp.int32)
    cnt_ref[...] = cnt + n_new
    cur_ref[...] = jnp.where(newf > 0, new_tile, cur)
    nused_ref[...] = nfree + jnp.sum(newf.astype(F32), axis=-1, keepdims=True).astype(jnp.int32)


def _merge_kernel(x_ref, ya_ref, yb_ref, gates_ref, gluw_ref, glub_ref, wba_ref, wbb_ref, wout_ref,
                  fng_ref, rw_ref, rb_ref, tri_ref, upper_ref, x1_ref, rt_ref, pos_ref,
                  te_ref, fill_ref, nused_ref, xs_ref, cnt_ref, cur_ref, tbuf_even, tbuf_odd, posv,
                  pos_smem, fill_smem, zblk, rsem, psem, zsem):
    i = pl.program_id(0)
    last = pl.num_programs(0) - 1
    tm = x_ref.shape[0]
    tbufs = (tbuf_even, tbuf_odd)

    @pl.when(i == 0)
    def _():
        cnt_ref[...] = jnp.zeros_like(cnt_ref)
        cur_ref[...] = jnp.zeros_like(cur_ref)
        te_ref[...] = jnp.zeros_like(te_ref)
        fill_ref[...] = jnp.zeros_like(fill_ref)
        nused_ref[...] = jnp.zeros_like(nused_ref)

    def row_copies(par, unroll):
        def body(r, carry):
            for k in range(2):
                pltpu.async_copy(tbufs[par].at[pl.ds(r, 1), :],
                                 xs_ref.at[pl.ds(pos_smem[par, k, r], 1), :], rsem.at[par])
            return carry
        lax.fori_loop(0, tm, body, 0, unroll=unroll)

    def drain_rows(par):
        for _ in range(2):
            pltpu.make_async_copy(tbufs[par], xs_ref.at[pl.ds(0, tm), :], rsem.at[par]).wait()

    def pos_to_smem(par):
        return pltpu.make_async_copy(posv, pos_smem.at[pl.ds(par, 1)], psem)

    @pl.when(i >= 1)
    def _():
        pos_to_smem(0).wait()

    def step(par, dispatch_prev):
        @pl.when(i >= 2)
        def _():
            drain_rows(par)
        if dispatch_prev:
            row_copies(1 - par, True)
        _merge_tile_math(x_ref, ya_ref, yb_ref, gates_ref, gluw_ref, glub_ref, wba_ref, wbb_ref,
                         wout_ref, fng_ref, rw_ref, rb_ref, tri_ref, upper_ref, x1_ref,
                         tbufs[par], rt_ref, pos_ref.at[0], te_ref, fill_ref, nused_ref, cnt_ref,
                         cur_ref)
        posv[...] = pos_ref[...]
        pos_to_smem(par).start()

    pl.when(i == 0)(lambda: step(0, False))
    for par in range(2):
        pl.when((i > 0) & (i % 2 == par))(functools.partial(step, par, True))

    @pl.when(i == last)
    def _():
        pos_to_smem(0).wait()
        for par in range(2):
            @pl.when(i % 2 == par)
            def _(par=par):
                row_copies(par, DMA_UNROLL)

                @pl.when(i >= 1)
                def _():
                    drain_rows(1 - par)
                drain_rows(par)

        fill_copy = pltpu.make_async_copy(fill_ref, fill_smem, psem)
        fill_copy.start()
        zblk[...] = jnp.zeros_like(zblk)
        fill_copy.wait()

        def over_tails(op):
            def tile(j, carry):
                fill = fill_smem[0, j]
                head = (-fill) & (SUBLANES - 1)
                for r in range(SUBLANES - 1):
                    @pl.when(r < head)
                    def _(r=r):
                        op(pltpu.make_async_copy(zblk.at[pl.ds(0, 1), :],
                                                 xs_ref.at[pl.ds(j * MOE_TILE + fill + r, 1), :], zsem))
                start = fill + head
                todo = MOE_TILE - start
                size = MOE_TILE
                while size >= SUBLANES:
                    @pl.when((todo & size) != 0)
                    def _(start=start, size=size):
                        off = pl.multiple_of(j * MOE_TILE + start, SUBLANES)
                        op(pltpu.make_async_copy(zblk.at[pl.ds(0, size), :],
                                                 xs_ref.at[pl.ds(off, size), :], zsem))
                    start = start + (todo & size)
                    size //= 2
                return carry
            lax.fori_loop(0, xs_ref.shape[0] // MOE_TILE, tile, 0)

        over_tails(lambda c: c.start())
        over_tails(lambda c: c.wait())


def _moe_tiles(t):
    return (2 * t) // MOE_TILE + N_EXPERTS


def _merge(x2, ya, yb, gates, gluw, glub, wba, wbb, wout, fng, rw, rb):
    t, d = x2.shape
    tm = MERGE_TILE
    assert t % tm == 0 and tm <= MOE_TILE
    n_tiles = _moe_tiles(t)
    te_lanes = -(-n_tiles // LANES) * LANES
    rr = jnp.arange(tm)
    tri = (rr[None, :] < rr[:, None]).astype(BF16)
    ll = jnp.arange(ROUTER_LANES)
    upper = (ll[:, None] < ll[None, :]).astype(BF16)
    full = lambda a: pl.BlockSpec(a.shape, lambda i: (0,) * a.ndim)
    rowblk = lambda n: pl.BlockSpec((tm, n), lambda i: (i, 0))
    fixed = lambda n: pl.BlockSpec((1, n), lambda i: (0, 0))
    return pl.pallas_call(
        _merge_kernel,
        grid=(t // tm,),
        in_specs=[rowblk(d), rowblk(ya.shape[1]), rowblk(yb.shape[1]), rowblk(gates.shape[1]),
                  full(gluw), full(glub), full(wba), full(wbb), full(wout), full(fng), full(rw),
                  full(rb), full(tri), full(upper)],
        out_specs=[rowblk(d), rowblk(ROUTER_LANES),
                   pl.BlockSpec((1, SUBLANES, tm), lambda i: (i, 0, 0)),
                   fixed(te_lanes), fixed(te_lanes), fixed(ROUTER_LANES),
                   pl.BlockSpec(memory_space=pl.ANY)],
        out_shape=[jax.ShapeDtypeStruct((t, d), F32),
                   jax.ShapeDtypeStruct((t, ROUTER_LANES), F32),
                   jax.ShapeDtypeStruct((t // tm, SUBLANES, tm), jnp.int32),
                   jax.ShapeDtypeStruct((1, te_lanes), jnp.int32),
                   jax.ShapeDtypeStruct((1, te_lanes), jnp.int32),
                   jax.ShapeDtypeStruct((1, ROUTER_LANES), jnp.int32),
                   jax.ShapeDtypeStruct((n_tiles * MOE_TILE, d), F32)],
        scratch_shapes=[pltpu.VMEM((1, ROUTER_LANES), jnp.int32),
                        pltpu.VMEM((1, ROUTER_LANES), jnp.int32),
                        pltpu.VMEM((tm, d), F32),
                        pltpu.VMEM((tm, d), F32),
                        pltpu.VMEM((1, SUBLANES, tm), jnp.int32),
                        pltpu.SMEM((2, SUBLANES, tm), jnp.int32),
                        pltpu.SMEM((1, te_lanes), jnp.int32),
                        pltpu.VMEM((MOE_TILE, d), F32),
                        pltpu.SemaphoreType.DMA((2,)),
                        pltpu.SemaphoreType.DMA(()),
                        pltpu.SemaphoreType.DMA(())],
        compiler_params=pltpu.CompilerParams(dimension_semantics=("arbitrary",),
                                             vmem_limit_bytes=VMEM_LIMIT),
        name="merge",
    )(x2, ya, yb, gates, gluw, glub, wba, wbb, wout, fng, rw, rb, tri, upper)


def _experts_kernel(te_ref, nused_ref, xs_ref, wg_ref, wu_ref, wd_ref, y_ref):
    j = pl.program_id(0)

    @pl.when(j < nused_ref[0])
    def _():
        x = xs_ref[...].astype(BF16)
        hg = _mm(x, wg_ref[...])
        hid = hg * _sigmoid(hg) * _mm(x, wu_ref[...])
        y_ref[...] = _mm(hid, wd_ref[...])

    @pl.when(j >= nused_ref[0])
    def _():
        y_ref[...] = jnp.zeros_like(y_ref)


def _experts(tile_expert, n_used, xs, wg, wu, wd):
    n_rows, hw = xs.shape
    ne, d, de = wg.shape
    return pl.pallas_call(
        _experts_kernel,
        grid_spec=pltpu.PrefetchScalarGridSpec(
            num_scalar_prefetch=2,
            grid=(n_rows // MOE_TILE,),
            in_specs=[pl.BlockSpec((MOE_TILE, hw), lambda j, te, nu: (j, 0)),
                      pl.BlockSpec((None, d, de), lambda j, te, nu: (te[j], 0, 0)),
                      pl.BlockSpec((None, d, de), lambda j, te, nu: (te[j], 0, 0)),
                      pl.BlockSpec((None, de, d), lambda j, te, nu: (te[j], 0, 0))],
            out_specs=pl.BlockSpec((MOE_TILE, hw), lambda j, te, nu: (j, 0))),
        out_shape=jax.ShapeDtypeStruct((n_rows, hw), F32),
        compiler_params=pltpu.CompilerParams(dimension_semantics=("arbitrary",),
                                             vmem_limit_bytes=VMEM_LIMIT),
        name="experts",
    )(tile_expert, n_used, xs, wg, wu, wd)


def _ple_kernel(pos1_ref, pos2_ref, x_ref, rt_ref, p_ref, png_ref, wg_ref, wp_ref, fng_ref, y_ref,
                o_ref, ybuf_even, ybuf_odd, sem):
    i = pl.program_id(0)
    tm = x_ref.shape[0]
    ybufs = (ybuf_even, ybuf_odd)

    def gather(tile, par, unroll):
        def body(r, carry):
            for k, pos_ref in enumerate((pos1_ref, pos2_ref)):
                pltpu.async_copy(y_ref.at[pl.ds(pos_ref[tile * tm + r], 1), :],
                                 ybufs[par].at[k, pl.ds(r, 1), :], sem.at[par])
            return carry
        lax.fori_loop(0, tm, body, 0, unroll=unroll)

    @pl.when(i == 0)
    def _():
        gather(i, 0, DMA_UNROLL)

    def step(par, prefetch_next):
        ybuf = ybufs[par]
        for k in range(2):
            pltpu.make_async_copy(y_ref.at[pl.ds(0, tm), :], ybuf.at[k], sem.at[par]).wait()
        if prefetch_next:
            gather(i + 1, 1 - par, True)
        rt = rt_ref[...]
        x2 = x_ref[...] + rt[:, 0:1] * ybuf[0] + rt[:, 1:2] * ybuf[1]
        hp = _rms_norm(x2, png_ref[...])
        gate = _sigmoid(_mm(hp, wg_ref[...]))
        x3 = x2 + gate * _mm(p_ref[...], wp_ref[...])
        o_ref[...] = _rms_norm(x3, fng_ref[...])

    last = pl.num_programs(0) - 1
    for par in range(2):
        pl.when((i % 2 == par) & (i < last))(functools.partial(step, par, True))
        pl.when((i % 2 == par) & (i == last))(functools.partial(step, par, False))


def _ple(pos1, pos2, x1, rt, p2, png, wg, wp, fng, y_pack, tm):
    t, d = x1.shape
    hw = y_pack.shape[1]
    full = lambda a: pl.BlockSpec(a.shape, lambda i, p1, p2_: (0,) * a.ndim)
    rowblk = lambda n: pl.BlockSpec((tm, n), lambda i, p1, p2_: (i, 0))
    return pl.pallas_call(
        _ple_kernel,
        grid_spec=pltpu.PrefetchScalarGridSpec(
            num_scalar_prefetch=2,
            grid=(t // tm,),
            in_specs=[rowblk(d), rowblk(rt.shape[1]), rowblk(p2.shape[1]),
                      full(png), full(wg), full(wp), full(fng),
                      pl.BlockSpec(memory_space=pl.ANY)],
            out_specs=rowblk(d),
            scratch_shapes=[pltpu.VMEM((2, tm, hw), F32),
                            pltpu.VMEM((2, tm, hw), F32),
                            pltpu.SemaphoreType.DMA((2,))]),
        out_shape=jax.ShapeDtypeStruct((t, d), F32),
        compiler_params=pltpu.CompilerParams(dimension_semantics=("arbitrary",),
                                             vmem_limit_bytes=VMEM_LIMIT),
        name="ple",
    )(pos1, pos2, x1, rt, p2, png, wg, wp, fng, y_pack)


def _row_tile(t, want):
    tm = min(want, t)
    while t % tm:
        tm //= 2
    return tm


def _layer(x, p, mix_norm, w_in, mu_shift, rk_w0, rk_w_up, rk_a0, rk_a_up, rk_g_up,
           rk_k_k, rk_k_a, rk_r_k, rk_ln_g, rk_ln_b, s5_lam_re, s5_lam_im, s5_log_dt,
           s5_b_re, s5_b_im, s5_c_re, s5_c_im, s5_d, s5_glu_w, s5_glu_b,
           w_branch_a, w_branch_b, w_out, ffn_norm, router_group_w, router_group_b,
           router_expert_w, router_expert_b, exp_w_gate, exp_w_up, exp_w_down,
           ple_norm, ple_gate_w, ple_proj):
    b, s, d = x.shape
    t = b * s
    W = RWKV_WIDTH
    row = lambda a: a.reshape(1, -1).astype(F32)
    x2 = x.reshape(t, d)

    crw, us5, gates = _in_proj(x2, row(mix_norm), w_in.astype(BF16), _row_tile(t, 512))

    wl = jnp.zeros((LORA_COLS, 3 * W), F32)
    wl = wl.at[:DECAY_LORA, :W].set(rk_w_up)
    wl = wl.at[DECAY_LORA:DECAY_LORA + AAA_LORA, W:2 * W].set(rk_a_up)
    wl = wl.at[DECAY_LORA + AAA_LORA:, 2 * W:].set(rk_g_up)
    hid = jnp.arange(LANES) // RWKV_HEAD_DIM
    ones_bd = (hid[:, None] == hid[None, :]).astype(BF16)
    ya = _rwkv(crw.reshape(b, s, RWKV_COLS), row(mu_shift), wl.astype(BF16), row(rk_w0), row(rk_a0),
               row(rk_k_k), row(rk_k_a), row(rk_r_k), row(rk_ln_g), row(rk_ln_b), ones_bd,
               _rwkv_cum_matrix(_row_tile(s, RWKV_BLOCK)))

    s5_wa, s5_wo, plr, pli = _s5_mats(s5_lam_re, s5_lam_im, s5_log_dt, s5_b_re, s5_b_im,
                                      s5_c_re, s5_c_im, s5_d)
    yb = _s5(us5.reshape(b, s // S5_CHUNK, S5_CHUNK, S5_WIDTH), s5_wa, s5_wo, plr, pli)

    rw = jnp.zeros((d, ROUTER_LANES), F32)
    rw = rw.at[:, :N_EXPERTS].set(router_expert_w).at[:, N_EXPERTS:N_EXPERTS + N_GROUPS].set(router_group_w)
    rb = jnp.zeros((1, ROUTER_LANES), F32)
    rb = rb.at[0, :N_EXPERTS].set(router_expert_b).at[0, N_EXPERTS:N_EXPERTS + N_GROUPS].set(router_group_b)
    rw_hi = rw.astype(BF16)
    rw = jnp.concatenate([rw_hi, (rw - rw_hi.astype(F32)).astype(BF16)], axis=1)
    x1, rt, pos, tile_expert, _, n_used, xs = _merge(
        x2, ya.reshape(t, W), yb.reshape(t, S5_WIDTH), gates, s5_glu_w.astype(BF16), row(s5_glu_b),
        w_branch_a.astype(BF16), w_branch_b.astype(BF16), w_out.astype(BF16), row(ffn_norm), rw, rb)
    n_tiles = _moe_tiles(t)
    pos1, pos2 = pos[:, 0, :].reshape(t), pos[:, 1, :].reshape(t)
    y_pack = _experts(tile_expert[0, :n_tiles], n_used[0, :1], xs, exp_w_gate, exp_w_up, exp_w_down)
    return (pos1, pos2, x1, rt, p.reshape(t, -1), row(ple_norm), ple_gate_w.astype(BF16),
            ple_proj.astype(BF16), y_pack)


def kernel(x, p, mix_norm, w_in, mu_shift, rk_w0, rk_w_up, rk_a0, rk_a_up, rk_g_up, rk_k_k, rk_k_a,
           rk_r_k, rk_ln_g, rk_ln_b, s5_lam_re, s5_lam_im, s5_log_dt, s5_b_re, s5_b_im, s5_c_re,
           s5_c_im, s5_d, s5_glu_w, s5_glu_b, w_branch_a, w_branch_b, w_out, ffn_norm,
           router_group_w, router_group_b, router_expert_w, router_expert_b, exp_w_gate, exp_w_up,
           exp_w_down, ple_norm, ple_gate_w, ple_proj, final_norm):
    b, s, d = x.shape
    depth = w_in.shape[0]
    assert depth == 1, "the final norm is fused into the last layer's PLE kernel"
    i = 0
    pos1, pos2, x1, rt, p2, png, wpg, wpp, y_pack = _layer(
        x, p[i], mix_norm[i], w_in[i], mu_shift[i], rk_w0[i], rk_w_up[i], rk_a0[i], rk_a_up[i],
        rk_g_up[i], rk_k_k[i], rk_k_a[i], rk_r_k[i], rk_ln_g[i], rk_ln_b[i], s5_lam_re[i],
        s5_lam_im[i], s5_log_dt[i], s5_b_re[i], s5_b_im[i], s5_c_re[i], s5_c_im[i], s5_d[i],
        s5_glu_w[i], s5_glu_b[i], w_branch_a[i], w_branch_b[i], w_out[i], ffn_norm[i],
        router_group_w[i], router_group_b[i], router_expert_w[i], router_expert_b[i],
        exp_w_gate[i], exp_w_up[i], exp_w_down[i], ple_norm[i], ple_gate_w[i], ple_proj[i])
    out = _ple(pos1, pos2, x1, rt, p2, png, wpg, wpp, final_norm.reshape(1, -1).astype(F32), y_pack,
               _row_tile(b * s, MERGE_TILE))
    return out.reshape(b, s, d)
```

```python
import functools
import math

import jax
import jax.numpy as jnp
from jax import lax
from jax.experimental import pallas as pl
from jax.experimental.pallas import tpu as pltpu

F32 = jnp.float32
BF16 = jnp.bfloat16

NORM_EPS = 1e-6
GN_EPS = 64e-5

RWKV_HEADS = 8
RWKV_HEAD_DIM = 64
RWKV_WIDTH = RWKV_HEADS * RWKV_HEAD_DIM
DECAY_LORA = 64
AAA_LORA = 64
GATE_LORA = 128
LORA_COLS = DECAY_LORA + AAA_LORA + GATE_LORA
RWKV_COLS = 3 * RWKV_WIDTH + LORA_COLS
S5_GROUPS = 16
S5_GROUP_CH = 16
S5_WIDTH = S5_GROUPS * S5_GROUP_CH
S5_STATE = 64
S5_ZW = 2 * S5_GROUPS * S5_STATE
N_GROUPS = 4
EXPERTS_PER_GROUP = 8
N_EXPERTS = N_GROUPS * EXPERTS_PER_GROUP

LANES = 128
SUBLANES = 8
RWKV_CHUNK = 64
RWKV_INV_BLOCK = 16
RWKV_BLOCK = 256
S5_CHUNK = 8
ROUTER_LANES = 128
MOE_TILE = 512
MERGE_TILE = 512
DMA_UNROLL = 8
VMEM_LIMIT = 56 * 1024 * 1024


def _mm(a, b):
    return jnp.dot(a.astype(BF16), b.astype(BF16), preferred_element_type=F32)


def _mm_nt(a, b):
    return lax.dot_general(a.astype(BF16), b.astype(BF16), (((1,), (1,)), ((), ())),
                           preferred_element_type=F32)


def _mm_tn(a, b):
    return lax.dot_general(a.astype(BF16), b.astype(BF16), (((0,), (0,)), ((), ())),
                           preferred_element_type=F32)


def _split2(x):
    hi = x.astype(BF16)
    return hi, (x - hi.astype(F32)).astype(BF16)


def _mm_split_lhs(a_bf16, x):
    hi, lo = _split2(x)
    d = lambda t: jnp.dot(a_bf16, t, preferred_element_type=F32)
    return d(hi) + d(lo)


def _sigmoid(x):
    return 0.5 * jnp.tanh(0.5 * x) + 0.5


def _rms_norm(x, g):
    ms = jnp.mean(x * x, axis=-1, keepdims=True)
    return x * lax.rsqrt(ms + NORM_EPS) * g


def _in_proj_kernel(x_ref, g_ref, w_ref, crw_ref, us5_ref, gates_ref):
    h = _rms_norm(x_ref[...], g_ref[...]).astype(BF16)
    c0, c1 = RWKV_COLS, RWKV_COLS + S5_WIDTH
    crw_ref[...] = jnp.dot(h, w_ref[:, :c0], preferred_element_type=F32)
    us5_ref[...] = jnp.dot(h, w_ref[:, c0:c1], preferred_element_type=F32)
    gates_ref[...] = jnp.dot(h, w_ref[:, c1:], preferred_element_type=F32).astype(BF16)


def _in_proj(x2, g, w, tm):
    t, d = x2.shape
    n = w.shape[1]
    ng = n - RWKV_COLS - S5_WIDTH
    return pl.pallas_call(
        _in_proj_kernel,
        grid=(t // tm,),
        in_specs=[pl.BlockSpec((tm, d), lambda i: (i, 0)),
                  pl.BlockSpec((1, d), lambda i: (0, 0)),
                  pl.BlockSpec((d, n), lambda i: (0, 0))],
        out_specs=[pl.BlockSpec((tm, RWKV_COLS), lambda i: (i, 0)),
                   pl.BlockSpec((tm, S5_WIDTH), lambda i: (i, 0)),
                   pl.BlockSpec((tm, ng), lambda i: (i, 0))],
        out_shape=[jax.ShapeDtypeStruct((t, RWKV_COLS), F32),
                   jax.ShapeDtypeStruct((t, S5_WIDTH), F32),
                   jax.ShapeDtypeStruct((t, ng), BF16)],
        compiler_params=pltpu.CompilerParams(dimension_semantics=("arbitrary",),
                                             vmem_limit_bytes=VMEM_LIMIT),
        name="in_proj",
    )(x2, g, w)


def _rwkv_kernel(c_ref, mu_ref, wl_ref, w0_ref, a0_ref, kk_ref, ka_ref, rk_ref, lng_ref, lnb_ref,
                 ones_ref, cum_ref, o_ref, carry_ref, s_ref):
    C = RWKV_CHUNK
    W = RWKV_WIDTH
    npairs = W // LANES

    @pl.when(pl.program_id(1) == 0)
    def _():
        carry_ref[...] = jnp.zeros_like(carry_ref)
        s_ref[...] = jnp.zeros_like(s_ref)

    c = c_ref[...]
    R = c.shape[0]
    nchunks = R // C
    row = lax.broadcasted_iota(jnp.int32, (R, 1), 0)
    prev = jnp.where(row == 0, carry_ref[...], pltpu.roll(c, 1, 0))
    carry_ref[...] = c[R - 1:R, :]
    cs = c + (prev - c) * mu_ref[...]

    r = cs[:, 0:W]
    k = cs[:, W:2 * W]
    v = cs[:, 2 * W:3 * W]
    lin = cs[:, 3 * W:]
    llane = lax.broadcasted_iota(jnp.int32, lin.shape, 1)
    lact = jnp.where(llane < DECAY_LORA, jnp.tanh(lin),
                     jnp.where(llane < DECAY_LORA + AAA_LORA, lin, _sigmoid(lin)))
    lo = _mm(lact, wl_ref[...])
    zw = -(w0_ref[...] + lo[:, 0:W])
    softplus = jnp.maximum(zw, 0.0) + jnp.log(1.0 + jnp.exp(-jnp.abs(zw)))
    ld = -jnp.exp(-softplus - 0.5)
    a = _sigmoid(a0_ref[...] + lo[:, W:2 * W])
    g = lo[:, 2 * W:3 * W]

    ones_bd = ones_ref[...]
    segsum = lambda t: jnp.concatenate(
        [_mm(t[:, p * LANES:(p + 1) * LANES], ones_bd) for p in range(npairs)], axis=1)
    kk = k * kk_ref[...]
    kkn = kk / jnp.maximum(jnp.sqrt(segsum(kk * kk)), 1e-12)
    kmod = k * (1.0 + (a - 1.0) * ka_ref[...])

    cums = _mm_split_lhs(cum_ref[...], ld)
    cum = cums[:R]
    tot = cums[R:]
    inv = jnp.exp(-cum)
    tail = jnp.exp(tot - cum)
    At = -kkn * jnp.exp(cum - ld)
    Rt = r * jnp.exp(cum)
    kka = kkn * a
    Bt = kka * inv
    Kt = kmod * inv
    Bend = kka * tail
    Kend = kmod * tail
    pc = jnp.exp(tot)

    lane = lax.broadcasted_iota(jnp.int32, (C, LANES), 1)
    h0 = lane < RWKV_HEAD_DIM
    split = lambda t: jnp.concatenate([jnp.where(h0, t, 0.0), jnp.where(h0, 0.0, t)], axis=0)
    grow = lax.broadcasted_iota(jnp.int32, (C, 4 * C), 0)
    gcol = lax.broadcasted_iota(jnp.int32, (C, 4 * C), 1) & (C - 1)
    r2 = lax.broadcasted_iota(jnp.int32, (2 * C, 2 * C), 0)
    c2 = lax.broadcasted_iota(jnp.int32, (2 * C, 2 * C), 1)
    eye = (r2 == c2).astype(F32)
    blk_shift = int(math.log2(RWKV_INV_BLOCK))
    same_blk = (r2 >> blk_shift) == (c2 >> blk_shift)
    same_head = (r2 < C) == (c2 < C)
    same_head2 = jnp.concatenate([same_head, same_head], axis=0)
    zeros_c = jnp.zeros((C, LANES), F32)
    zeros_2c = jnp.zeros((2 * C, LANES), F32)

    units = [(ci, p) for ci in range(nchunks) for p in range(npairs)]
    blk = lambda t, u: t[u[0] * C:(u[0] + 1) * C, u[1] * LANES:(u[1] + 1) * LANES]
    each = lambda f, *ls: [f(*xs) for xs in zip(*ls)]

    lhs = [jnp.concatenate([blk(At, u), blk(Rt, u)], axis=0) for u in units]
    rhs = [jnp.concatenate([split(blk(Bt, u)), split(blk(Kt, u))], axis=0) for u in units]
    G = each(_mm_nt, lhs, rhs)
    a_row = [jnp.where(gcol < grow, t[:C], 0.0) for t in G]
    m_row = [jnp.where(gcol <= grow, t[C:], 0.0) for t in G]
    a_bd = [split(t[:, :2 * C]) for t in a_row]

    a_d = [jnp.where(same_blk, t, 0.0) for t in a_bd]
    a_off = each(lambda x, y: x - y, a_bd, a_d)
    dinv = [eye + t for t in a_d]
    pw = a_d
    for _ in range(blk_shift - 1):
        pw = each(_mm, pw, pw)
        dinv = each(lambda x, y: x + _mm(x, y), dinv, pw)
    n1 = each(_mm, dinv, a_off)
    n2 = each(_mm, n1, n1)
    n3 = each(_mm, n1, n2)
    tinv = each(lambda x1, x2, x3, dv: _mm(eye + x1 + x2 + x3, dv), n1, n2, n3, dinv)

    vp = [blk(v, u) for u in units]
    v_st = [split(t) for t in vp]
    rhs0 = each(lambda ar, vs: _mm(ar[:, 2 * C:], vs), a_row, v_st)
    wu = each(lambda t, l, r0: _mm(t, jnp.concatenate([split(l[:C]), split(r0)], axis=1)),
              tinv, lhs, rhs0)
    wu_lp = [t[:C] + t[C:] for t in wu]
    mn = each(lambda w_, v_, u: _mm_tn(
        jnp.concatenate([w_, jnp.concatenate([zeros_c, v_], axis=1)], axis=0),
        jnp.concatenate([blk(Bend, u), blk(Kend, u)], axis=0)), wu_lp, vp, units)
    mn = [jnp.where(same_head2, t, 0.0) for t in mn]
    qy = each(lambda m_, w_, vs: _mm(m_, jnp.concatenate(
        [w_, jnp.concatenate([zeros_2c, vs], axis=1)], axis=0)), m_row, wu, v_st)

    ys = [[None] * npairs for _ in range(nchunks)]
    states = [s_ref[p] for p in range(npairs)]
    for i, (ci, p) in enumerate(units):
        s_old = states[p]
        q = lhs[i][C:] + qy[i][:, :LANES]
        ys[ci][p] = _mm_nt(q, s_old) + qy[i][:, LANES:]
        states[p] = s_old * blk(pc, (ci, p))[0:1, :] + _mm(s_old, mn[i][:LANES]) + mn[i][LANES:]
    for p in range(npairs):
        s_ref[p] = states[p]

    y = jnp.concatenate([jnp.concatenate(t, axis=1) for t in ys], axis=0)
    inv_n = 1.0 / RWKV_HEAD_DIM
    mean = segsum(y) * inv_n
    d = y - mean
    var = segsum(d * d) * inv_n
    yn = d * lax.rsqrt(var + GN_EPS) * lng_ref[...] + lnb_ref[...]
    bonus = segsum(r * kmod * rk_ref[...]) * v
    o_ref[...] = ((yn + bonus) * g).astype(o_ref.dtype)


def _rwkv_cum_matrix(rows):
    t = jnp.arange(rows)
    same = (t[:, None] // RWKV_CHUNK) == (t[None, :] // RWKV_CHUNK)
    return jnp.concatenate([same & (t[None, :] <= t[:, None]), same], axis=0).astype(BF16)


def _rwkv(crw3, mu, wl, w0, a0, k_k, k_a, r_k, ln_g, ln_b, ones_bd, cum_mat):
    b, s, _ = crw3.shape
    R = cum_mat.shape[1]
    W = RWKV_WIDTH
    vec = lambda n: pl.BlockSpec((1, n), lambda i, j: (0, 0))
    return pl.pallas_call(
        _rwkv_kernel,
        grid=(b, s // R),
        in_specs=[pl.BlockSpec((None, R, RWKV_COLS), lambda i, j: (i, j, 0)),
                  vec(RWKV_COLS),
                  pl.BlockSpec((LORA_COLS, 3 * W), lambda i, j: (0, 0)),
                  vec(W), vec(W), vec(W), vec(W), vec(W), vec(W), vec(W),
                  pl.BlockSpec((LANES, LANES), lambda i, j: (0, 0)),
                  pl.BlockSpec((2 * R, R), lambda i, j: (0, 0))],
        out_specs=pl.BlockSpec((None, R, W), lambda i, j: (i, j, 0)),
        out_shape=jax.ShapeDtypeStruct((b, s, W), BF16),
        scratch_shapes=[pltpu.VMEM((1, RWKV_COLS), F32),
                        pltpu.VMEM((W // LANES, LANES, LANES), F32)],
        compiler_params=pltpu.CompilerParams(dimension_semantics=("arbitrary", "arbitrary"),
                                             vmem_limit_bytes=VMEM_LIMIT),
        name="rwkv",
    )(crw3, mu, wl, w0, a0, k_k, k_a, r_k, ln_g, ln_b, ones_bd, cum_mat)


def _s5_expand_kernel(are_ref, aim_ref, kf_ref, of_ref, wa_ref, wo_ref, *, groups, state, chans):
    rows = are_ref.shape[0]
    base = pl.program_id(0) * rows
    r = base + lax.broadcasted_iota(jnp.int32, (rows, LANES), 0)
    lane = lax.broadcasted_iota(jnp.int32, (rows, LANES), 1)
    in_group = (r // chans) % groups
    st_group = (r // state) % groups

    def emit(dst_ref, col0, src, row_group, per_group):
        per_tile = LANES // per_group
        tiles = groups // per_tile
        for m in range(src.shape[1] // LANES):
            vals = src[:, m * LANES:(m + 1) * LANES]
            for tl in range(tiles):
                c = col0 + (m * tiles + tl) * LANES
                keep = row_group == tl * per_tile + lane // per_group
                dst_ref[:, c:c + LANES] = jnp.where(keep, vals, 0.0).astype(dst_ref.dtype)

    half = groups * state
    emit(wa_ref, 0, are_ref[...], in_group, state)
    emit(wa_ref, half, aim_ref[...], in_group, state)
    emit(wa_ref, 2 * half, kf_ref[...], in_group, chans)
    emit(wo_ref, 0, of_ref[...], st_group, chans)


def _s5_expand(a_re, a_im, kf, of, groups, state, chans):
    rows = a_re.shape[0]
    assert kf.shape[0] == rows and of.shape[0] == rows
    tr = _row_tile(rows, 256)
    n_a = 2 * groups * state + kf.shape[1] // LANES * groups * chans
    n_o = of.shape[1] // LANES * groups * chans
    blk = lambda a: pl.BlockSpec((tr, a.shape[1]), lambda i: (i, 0))
    return pl.pallas_call(
        functools.partial(_s5_expand_kernel, groups=groups, state=state, chans=chans),
        grid=(rows // tr,),
        in_specs=[blk(a_re), blk(a_im), blk(kf), blk(of)],
        out_specs=[pl.BlockSpec((tr, n_a), lambda i: (i, 0)), pl.BlockSpec((tr, n_o), lambda i: (i, 0))],
        out_shape=[jax.ShapeDtypeStruct((rows, n_a), BF16), jax.ShapeDtypeStruct((rows, n_o), BF16)],
        compiler_params=pltpu.CompilerParams(dimension_semantics=("arbitrary",),
                                             vmem_limit_bytes=VMEM_LIMIT),
        name="s5_expand",
    )(a_re, a_im, kf, of)


def _s5_mats(lam_re, lam_im, log_dt, b_re, b_im, c_re, c_im, d_skip):
    L = S5_CHUNK
    G, N = lam_re.shape
    ch = b_re.shape[-1]
    dt = jnp.exp(log_dt)[:, None]
    lr, li = lam_re, lam_im
    mag = jnp.exp(lr * dt)
    lb_re, lb_im = mag * jnp.cos(li * dt), mag * jnp.sin(li * dt)
    den = lr * lr + li * li
    nr, ni = lb_re - 1.0, lb_im
    coef_re = (nr * lr + ni * li) / den
    coef_im = (ni * lr - nr * li) / den
    bb_re = coef_re[..., None] * b_re - coef_im[..., None] * b_im
    bb_im = coef_re[..., None] * b_im + coef_im[..., None] * b_re
    prs, pis = [jnp.ones_like(lb_re)], [jnp.zeros_like(lb_im)]
    for _ in range(L):
        pr_, pi_ = prs[-1], pis[-1]
        prs.append(pr_ * lb_re - pi_ * lb_im)
        pis.append(pr_ * lb_im + pi_ * lb_re)
    pr = jnp.stack(prs)
    pi = jnp.stack(pis)
    hp = lax.Precision.HIGHEST

    def lam_bb(qr, qi):
        return (qr[..., None] * bb_re[None] - qi[..., None] * bb_im[None],
                qr[..., None] * bb_im[None] + qi[..., None] * bb_re[None])

    def repeat_cols(a, inner, reps):
        k = a.shape[-1]
        src = jnp.arange(k)[:, None]
        dst = jnp.arange(k * reps)[None, :]
        rep = ((src // inner == dst // (inner * reps)) & (src % inner == dst % inner)).astype(F32)
        return jnp.dot(a, rep, precision=hp)

    in_rows = L * G * ch
    st_rows = 2 * G * N
    wre, wim = lam_bb(pr[:L][::-1], pi[:L][::-1])
    a_in = [repeat_cols(jnp.swapaxes(part, 2, 3).reshape(in_rows, N), N, LANES // N)
            for part in (wre, wim)]
    lre, lim = lam_bb(pr[:L], pi[:L])
    kern = (jnp.einsum('gon,lgni->lgio', c_re, lre, precision=hp)
            - jnp.einsum('gon,lgni->lgio', c_im, lim, precision=hp))
    kern = kern.at[0].add(d_skip[:, :, None] * jnp.eye(ch, dtype=F32)[None])
    lag = jnp.arange(L)[None, :] - jnp.arange(L)[:, None]
    kst = jnp.where((lag >= 0)[:, :, None, None, None], kern[jnp.maximum(lag, 0)], 0.0)
    kf = jnp.transpose(kst, (0, 2, 3, 1, 4)).reshape(in_rows, L * ch)
    kf = repeat_cols(kf, ch, LANES // ch)
    qr, qi = pr[1:L + 1][:, :, None, :], pi[1:L + 1][:, :, None, :]
    o_re = c_re[None] * qr - c_im[None] * qi
    o_im = -c_re[None] * qi - c_im[None] * qr
    of = jnp.transpose(jnp.stack([o_re, o_im]), (0, 2, 4, 1, 3))
    of = repeat_cols(of.reshape(st_rows, L * ch), ch, LANES // ch)
    w_a, w_out_flat = _s5_expand(a_in[0], a_in[1], kf, of, G, N, ch)
    plr = pr[L].reshape(1, G * N)
    pli = pi[L].reshape(1, G * N)
    return w_a, w_out_flat, plr, pli


def _s5_kernel(u_ref, wa_ref, wo_ref, plr_ref, pli_ref, o_ref, wloc_ref, zprev_ref):
    nch, L, w = u_ref.shape
    half = S5_ZW // 2

    r = _mm(u_ref[:, 0, :], wa_ref[0:w, :])
    for j in range(1, L):
        r = r + _mm(u_ref[:, j, :], wa_ref[j * w:(j + 1) * w, :])
    wloc_ref[...] = r[:, :S5_ZW]
    y_lag = r[:, S5_ZW:]

    plr = plr_ref[...]
    pli = pli_ref[...]

    def step(ci, z):
        zprev_ref[pl.ds(ci, 1), :] = z
        zr, zi = z[:, :half], z[:, half:]
        nz = jnp.concatenate([plr * zr - pli * zi, plr * zi + pli * zr], axis=1)
        return nz + wloc_ref[pl.ds(ci, 1), :]

    lax.fori_loop(0, nch, step, jnp.zeros((1, S5_ZW), F32))
    y = y_lag + _mm(zprev_ref[...], wo_ref[...])
    for j in range(L):
        o_ref[:, j, :] = y[:, j * w:(j + 1) * w]


def _s5(u4, w_a, w_o, plr, pli):
    b, nch, L, w = u4.shape
    const = lambda a: pl.BlockSpec(a.shape, lambda i: (0,) * a.ndim, pipeline_mode=pl.Buffered(1))
    blk = pl.BlockSpec((None, nch, L, w), lambda i: (i, 0, 0, 0))
    return pl.pallas_call(
        _s5_kernel,
        grid=(b,),
        in_specs=[blk, const(w_a), const(w_o), const(plr), const(pli)],
        out_specs=blk,
        out_shape=jax.ShapeDtypeStruct(u4.shape, F32),
        scratch_shapes=[pltpu.VMEM((nch, S5_ZW), F32),
                        pltpu.VMEM((nch, S5_ZW), F32)],
        compiler_params=pltpu.CompilerParams(dimension_semantics=("arbitrary",),
                                             vmem_limit_bytes=VMEM_LIMIT),
        name="s5",
    )(u4, w_a, w_o, plr, pli)


def _merge_tile_math(x_ref, ya_ref, yb_ref, gates_ref, gluw_ref, glub_ref, wba_ref, wbb_ref, wout_ref,
                     fng_ref, rw_ref, rb_ref, tri_ref, upper_ref, x1_ref, t_ref, rt_ref, pos_ref,
                     te_ref, fill_ref, nused_ref, cnt_ref, cur_ref):
    d = x_ref.shape[1]
    tm = x_ref.shape[0]
    y_a = jnp.dot(ya_ref[...], wba_ref[...], preferred_element_type=F32)
    ys = yb_ref[...]
    z = 0.5 * ys * (1.0 + jnp.tanh(math.sqrt(2.0 / math.pi) * (ys + 0.044715 * (ys * ys * ys))))
    z = z * _sigmoid(_mm(z, gluw_ref[...]) + glub_ref[...])
    y_b = _mm(z, wbb_ref[...])
    gates = gates_ref[...].astype(F32)
    merged = _sigmoid(gates[:, :d]) * y_a + _sigmoid(gates[:, d:]) * y_b
    x1 = x_ref[...] + _mm(merged, wout_ref[...])
    x1_ref[...] = x1
    t = _rms_norm(x1, fng_ref[...])
    t_hi = t.astype(BF16)
    t_ref[...] = t

    t_lo = (t - t_hi.astype(F32)).astype(BF16)
    hh_hl = jnp.dot(t_hi, rw_ref[...], preferred_element_type=F32)
    lh = jnp.dot(t_lo, rw_ref[:, :ROUTER_LANES], preferred_element_type=F32)
    logits = hh_hl[:, :ROUTER_LANES] + hh_hl[:, ROUTER_LANES:] + lh + rb_ref[...]
    lane = lax.broadcasted_iota(jnp.int32, logits.shape, 1)
    neg = -jnp.inf
    lane_f = lane.astype(F32)
    big = float(1 << 20)
    is_g = (lane >= N_EXPERTS) & (lane < N_EXPERTS + N_GROUPS)
    gl = jnp.where(is_g, logits, neg)
    gmax = jnp.max(gl, axis=-1, keepdims=True)
    g_p = 1.0 / jnp.sum(jnp.exp(gl - gmax), axis=-1, keepdims=True)
    g_idx = jnp.min(jnp.where(gl == gmax, lane_f - N_EXPERTS, big), axis=-1,
                    keepdims=True).astype(jnp.int32)
    el = jnp.where((lane < N_EXPERTS) & ((lane >> int(math.log2(EXPERTS_PER_GROUP))) == g_idx), logits, neg)
    t1 = jnp.max(el, axis=-1, keepdims=True)
    i1 = jnp.min(jnp.where(el == t1, lane_f, big), axis=-1, keepdims=True).astype(jnp.int32)
    el2 = jnp.where(lane == i1, neg, el)
    t2 = jnp.max(el2, axis=-1, keepdims=True)
    i2 = jnp.min(jnp.where(el2 == t2, lane_f, big), axis=-1, keepdims=True).astype(jnp.int32)
    e21 = jnp.exp(t2 - t1)
    w1 = g_p / (1.0 + e21)
    w2 = g_p * e21 / (1.0 + e21)
    rt_ref[...] = jnp.where(lane == 0, w1, jnp.where(lane == 1, w2, 0.0))

    sh = int(math.log2(MOE_TILE))
    oh1 = lane == i1
    oh2 = lane == i2
    ind = jnp.where(oh1 | oh2, 1.0, 0.0)
    lrank = jnp.dot(tri_ref[...], ind.astype(BF16), preferred_element_type=F32).astype(jnp.int32)
    n_new = lrank[tm - 1:tm, :] + ind[tm - 1:tm, :].astype(jnp.int32)
    cnt = cnt_ref[...]
    cur = cur_ref[...]
    nfree = nused_ref[...]
    tiles_before = (cnt + (MOE_TILE - 1)) >> sh
    newf = ((cnt + n_new + (MOE_TILE - 1)) >> sh) - tiles_before
    newf8 = jnp.broadcast_to(newf.astype(BF16), (8, newf.shape[1]))
    pre = jnp.dot(newf8, upper_ref[...], preferred_element_type=F32)[0:1, :].astype(jnp.int32)
    new_tile = nfree + pre
    grank = cnt + lrank
    ptile = jnp.where((grank >> sh) < tiles_before, cur, new_tile)
    posfull = ((ptile << sh) + (grank & (MOE_TILE - 1))).astype(F32)
    pos1 = jnp.sum(jnp.where(oh1, posfull, 0.0), axis=-1, keepdims=True)
    pos2 = jnp.sum(jnp.where(oh2, posfull, 0.0), axis=-1, keepdims=True)
    tr = lax.broadcasted_iota(jnp.int32, (tm, tm), 0)
    tc = lax.broadcasted_iota(jnp.int32, (tm, tm), 1)
    as_row = lambda v: jnp.sum(jnp.where(tr == tc, jnp.broadcast_to(v, (tm, tm)), 0.0),
                               axis=0, keepdims=True)
    sub = lax.broadcasted_iota(jnp.int32, pos_ref.shape, 0)
    pos_ref[...] = jnp.where(sub == 0, as_row(pos1),
                             jnp.where(sub == 1, as_row(pos2), 0.0)).astype(jnp.int32)

    el_r = lax.broadcasted_iota(jnp.int32, (ROUTER_LANES, ROUTER_LANES), 0)
    el_c = lax.broadcasted_iota(jnp.int32, (ROUTER_LANES, ROUTER_LANES), 1)
    as_col = lambda v: jnp.sum(jnp.where(el_r == el_c, jnp.broadcast_to(v, el_r.shape), 0.0),
                               axis=1, keepdims=True)
    tile_col = as_col(jnp.where(newf > 0, new_tile, -1).astype(F32)).astype(jnp.int32)
    tlane = lax.broadcasted_iota(jnp.int32, (ROUTER_LANES, te_ref.shape[1]), 1)
    erow = lax.broadcasted_iota(jnp.int32, (ROUTER_LANES, te_ref.shape[1]), 0).astype(F32)
    te_ref[...] += jnp.sum(jnp.where(tlane == tile_col, erow, 0.0), axis=0,
                           keepdims=True).astype(jnp.int32)
    to_cur = jnp.minimum(n_new, (tiles_before << sh) - cnt)
    cur_col = as_col(cur.astype(F32)).astype(jnp.int32)
    added = (jnp.where(tlane == cur_col, as_col(to_cur.astype(F32)), 0.0)
             + jnp.where(tlane == tile_col, as_col((n_new - to_cur).astype(F32)), 0.0))
    fill_ref[...] += jnp.sum(added, axis=0, keepdims=True).astype(jnp.int32)
    cnt_ref[...] = cnt + n_new
    cur_ref[...] = jnp.where(newf > 0, new_tile, cur)
    nused_ref[...] = nfree + jnp.sum(newf.astype(F32), axis=-1, keepdims=True).astype(jnp.int32)


def _merge_kernel(x_ref, ya_ref, yb_ref, gates_ref, gluw_ref, glub_ref, wba_ref, wbb_ref, wout_ref,
                  fng_ref, rw_ref, rb_ref, tri_ref, upper_ref, x1_ref, rt_ref, pos_ref,
                  te_ref, fill_ref, nused_ref, xs_ref, cnt_ref, cur_ref, tbuf_even, tbuf_odd, posv,
                  pos_smem, fill_smem, zblk, rsem, psem, zsem):
    i = pl.program_id(0)
    last = pl.num_programs(0) - 1
    tm = x_ref.shape[0]
    tbufs = (tbuf_even, tbuf_odd)

    @pl.when(i == 0)
    def _():
        cnt_ref[...] = jnp.zeros_like(cnt_ref)
        cur_ref[...] = jnp.zeros_like(cur_ref)
        te_ref[...] = jnp.zeros_like(te_ref)
        fill_ref[...] = jnp.zeros_like(fill_ref)
        nused_ref[...] = jnp.zeros_like(nused_ref)

    def row_copies(par, unroll):
        def body(r, carry):
            for k in range(2):
                pltpu.async_copy(tbufs[par].at[pl.ds(r, 1), :],
                                 xs_ref.at[pl.ds(pos_smem[par, k, r], 1), :], rsem.at[par], priority=k)
            return carry
        lax.fori_loop(0, tm, body, 0, unroll=unroll)

    def drain_rows(par):
        for _ in range(2):
            pltpu.make_async_copy(tbufs[par], xs_ref.at[pl.ds(0, tm), :], rsem.at[par]).wait()

    def pos_to_smem(par):
        return pltpu.make_async_copy(posv, pos_smem.at[pl.ds(par, 1)], psem)

    @pl.when(i >= 1)
    def _():
        pos_to_smem(0).wait()

    def step(par, dispatch_prev):
        @pl.when(i >= 2)
        def _():
            drain_rows(par)
        if dispatch_prev:
            row_copies(1 - par, True)
        _merge_tile_math(x_ref, ya_ref, yb_ref, gates_ref, gluw_ref, glub_ref, wba_ref, wbb_ref,
                         wout_ref, fng_ref, rw_ref, rb_ref, tri_ref, upper_ref, x1_ref,
                         tbufs[par], rt_ref, pos_ref.at[0], te_ref, fill_ref, nused_ref, cnt_ref,
                         cur_ref)
        posv[...] = pos_ref[...]
        pos_to_smem(par).start()

    pl.when(i == 0)(lambda: step(0, False))
    for par in range(2):
        pl.when((i > 0) & (i % 2 == par))(functools.partial(step, par, True))

    @pl.when(i == last)
    def _():
        pos_to_smem(0).wait()
        for par in range(2):
            @pl.when(i % 2 == par)
            def _(par=par):
                row_copies(par, DMA_UNROLL)

                @pl.when(i >= 1)
                def _():
                    drain_rows(1 - par)
                drain_rows(par)

        fill_copy = pltpu.make_async_copy(fill_ref, fill_smem, psem)
        fill_copy.start()
        zblk[...] = jnp.zeros_like(zblk)
        fill_copy.wait()

        def over_tails(op):
            def tile(j, carry):
                fill = fill_smem[0, j]
                head = (-fill) & (SUBLANES - 1)
                for r in range(SUBLANES - 1):
                    @pl.when(r < head)
                    def _(r=r):
                        op(pltpu.make_async_copy(zblk.at[pl.ds(0, 1), :],
                                                 xs_ref.at[pl.ds(j * MOE_TILE + fill + r, 1), :], zsem))
                start = fill + head
                todo = MOE_TILE - start
                size = MOE_TILE
                while size >= SUBLANES:
                    @pl.when((todo & size) != 0)
                    def _(start=start, size=size):
                        off = pl.multiple_of(j * MOE_TILE + start, SUBLANES)
                        op(pltpu.make_async_copy(zblk.at[pl.ds(0, size), :],
                                                 xs_ref.at[pl.ds(off, size), :], zsem))
                    start = start + (todo & size)
                    size //= 2
                return carry
            lax.fori_loop(0, xs_ref.shape[0] // MOE_TILE, tile, 0)

        over_tails(lambda c: c.start())
        over_tails(lambda c: c.wait())


def _moe_tiles(t):
    return (2 * t) // MOE_TILE + N_EXPERTS


def _merge(x2, ya, yb, gates, gluw, glub, wba, wbb, wout, fng, rw, rb):
    t, d = x2.shape
    tm = MERGE_TILE
    assert t % tm == 0 and tm <= MOE_TILE
    n_tiles = _moe_tiles(t)
    te_lanes = -(-n_tiles // LANES) * LANES
    rr = jnp.arange(tm)
    tri = (rr[None, :] < rr[:, None]).astype(BF16)
    ll = jnp.arange(ROUTER_LANES)
    upper = (ll[:, None] < ll[None, :]).astype(BF16)
    full = lambda a: pl.BlockSpec(a.shape, lambda i: (0,) * a.ndim)
    rowblk = lambda n: pl.BlockSpec((tm, n), lambda i: (i, 0))
    fixed = lambda n: pl.BlockSpec((1, n), lambda i: (0, 0))
    return pl.pallas_call(
        _merge_kernel,
        grid=(t // tm,),
        in_specs=[rowblk(d), rowblk(ya.shape[1]), rowblk(yb.shape[1]), rowblk(gates.shape[1]),
                  full(gluw), full(glub), full(wba), full(wbb), full(wout), full(fng), full(rw),
                  full(rb), full(tri), full(upper)],
        out_specs=[rowblk(d), rowblk(ROUTER_LANES),
                   pl.BlockSpec((1, SUBLANES, tm), lambda i: (i, 0, 0)),
                   fixed(te_lanes), fixed(te_lanes), fixed(ROUTER_LANES),
                   pl.BlockSpec(memory_space=pl.ANY)],
        out_shape=[jax.ShapeDtypeStruct((t, d), F32),
                   jax.ShapeDtypeStruct((t, ROUTER_LANES), F32),
                   jax.ShapeDtypeStruct((t // tm, SUBLANES, tm), jnp.int32),
                   jax.ShapeDtypeStruct((1, te_lanes), jnp.int32),
                   jax.ShapeDtypeStruct((1, te_lanes), jnp.int32),
                   jax.ShapeDtypeStruct((1, ROUTER_LANES), jnp.int32),
                   jax.ShapeDtypeStruct((n_tiles * MOE_TILE, d), F32)],
        scratch_shapes=[pltpu.VMEM((1, ROUTER_LANES), jnp.int32),
                        pltpu.VMEM((1, ROUTER_LANES), jnp.int32),
                        pltpu.VMEM((tm, d), F32),
                        pltpu.VMEM((tm, d), F32),
                        pltpu.VMEM((1, SUBLANES, tm), jnp.int32),
                        pltpu.SMEM((2, SUBLANES, tm), jnp.int32),
                        pltpu.SMEM((1, te_lanes), jnp.int32),
                        pltpu.VMEM((MOE_TILE, d), F32),
                        pltpu.SemaphoreType.DMA((2,)),
                        pltpu.SemaphoreType.DMA(()),
                        pltpu.SemaphoreType.DMA(())],
        compiler_params=pltpu.CompilerParams(dimension_semantics=("arbitrary",),
                                             vmem_limit_bytes=VMEM_LIMIT),
        name="merge",
    )(x2, ya, yb, gates, gluw, glub, wba, wbb, wout, fng, rw, rb, tri, upper)


def _experts_kernel(te_ref, nused_ref, xs_ref, wg_ref, wu_ref, wd_ref, y_ref):
    j = pl.program_id(0)

    @pl.when(j < nused_ref[0])
    def _():
        x = xs_ref[...].astype(BF16)
        hg = _mm(x, wg_ref[...])
        hid = hg * _sigmoid(hg) * _mm(x, wu_ref[...])
        y_ref[...] = _mm(hid, wd_ref[...])

    @pl.when(j >= nused_ref[0])
    def _():
        y_ref[...] = jnp.zeros_like(y_ref)


def _experts(tile_expert, n_used, xs, wg, wu, wd):
    n_rows, hw = xs.shape
    ne, d, de = wg.shape
    return pl.pallas_call(
        _experts_kernel,
        grid_spec=pltpu.PrefetchScalarGridSpec(
            num_scalar_prefetch=2,
            grid=(n_rows // MOE_TILE,),
            in_specs=[pl.BlockSpec((MOE_TILE, hw), lambda j, te, nu: (j, 0)),
                      pl.BlockSpec((None, d, de), lambda j, te, nu: (te[j], 0, 0)),
                      pl.BlockSpec((None, d, de), lambda j, te, nu: (te[j], 0, 0)),
                      pl.BlockSpec((None, de, d), lambda j, te, nu: (te[j], 0, 0))],
            out_specs=pl.BlockSpec((MOE_TILE, hw), lambda j, te, nu: (j, 0))),
        out_shape=jax.ShapeDtypeStruct((n_rows, hw), F32),
        compiler_params=pltpu.CompilerParams(dimension_semantics=("arbitrary",),
                                             vmem_limit_bytes=VMEM_LIMIT),
        name="experts",
    )(tile_expert, n_used, xs, wg, wu, wd)


def _ple_kernel(pos1_ref, pos2_ref, x_ref, rt_ref, p_ref, png_ref, wg_ref, wp_ref, fng_ref, y_ref,
                o_ref, ybuf_even, ybuf_odd, sem):
    i = pl.program_id(0)
    tm = x_ref.shape[0]
    ybufs = (ybuf_even, ybuf_odd)

    def gather(tile, par, unroll):
        def body(r, carry):
            for k, pos_ref in enumerate((pos1_ref, pos2_ref)):
                pltpu.async_copy(y_ref.at[pl.ds(pos_ref[tile * tm + r], 1), :],
                                 ybufs[par].at[k, pl.ds(r, 1), :], sem.at[par], priority=k)
            return carry
        lax.fori_loop(0, tm, body, 0, unroll=unroll)

    @pl.when(i == 0)
    def _():
        gather(i, 0, DMA_UNROLL)

    def step(par, prefetch_next):
        ybuf = ybufs[par]
        for k in range(2):
            pltpu.make_async_copy(y_ref.at[pl.ds(0, tm), :], ybuf.at[k], sem.at[par]).wait()
        if prefetch_next:
            gather(i + 1, 1 - par, True)
        rt = rt_ref[...]
        x2 = x_ref[...] + rt[:, 0:1] * ybuf[0] + rt[:, 1:2] * ybuf[1]
        hp = _rms_norm(x2, png_ref[...])
        gate = _sigmoid(_mm(hp, wg_ref[...]))
        x3 = x2 + gate * _mm(p_ref[...], wp_ref[...])
        o_ref[...] = _rms_norm(x3, fng_ref[...])

    last = pl.num_programs(0) - 1
    for par in range(2):
        pl.when((i % 2 == par) & (i < last))(functools.partial(step, par, True))
        pl.when((i % 2 == par) & (i == last))(functools.partial(step, par, False))


def _ple(pos1, pos2, x1, rt, p2, png, wg, wp, fng, y_pack, tm):
    t, d = x1.shape
    hw = y_pack.shape[1]
    full = lambda a: pl.BlockSpec(a.shape, lambda i, p1, p2_: (0,) * a.ndim)
    rowblk = lambda n: pl.BlockSpec((tm, n), lambda i, p1, p2_: (i, 0))
    return pl.pallas_call(
        _ple_kernel,
        grid_spec=pltpu.PrefetchScalarGridSpec(
            num_scalar_prefetch=2,
            grid=(t // tm,),
            in_specs=[rowblk(d), rowblk(rt.shape[1]), rowblk(p2.shape[1]),
                      full(png), full(wg), full(wp), full(fng),
                      pl.BlockSpec(memory_space=pl.ANY)],
            out_specs=rowblk(d),
            scratch_shapes=[pltpu.VMEM((2, tm, hw), F32),
                            pltpu.VMEM((2, tm, hw), F32),
                            pltpu.SemaphoreType.DMA((2,))]),
        out_shape=jax.ShapeDtypeStruct((t, d), F32),
        compiler_params=pltpu.CompilerParams(dimension_semantics=("arbitrary",),
                                             vmem_limit_bytes=VMEM_LIMIT),
        name="ple",
    )(pos1, pos2, x1, rt, p2, png, wg, wp, fng, y_pack)


def _row_tile(t, want):
    tm = min(want, t)
    while t % tm:
        tm //= 2
    return tm


def _layer(x, p, mix_norm, w_in, mu_shift, rk_w0, rk_w_up, rk_a0, rk_a_up, rk_g_up,
           rk_k_k, rk_k_a, rk_r_k, rk_ln_g, rk_ln_b, s5_lam_re, s5_lam_im, s5_log_dt,
           s5_b_re, s5_b_im, s5_c_re, s5_c_im, s5_d, s5_glu_w, s5_glu_b,
           w_branch_a, w_branch_b, w_out, ffn_norm, router_group_w, router_group_b,
           router_expert_w, router_expert_b, exp_w_gate, exp_w_up, exp_w_down,
           ple_norm, ple_gate_w, ple_proj):
    b, s, d = x.shape
    t = b * s
    W = RWKV_WIDTH
    row = lambda a: a.reshape(1, -1).astype(F32)
    x2 = x.reshape(t, d)

    crw, us5, gates = _in_proj(x2, row(mix_norm), w_in.astype(BF16), _row_tile(t, 512))

    wl = jnp.zeros((LORA_COLS, 3 * W), F32)
    wl = wl.at[:DECAY_LORA, :W].set(rk_w_up)
    wl = wl.at[DECAY_LORA:DECAY_LORA + AAA_LORA, W:2 * W].set(rk_a_up)
    wl = wl.at[DECAY_LORA + AAA_LORA:, 2 * W:].set(rk_g_up)
    hid = jnp.arange(LANES) // RWKV_HEAD_DIM
    ones_bd = (hid[:, None] == hid[None, :]).astype(BF16)
    ya = _rwkv(crw.reshape(b, s, RWKV_COLS), row(mu_shift), wl.astype(BF16), row(rk_w0), row(rk_a0),
               row(rk_k_k), row(rk_k_a), row(rk_r_k), row(rk_ln_g), row(rk_ln_b), ones_bd,
               _rwkv_cum_matrix(_row_tile(s, RWKV_BLOCK)))

    s5_wa, s5_wo, plr, pli = _s5_mats(s5_lam_re, s5_lam_im, s5_log_dt, s5_b_re, s5_b_im,
                                      s5_c_re, s5_c_im, s5_d)
    yb = _s5(us5.reshape(b, s // S5_CHUNK, S5_CHUNK, S5_WIDTH), s5_wa, s5_wo, plr, pli)

    rw = jnp.zeros((d, ROUTER_LANES), F32)
    rw = rw.at[:, :N_EXPERTS].set(router_expert_w).at[:, N_EXPERTS:N_EXPERTS + N_GROUPS].set(router_group_w)
    rb = jnp.zeros((1, ROUTER_LANES), F32)
    rb = rb.at[0, :N_EXPERTS].set(router_expert_b).at[0, N_EXPERTS:N_EXPERTS + N_GROUPS].set(router_group_b)
    rw_hi = rw.astype(BF16)
    rw = jnp.concatenate([rw_hi, (rw - rw_hi.astype(F32)).astype(BF16)], axis=1)
    x1, rt, pos, tile_expert, _, n_used, xs = _merge(
        x2, ya.reshape(t, W), yb.reshape(t, S5_WIDTH), gates, s5_glu_w.astype(BF16), row(s5_glu_b),
        w_branch_a.astype(BF16), w_branch_b.astype(BF16), w_out.astype(BF16), row(ffn_norm), rw, rb)
    n_tiles = _moe_tiles(t)
    pos1, pos2 = pos[:, 0, :].reshape(t), pos[:, 1, :].reshape(t)
    y_pack = _experts(tile_expert[0, :n_tiles], n_used[0, :1], xs, exp_w_gate, exp_w_up, exp_w_down)
    return (pos1, pos2, x1, rt, p.reshape(t, -1), row(ple_norm), ple_gate_w.astype(BF16),
            ple_proj.astype(BF16), y_pack)


def kernel(x, p, mix_norm, w_in, mu_shift, rk_w0, rk_w_up, rk_a0, rk_a_up, rk_g_up, rk_k_k, rk_k_a,
           rk_r_k, rk_ln_g, rk_ln_b, s5_lam_re, s5_lam_im, s5_log_dt, s5_b_re, s5_b_im, s5_c_re,
           s5_c_im, s5_d, s5_glu_w, s5_glu_b, w_branch_a, w_branch_b, w_out, ffn_norm,
           router_group_w, router_group_b, router_expert_w, router_expert_b, exp_w_gate, exp_w_up,
           exp_w_down, ple_norm, ple_gate_w, ple_proj, final_norm):
    b, s, d = x.shape
    depth = w_in.shape[0]
    assert depth == 1, "the final norm is fused into the last layer's PLE kernel"
    i = 0
    pos1, pos2, x1, rt, p2, png, wpg, wpp, y_pack = _layer(
        x, p[i], mix_norm[i], w_in[i], mu_shift[i], rk_w0[i], rk_w_up[i], rk_a0[i], rk_a_up[i],
        rk_g_up[i], rk_k_k[i], rk_k_a[i], rk_r_k[i], rk_ln_g[i], rk_ln_b[i], s5_lam_re[i],
        s5_lam_im[i], s5_log_dt[i], s5_b_re[i], s5_b_im[i], s5_c_re[i], s5_c_im[i], s5_d[i],
        s5_glu_w[i], s5_glu_b[i], w_branch_a[i], w_branch_b[i], w_out[i], ffn_norm[i],
        router_group_w[i], router_group_b[i], router_expert_w[i], router_expert_b[i],
        exp_w_gate[i], exp_w_up[i], exp_w_down[i], ple_norm[i], ple_gate_w[i], ple_proj[i])
    out = _ple(pos1, pos2, x1, rt, p2, png, wpg, wpp, final_norm.reshape(1, -1).astype(F32), y_pack,
               _row_tile(b * s, MERGE_TILE))
    return out.reshape(b, s, d)
```
